```python
import math
import jax, jax.numpy as jnp
from jax import lax
import numpy as np

D_MODEL = 1024
BATCH = 8
SEQ = 8192
DEPTH = 4

CHUNK = 64
Q_BLOCK = 128
HEAD_DIM = 64
BRANCH_WIDTH = 512
N_BRANCH = 3
A_HEADS = BRANCH_WIDTH // (2 * HEAD_DIM)
B_HEADS = BRANCH_WIDTH // HEAD_DIM
C_HEADS = 4
C_HEAD_DIM = BRANCH_WIDTH // C_HEADS
B_LEFT_CHUNKS = 8
B_BAND = (B_LEFT_CHUNKS + 1) * CHUNK
B_MAX_REL = 128
T5_BUCKETS = 32
T5_MAX_DIST = 128
D_FF = -(-(8 * D_MODEL) // (3 * 256)) * 256
IN_COLS = 3 * N_BRANCH * BRANCH_WIDTH + N_BRANCH * D_MODEL
SPLITS = tuple(BRANCH_WIDTH * i for i in range(1, 3 * N_BRANCH + 1))
NEG_INF = -1e30
EPS = 1e-6

kernel_name = 'hybrid_diff_band_stickbreak_encoder'


def rms_norm(x, g):
    xf = x.astype(jnp.float32)
    y = xf * lax.rsqrt(jnp.mean(xf * xf, axis=-1, keepdims=True) + EPS)
    return (y * g.astype(jnp.float32)).astype(x.dtype)


def to_blocks(t, n):
    b, s, h, d = t.shape
    return t.reshape(b, s // n, n, h, d).transpose(1, 0, 2, 3, 4)


def from_blocks(t):
    nb, b, n, h, d = t.shape
    return t.transpose(1, 0, 2, 3, 4).reshape(b, nb * n, h, d)


def t5_bucket(rel):
    nb = T5_BUCKETS // 2
    max_exact = nb // 2
    ret = jnp.where(rel > 0, nb, 0)
    n = jnp.abs(rel)
    nf = jnp.maximum(n, 1).astype(jnp.float32)
    large = max_exact + (jnp.log(nf / max_exact) / math.log(T5_MAX_DIST / max_exact)
                         * (nb - max_exact)).astype(jnp.int32)
    large = jnp.minimum(large, nb - 1)
    return ret + jnp.where(n < max_exact, n, large)


def diff_attention(q, k, v, lam, t5_table):
    seq = q.shape[1]
    q = q * (HEAD_DIM ** -0.5)
    outs = []
    for i in range(seq // Q_BLOCK):
        k_end = (i + 1) * Q_BLOCK
        qb = q[:, i * Q_BLOCK:k_end]
        kb, vb = k[:, :k_end], v[:, :k_end]
        qpos = i * Q_BLOCK + jnp.arange(Q_BLOCK)
        kpos = jnp.arange(k_end)
        allowed = (kpos[None, :] // CHUNK) <= (qpos[:, None] // CHUNK)
        bias = jnp.transpose(t5_table[t5_bucket(kpos[None, :] - qpos[:, None])], (2, 0, 1))
        bias = jnp.where(allowed[None], bias.astype(jnp.float32), NEG_INF)
        s = jnp.einsum('bqhmd,bkhmd->bhmqk', qb, kb).astype(jnp.float32) + bias[:, None]
        e = jnp.exp(s - jnp.max(s, axis=-1, keepdims=True))
        den = jnp.transpose(jnp.sum(e, axis=-1), (0, 3, 1, 2))[..., None]
        o = jnp.einsum('bhmqk,bkhe->bqhme', e.astype(vb.dtype), vb) / den.astype(vb.dtype)
        outs.append(o[..., 0, :] - lam * o[..., 1, :])
    return jnp.concatenate(outs, axis=1)


def chunk_band_attention(q, k, v, rel_bias):
    seq = q.shape[1]
    nc = seq // CHUNK
    pad = B_LEFT_CHUNKS * CHUNK
    scale = HEAD_DIM ** -0.5
    kp = jnp.pad(k, ((0, 0), (pad, 0), (0, 0), (0, 0)))
    vp = jnp.pad(v, ((0, 0), (pad, 0), (0, 0), (0, 0)))
    i = jnp.arange(CHUNK)
    j = jnp.arange(B_BAND)
    rel = jnp.clip((i[:, None] + pad) - j[None, :], -B_MAX_REL, B_MAX_REL) + B_MAX_REL
    bias = rel_bias[:, rel].astype(jnp.float32)

    def block(args):
        qb, n = args
        kb = lax.dynamic_slice_in_dim(kp, n * CHUNK, B_BAND, axis=1)
        vb = lax.dynamic_slice_in_dim(vp, n * CHUNK, B_BAND, axis=1)
        valid = (n * CHUNK - pad + j) >= 0
        s = jnp.einsum('bqhd,bkhd->bhqk', qb, kb).astype(jnp.float32) * scale + bias
        p = jax.nn.softmax(jnp.where(valid, s, NEG_INF), axis=-1)
        return jnp.einsum('bhqk,bkhd->bqhd', p.astype(vb.dtype), vb)

    out = lax.map(block, (to_blocks(q, CHUNK), jnp.arange(nc)))
    return from_blocks(out)


def stick_breaking_attention(q, k, v):
    seq = q.shape[1]
    q = q * (C_HEAD_DIM ** -0.5)
    incl_within = jnp.tri(Q_BLOCK, dtype=jnp.float32)
    outs = []
    for i in range(seq // Q_BLOCK):
        n_kb = i + 1
        k_end = n_kb * Q_BLOCK
        qb = q[:, i * Q_BLOCK:k_end]
        kb, vb = k[:, :k_end], v[:, :k_end]
        qpos = i * Q_BLOCK + jnp.arange(Q_BLOCK)
        kpos = jnp.arange(k_end)
        before = kpos[None, :] < qpos[:, None]
        z = jnp.einsum('bqhd,bkhd->bhqk', qb, kb).astype(jnp.float32)
        lm = jnp.where(before, jax.nn.log_sigmoid(-z), 0.0)
        bsz, nh = lm.shape[0], lm.shape[1]
        lm5 = lm.reshape(bsz, nh, Q_BLOCK, n_kb, Q_BLOCK)
        within = jnp.einsum('bhqnj,js->bhqns', lm5, incl_within)
        later = jnp.einsum('bhqm,mn->bhqn', jnp.sum(lm5, axis=-1),
                           jnp.tri(n_kb, k=-1, dtype=jnp.float32))
        log_w = z + (within + later[..., None]).reshape(bsz, nh, Q_BLOCK, k_end)
        w = jnp.exp(jnp.where(before, log_w, NEG_INF))
        outs.append(jnp.einsum('bhqk,bkhd->bqhd', w.astype(vb.dtype), vb))
    return jnp.concatenate(outs, axis=1)


def setup_inputs(seed: int = 0) -> dict:
    key = jax.random.key(seed)
    ks = jax.random.split(key, 20)
    f32 = jnp.float32
    nrm = lambda k, shape, s: jax.random.normal(k, shape, f32) * s
    return {
        'x': nrm(ks[0], (BATCH, SEQ, D_MODEL), 1.0),
        'c': nrm(ks[1], (BATCH, D_MODEL), 1.0),
        'w_ada': nrm(ks[2], (DEPTH, D_MODEL, 6 * D_MODEL), 0.5 * D_MODEL ** -0.5),
        'b_ada': nrm(ks[3], (DEPTH, 6 * D_MODEL), 0.02),
        'norm1_g': 1.0 + nrm(ks[4], (DEPTH, D_MODEL), 0.02),
        'w_in': nrm(ks[5], (DEPTH, D_MODEL, IN_COLS), D_MODEL ** -0.5),
        'b_gate': nrm(ks[6], (DEPTH, N_BRANCH * D_MODEL), 0.02),
        'lam_q1': nrm(ks[7], (DEPTH, HEAD_DIM), 0.1),
        'lam_k1': nrm(ks[8], (DEPTH, HEAD_DIM), 0.1),
        'lam_q2': nrm(ks[9], (DEPTH, HEAD_DIM), 0.1),
        'lam_k2': nrm(ks[10], (DEPTH, HEAD_DIM), 0.1),
        'subln_g': 1.0 + nrm(ks[11], (DEPTH, 2 * HEAD_DIM), 0.02),
        't5_table': nrm(ks[12], (T5_BUCKETS, A_HEADS), 0.2),
        'rel_bias_b': nrm(ks[13], (DEPTH, B_HEADS, 2 * B_MAX_REL + 1), 0.2),
        'w_branch': nrm(ks[14], (DEPTH, N_BRANCH, BRANCH_WIDTH, D_MODEL), BRANCH_WIDTH ** -0.5),
        'w_out': nrm(ks[15], (DEPTH, D_MODEL, D_MODEL), D_MODEL ** -0.5),
        'norm2_g': 1.0 + nrm(ks[16], (DEPTH, D_MODEL), 0.02),
        'w13': nrm(ks[17], (DEPTH, D_MODEL, 2 * D_FF), D_MODEL ** -0.5),
        'w2': nrm(ks[18], (DEPTH, D_FF, D_MODEL), D_FF ** -0.5),
        'final_g': 1.0 + nrm(ks[19], (D_MODEL,), 0.02),
    }


def reference(x, c, w_ada, b_ada, norm1_g, w_in, b_gate, lam_q1, lam_k1, lam_q2, lam_k2,
              subln_g, t5_table, rel_bias_b, w_branch, w_out, norm2_g, w13, w2, final_g):
    bsz, seq, _ = x.shape
    cs = jax.nn.silu(c)
    for l in range(DEPTH):
        mod = jnp.einsum('bd,de->be', cs, w_ada[l]) + b_ada[l]
        sh1, sc1, g1, sh2, sc2, g2 = [m[:, None, :] for m in jnp.split(mod, 6, axis=-1)]

        h = rms_norm(x, norm1_g[l]) * (1 + sc1) + sh1
        proj = jnp.einsum('bsd,de->bse', h, w_in[l])
        aq, ak, av, bq, bk, bv, cq, ck, cv, gate_pre = jnp.split(proj, SPLITS, axis=-1)

        lam_init = 0.8 - 0.6 * math.exp(-0.3 * l)
        lam = (jnp.exp(jnp.sum(lam_q1[l].astype(jnp.float32) * lam_k1[l].astype(jnp.float32)))
               - jnp.exp(jnp.sum(lam_q2[l].astype(jnp.float32) * lam_k2[l].astype(jnp.float32)))
               + lam_init)
        ya = diff_attention(aq.reshape(bsz, seq, A_HEADS, 2, HEAD_DIM),
                            ak.reshape(bsz, seq, A_HEADS, 2, HEAD_DIM),
                            av.reshape(bsz, seq, A_HEADS, 2 * HEAD_DIM),
                            lam.astype(av.dtype), t5_table)
        ya = (rms_norm(ya, subln_g[l]) * (1 - lam_init)).reshape(bsz, seq, BRANCH_WIDTH)

        yb = chunk_band_attention(bq.reshape(bsz, seq, B_HEADS, HEAD_DIM),
                                  bk.reshape(bsz, seq, B_HEADS, HEAD_DIM),
                                  bv.reshape(bsz, seq, B_HEADS, HEAD_DIM),
                                  rel_bias_b[l]).reshape(bsz, seq, BRANCH_WIDTH)

        yc = stick_breaking_attention(cq.reshape(bsz, seq, C_HEADS, C_HEAD_DIM),
                                      ck.reshape(bsz, seq, C_HEADS, C_HEAD_DIM),
                                      cv.reshape(bsz, seq, C_HEADS, C_HEAD_DIM)).reshape(bsz, seq, BRANCH_WIDTH)

        gates = jax.nn.sigmoid(gate_pre + b_gate[l]).reshape(bsz, seq, N_BRANCH, D_MODEL)
        merged = gates[:, :, 0] * jnp.einsum('bsw,wd->bsd', ya, w_branch[l, 0])
        merged = merged + gates[:, :, 1] * jnp.einsum('bsw,wd->bsd', yb, w_branch[l, 1])
        merged = merged + gates[:, :, 2] * jnp.einsum('bsw,wd->bsd', yc, w_branch[l, 2])
        x = x + g1 * jnp.einsum('bsd,de->bse', merged, w_out[l])

        h2 = rms_norm(x, norm2_g[l]) * (1 + sc2) + sh2
        u_gate, u_up = jnp.split(jnp.einsum('bsd,df->bsf', h2, w13[l]), 2, axis=-1)
        x = x + g2 * jnp.einsum('bsf,fd->bsd', jax.nn.silu(u_gate) * u_up, w2[l])
    return rms_norm(x, final_g)
```

```python
import functools
import math

import numpy as np
import jax
import jax.numpy as jnp
from jax import lax
from jax.experimental import pallas as pl
from jax.experimental.pallas import tpu as pltpu

F32 = jnp.float32
BF16 = jnp.bfloat16

D_MODEL = 1024
DEPTH = 4
CHUNK = 64
HEAD_DIM = 64
BRANCH_WIDTH = 512
N_BRANCH = 3
A_HEADS = 4
B_HEADS = 8
C_HEADS = 4
C_HEAD_DIM = 128
B_LEFT_CHUNKS = 8
B_PAD = B_LEFT_CHUNKS * CHUNK
B_MAX_REL = 128
T5_BUCKETS = 32
T5_MAX_DIST = 128
D_FF = 2816
QKV_COLS = 3 * N_BRANCH * BRANCH_WIDTH
GATE_COLS = N_BRANCH * D_MODEL
IN_COLS = QKV_COLS + GATE_COLS
NEG_INF = -1e30
EPS = 1e-6

LANES = 128
N_COLBLK = IN_COLS // LANES
GATE_BLKS = GATE_COLS // LANES
AQ0, AK0, AV0 = GATE_BLKS, GATE_BLKS + 4, GATE_BLKS + 8
BQ0, BK0, BV0 = GATE_BLKS + 12, GATE_BLKS + 16, GATE_BLKS + 20
CQ0, CK0, CV0 = GATE_BLKS + 24, GATE_BLKS + 28, GATE_BLKS + 32

TM_PROJ = 1024
TN_PROJ = 1536
TM_MERGE = 512
TM_FFN = 512
TF_FFN = 1408
TQ_A = 256
RQ_B = 128
TQ_C = 256
C_SKIP_LOG = -100.0
VMEM_LIMIT = 56 * 1024 * 1024


def _cparams(*sem):
    return pltpu.CompilerParams(dimension_semantics=sem, vmem_limit_bytes=VMEM_LIMIT)


def _ada_kernel(c_ref, w_ref, b_ref, o_ref):
    c = c_ref[...]
    cs = c * jax.nn.sigmoid(c)
    o_ref[...] = jnp.dot(cs, w_ref[...], preferred_element_type=F32,
                         precision=lax.Precision.HIGHEST) + b_ref[...]


def _ada_mod(c, w_ada, b_ada):
    depth, d, e = w_ada.shape
    bsz = c.shape[0]
    nblk = e // d
    return pl.pallas_call(
        _ada_kernel,
        grid=(depth, nblk),
        in_specs=[
            pl.BlockSpec((bsz, d), lambda l, j: (0, 0)),
            pl.BlockSpec((None, d, d), lambda l, j: (l, 0, j)),
            pl.BlockSpec((None, 1, d), lambda l, j: (l, 0, j)),
        ],
        out_specs=pl.BlockSpec((None, bsz, d), lambda l, j: (l, 0, j)),
        out_shape=jax.ShapeDtypeStruct((depth, bsz, e), F32),
        compiler_params=_cparams("arbitrary", "arbitrary"),
        name="ada_mod",
    )(c, w_ada, b_ada.reshape(depth, 1, e))


def _norm_mod(x, g, shift, scale):
    ms = jnp.mean(x * x, axis=-1, keepdims=True)
    y = x * lax.rsqrt(ms + EPS) * g
    return y * (1.0 + scale) + shift


def _inproj_kernel(x_ref, mod_ref, g_ref, w_ref, o_ref, h_ref):
    @pl.when(pl.program_id(1) == 0)
    def _():
        h = _norm_mod(x_ref[...], g_ref[...], mod_ref[0:1, :], mod_ref[1:2, :])
        h_ref[...] = h.astype(BF16)

    res = jnp.dot(h_ref[...], w_ref[...], preferred_element_type=F32)
    for cb in range(o_ref.shape[0]):
        o_ref[cb] = res[:, cb * LANES:(cb + 1) * LANES].astype(o_ref.dtype)


def _inproj(x2d, mod_l, g, w, seq):
    t, d = x2d.shape
    n = w.shape[1]
    tm = min(TM_PROJ, seq)
    tn = TN_PROJ
    cpb = tn // LANES
    return pl.pallas_call(
        _inproj_kernel,
        grid=(t // tm, n // tn),
        in_specs=[
            pl.BlockSpec((tm, d), lambda i, j: (i, 0)),
            pl.BlockSpec((None, 6, d), lambda i, j: ((i * tm) // seq, 0, 0)),
            pl.BlockSpec((1, d), lambda i, j: (0, 0)),
            pl.BlockSpec((d, tn), lambda i, j: (0, j)),
        ],
        out_specs=pl.BlockSpec((cpb, tm, LANES), lambda i, j: (j, i, 0)),
        out_shape=jax.ShapeDtypeStruct((n // LANES, t, LANES), BF16),
        scratch_shapes=[pltpu.VMEM((tm, d), BF16)],
        compiler_params=_cparams("parallel", "arbitrary"),
        name="in_proj",
    )(x2d, mod_l, g, w)


def _t5_bucket_np(rel):
    nb = T5_BUCKETS // 2
    max_exact = nb // 2
    ret = np.where(rel > 0, nb, 0)
    n = np.abs(rel)
    nf = np.maximum(n, 1).astype(np.float32)
    scaled = (np.log(nf / np.float32(max_exact)) / np.float32(math.log(T5_MAX_DIST / max_exact))
              * np.float32(nb - max_exact))
    large = max_exact + scaled.astype(np.int32)
    large = np.minimum(large, nb - 1)
    return (ret + np.where(n < max_exact, n, large)).astype(np.int32)


def _a_bias_tiles(t5_table, tq):
    r = np.arange(tq)[:, None]
    c = np.arange(tq)[None, :]
    tiles = []
    for off in (2, 1, 0):
        rel = (c - off * tq) - r
        bias = jnp.transpose(t5_table[_t5_bucket_np(rel)], (2, 0, 1)).astype(F32)
        if off == 0:
            allowed = (c // CHUNK) <= (r // CHUNK)
            bias = jnp.where(jnp.asarray(allowed)[None], bias, NEG_INF)
        tiles.append(bias)
    bias = jnp.stack(tiles, axis=1)
    return jnp.concatenate([bias, bias], axis=2)


def _attn_a_kernel(lq1_ref, lk1_ref, lq2_ref, lk2_ref, subg_ref, bias_ref, q_ref, k_ref, v_ref,
                   o_ref, *, tq, lam_init):
    seq = q_ref.shape[0]
    nq = seq // tq
    lam = (jnp.exp(jnp.sum(lq1_ref[...] * lk1_ref[...], axis=-1, keepdims=True))
           - jnp.exp(jnp.sum(lq2_ref[...] * lk2_ref[...], axis=-1, keepdims=True)) + lam_init)
    lane = lax.broadcasted_iota(jnp.int32, (1, LANES), 1)
    mask1 = (lane < HEAD_DIM).astype(BF16)
    mask2 = (lane >= HEAD_DIM).astype(BF16)
    subg = subg_ref[...]

    def qblock(i, carry):
        r0 = pl.multiple_of(i * tq, tq)
        qb = q_ref[pl.ds(r0, tq), :]
        qq = jnp.concatenate([qb * mask1, qb * mask2], axis=0)

        def kstep(j, st):
            m, l, acc = st
            c0 = pl.multiple_of(j * tq, tq)
            kb = k_ref[pl.ds(c0, tq), :]
            vb = v_ref[pl.ds(c0, tq), :]
            s = lax.dot_general(qq, kb, (((1,), (1,)), ((), ())), preferred_element_type=F32)
            s = s + bias_ref[jnp.maximum(j - i + 2, 0)]
            m_new = jnp.maximum(m, jnp.max(s, axis=-1, keepdims=True))
            alpha = jnp.exp(m - m_new)
            p = jnp.exp(s - m_new)
            l = alpha * l + jnp.sum(p, axis=-1, keepdims=True)
            acc = alpha * acc + jnp.dot(p.astype(BF16), vb, preferred_element_type=F32)
            return m_new, l, acc

        st0 = (jnp.full((2 * tq, 1), NEG_INF, F32), jnp.zeros((2 * tq, 1), F32),
               jnp.zeros((2 * tq, LANES), F32))
        _, l, acc = lax.fori_loop(0, i + 1, kstep, st0)
        o = acc / l
        d = o[:tq] - lam * o[tq:]
        y = d * lax.rsqrt(jnp.mean(d * d, axis=-1, keepdims=True) + EPS) * subg * (1.0 - lam_init)
        o_ref[pl.ds(r0, tq), :] = y.astype(o_ref.dtype)
        return carry

    lax.fori_loop(0, nq, qblock, 0)


def _attn_a(proj, bias, lq1, lk1, lq2, lk2, subg, bsz, seq, lam_init):
    tq = min(TQ_A, seq)
    vec = lambda n: pl.BlockSpec((1, n), lambda b, h: (0, 0))
    col = lambda base: pl.BlockSpec((None, seq, LANES), lambda b, h: (base + h, b, 0))
    return pl.pallas_call(
        functools.partial(_attn_a_kernel, tq=tq, lam_init=lam_init),
        grid=(bsz, A_HEADS),
        in_specs=[vec(HEAD_DIM), vec(HEAD_DIM), vec(HEAD_DIM), vec(HEAD_DIM), vec(2 * HEAD_DIM),
                  pl.BlockSpec((None, 3, 2 * tq, tq), lambda b, h: (h, 0, 0, 0)),
                  col(AQ0), col(AK0), col(AV0)],
        out_specs=pl.BlockSpec((None, seq, LANES), lambda b, h: (h, b, 0)),
        out_shape=jax.ShapeDtypeStruct((A_HEADS, bsz * seq, LANES), BF16),
        compiler_params=_cparams("parallel", "parallel"),
        name="attn_a",
    )(lq1, lk1, lq2, lk2, subg, bias, proj, proj, proj)


def _b_bias_tiles(rel_bias, rq):
    w = rq + B_PAD
    r = np.arange(rq)[:, None]
    c = np.arange(w)[None, :]
    idx = np.clip(r + B_PAD - c, -B_MAX_REL, B_MAX_REL) + B_MAX_REL
    lo = CHUNK * (r // CHUNK)
    in_band = (c >= lo) & (c < lo + B_PAD + CHUNK)
    bias = rel_bias[:, idx].astype(F32)
    bias = jnp.where(jnp.asarray(in_band)[None], bias, NEG_INF)
    return bias.reshape(B_HEADS // 2, 2 * rq, w)


def _attn_b_kernel(bias_ref, q_ref, k_ref, v_ref, o_ref, kp_ref, vp_ref, *, rq):
    seq = q_ref.shape[0]
    w = rq + B_PAD
    nq = seq // rq
    zeros = jnp.zeros((B_PAD, LANES), BF16)
    kp_ref[0:B_PAD, :] = zeros
    vp_ref[0:B_PAD, :] = zeros
    kp_ref[B_PAD:B_PAD + seq, :] = k_ref[...]
    vp_ref[B_PAD:B_PAD + seq, :] = v_ref[...]
    lane = lax.broadcasted_iota(jnp.int32, (1, LANES), 1)
    mask1 = (lane < HEAD_DIM).astype(BF16)
    mask2 = (lane >= HEAD_DIM).astype(BF16)
    colpos = lax.broadcasted_iota(jnp.int32, (2 * rq, w), 1)
    lane_o = lax.broadcasted_iota(jnp.int32, (rq, LANES), 1)

    def qblock(i, carry):
        r0 = pl.multiple_of(i * rq, rq)
        qb = q_ref[pl.ds(r0, rq), :]
        qq = jnp.concatenate([qb * mask1, qb * mask2], axis=0)
        kb = kp_ref[pl.ds(r0, w), :]
        vb = vp_ref[pl.ds(r0, w), :]
        s = lax.dot_general(qq, kb, (((1,), (1,)), ((), ())), preferred_element_type=F32)
        s = s + bias_ref[...]
        s = jnp.where(colpos + r0 >= B_PAD, s, NEG_INF)
        m = jnp.max(s, axis=-1, keepdims=True)
        p = jnp.exp(s - m)
        den = jnp.sum(p, axis=-1, keepdims=True)
        o = jnp.dot(p.astype(BF16), vb, preferred_element_type=F32) / den
        o_ref[pl.ds(r0, rq), :] = jnp.where(lane_o < HEAD_DIM, o[:rq], o[rq:]).astype(o_ref.dtype)
        return carry

    lax.fori_loop(0, nq, qblock, 0)


def _attn_b(proj, bias, bsz, seq):
    rq = RQ_B
    w = rq + B_PAD
    col = lambda base: pl.BlockSpec((None, seq, LANES), lambda b, h: (base + h, b, 0))
    return pl.pallas_call(
        functools.partial(_attn_b_kernel, rq=rq),
        grid=(bsz, B_HEADS // 2),
        in_specs=[pl.BlockSpec((None, 2 * rq, w), lambda b, h: (h, 0, 0)),
                  col(BQ0), col(BK0), col(BV0)],
        out_specs=pl.BlockSpec((None, seq, LANES), lambda b, h: (h, b, 0)),
        out_shape=jax.ShapeDtypeStruct((B_HEADS // 2, bsz * seq, LANES), BF16),
        scratch_shapes=[pltpu.VMEM((seq + B_PAD, LANES), BF16), pltpu.VMEM((seq + B_PAD, LANES), BF16)],
        compiler_params=_cparams("parallel", "parallel"),
        name="attn_b",
    )(bias, proj, proj, proj)


def _attn_c_kernel(q_ref, k_ref, v_ref, o_ref, *, tq):
    seq = q_ref.shape[0]
    nq = seq // tq
    row = lax.broadcasted_iota(jnp.int32, (tq, tq), 0)
    colm = lax.broadcasted_iota(jnp.int32, (tq, tq), 1)
    tri_strict = (row > colm).astype(BF16)
    before = colm < row

    def block(qb, j, carry, acc, diag):
        c0 = pl.multiple_of(j * tq, tq)
        kb = k_ref[pl.ds(c0, tq), :]
        vb = v_ref[pl.ds(c0, tq), :]
        z = lax.dot_general(qb, kb, (((1,), (1,)), ((), ())), preferred_element_type=F32)
        lm = -(jnp.maximum(z, 0.0) + jnp.log(1.0 + jnp.exp(-jnp.abs(z))))
        if diag:
            lm = jnp.where(before, lm, 0.0)
        lm_b = lm.astype(BF16)
        excl = jnp.dot(lm_b, tri_strict, preferred_element_type=F32)
        logw = (z + lm) + excl + carry
        wgt = jnp.exp(logw)
        if diag:
            wgt = jnp.where(before, wgt, 0.0)
        acc = acc + jnp.dot(wgt.astype(BF16), vb, preferred_element_type=F32)
        carry = carry + excl[:, 0:1] + lm_b[:, 0:1].astype(F32)
        return carry, acc

    def qblock(i, c):
        r0 = pl.multiple_of(i * tq, tq)
        qb = q_ref[pl.ds(r0, tq), :]
        carry, acc = block(qb, i, jnp.zeros((tq, 1), F32), jnp.zeros((tq, LANES), F32), True)

        def cond(st):
            j, cmax, _, _ = st
            return jnp.logical_and(j >= 0, cmax > C_SKIP_LOG)

        def body(st):
            j, _, carry, acc = st
            carry, acc = block(qb, j, carry, acc, False)
            return j - 1, jnp.max(carry), carry, acc

        _, _, _, acc = lax.while_loop(cond, body, (i - 1, jnp.max(carry), carry, acc))
        o_ref[pl.ds(r0, tq), :] = acc.astype(o_ref.dtype)
        return c

    lax.fori_loop(0, nq, qblock, 0)


def _attn_c(proj, bsz, seq):
    tq = min(TQ_C, seq)
    col = lambda base: pl.BlockSpec((None, seq, LANES), lambda b, h: (base + h, b, 0))
    return pl.pallas_call(
        functools.partial(_attn_c_kernel, tq=tq),
        grid=(bsz, C_HEADS),
        in_specs=[col(CQ0), col(CK0), col(CV0)],
        out_specs=pl.BlockSpec((None, seq, LANES), lambda b, h: (h, b, 0)),
        out_shape=jax.ShapeDtypeStruct((C_HEADS, bsz * seq, LANES), BF16),
        compiler_params=_cparams("parallel", "parallel"),
        name="attn_c",
    )(proj, proj, proj)


def _cat_lanes(ref, start, count):
    return jnp.concatenate([ref[start + c] for c in range(count)], axis=1)


def _merge_kernel(x_ref, mod_ref, gate_ref, bg_ref, ya_ref, yb_ref, yc_ref, wb_ref, wo_ref, o_ref):
    blks = D_MODEL // LANES
    merged = None
    for r, y_ref in enumerate((ya_ref, yb_ref, yc_ref)):
        y = _cat_lanes(y_ref, 0, BRANCH_WIDTH // LANES)
        br = jnp.dot(y, wb_ref[r], preferred_element_type=F32)
        pre = _cat_lanes(gate_ref, r * blks, blks).astype(F32) + bg_ref[:, r * D_MODEL:(r + 1) * D_MODEL]
        term = jax.nn.sigmoid(pre) * br
        merged = term if merged is None else merged + term
    out = jnp.dot(merged.astype(BF16), wo_ref[...], preferred_element_type=F32)
    o_ref[...] = x_ref[...] + mod_ref[2:3, :] * out


def _merge(x2d, mod_l, proj, b_gate, ya, yb, yc, w_branch, w_out, seq):
    t, d = x2d.shape
    tm = min(TM_MERGE, seq)
    ybs = lambda: pl.BlockSpec((BRANCH_WIDTH // LANES, tm, LANES), lambda i: (0, i, 0))
    return pl.pallas_call(
        _merge_kernel,
        grid=(t // tm,),
        in_specs=[
            pl.BlockSpec((tm, d), lambda i: (i, 0)),
            pl.BlockSpec((None, 6, d), lambda i: ((i * tm) // seq, 0, 0)),
            pl.BlockSpec((GATE_BLKS, tm, LANES), lambda i: (0, i, 0)),
            pl.BlockSpec((1, GATE_COLS), lambda i: (0, 0)),
            ybs(), ybs(), ybs(),
            pl.BlockSpec((N_BRANCH, BRANCH_WIDTH, d), lambda i: (0, 0, 0)),
            pl.BlockSpec((d, d), lambda i: (0, 0)),
        ],
        out_specs=pl.BlockSpec((tm, d), lambda i: (i, 0)),
        out_shape=jax.ShapeDtypeStruct((t, d), F32),
        compiler_params=_cparams("parallel"),
        name="merge",
    )(x2d, mod_l, proj, b_gate, ya, yb, yc, w_branch, w_out)


def _ffn_kernel(x_ref, mod_ref, g_ref, fg_ref, w1_ref, w3_ref, w2_ref, o_ref, h_ref, acc_ref, *, final):
    f = pl.program_id(1)

    @pl.when(f == 0)
    def _():
        h = _norm_mod(x_ref[...], g_ref[...], mod_ref[3:4, :], mod_ref[4:5, :])
        h_ref[...] = h.astype(BF16)

    h = h_ref[...]
    u_gate = jnp.dot(h, w1_ref[...], preferred_element_type=F32)
    u_up = jnp.dot(h, w3_ref[...], preferred_element_type=F32)
    act = (u_gate * jax.nn.sigmoid(u_gate) * u_up).astype(BF16)
    part = jnp.dot(act, w2_ref[...], preferred_element_type=F32)

    @pl.when(f == 0)
    def _():
        acc_ref[...] = part

    @pl.when(f > 0)
    def _():
        acc_ref[...] += part

    @pl.when(f == pl.num_programs(1) - 1)
    def _():
        y = x_ref[...] + mod_ref[5:6, :] * acc_ref[...]
        if final:
            y = y * lax.rsqrt(jnp.mean(y * y, axis=-1, keepdims=True) + EPS) * fg_ref[...]
        o_ref[...] = y


def _ffn(x2d, mod_l, g, final_g, w13, w2, seq, final):
    t, d = x2d.shape
    dff = w2.shape[0]
    tm = min(TM_FFN, seq)
    tf = TF_FFN
    nf = dff // tf
    return pl.pallas_call(
        functools.partial(_ffn_kernel, final=final),
        grid=(t // tm, nf),
        in_specs=[
            pl.BlockSpec((tm, d), lambda i, f: (i, 0)),
            pl.BlockSpec((None, 6, d), lambda i, f: ((i * tm) // seq, 0, 0)),
            pl.BlockSpec((1, d), lambda i, f: (0, 0)),
            pl.BlockSpec((1, d), lambda i, f: (0, 0)),
            pl.BlockSpec((d, tf), lambda i, f: (0, f)),
            pl.BlockSpec((d, tf), lambda i, f: (0, f + nf)),
            pl.BlockSpec((tf, d), lambda i, f: (f, 0)),
        ],
        out_specs=pl.BlockSpec((tm, d), lambda i, f: (i, 0)),
        out_shape=jax.ShapeDtypeStruct((t, d), F32),
        scratch_shapes=[pltpu.VMEM((tm, d), BF16), pltpu.VMEM((tm, d), F32)],
        compiler_params=_cparams("parallel", "arbitrary"),
        name="ffn",
    )(x2d, mod_l, g, final_g, w13, w13, w2)


def _prep_w_in(w_in):
    scale = np.ones((IN_COLS,), np.float32)
    scale[0 * BRANCH_WIDTH:1 * BRANCH_WIDTH] = HEAD_DIM ** -0.5
    scale[3 * BRANCH_WIDTH:4 * BRANCH_WIDTH] = HEAD_DIM ** -0.5
    scale[6 * BRANCH_WIDTH:7 * BRANCH_WIDTH] = C_HEAD_DIM ** -0.5
    w = w_in * jnp.asarray(scale)
    return jnp.concatenate([w[..., QKV_COLS:], w[..., :QKV_COLS]], axis=-1).astype(BF16)


def kernel(x, c, w_ada, b_ada, norm1_g, w_in, b_gate, lam_q1, lam_k1, lam_q2, lam_k2, subln_g,
           t5_table, rel_bias_b, w_branch, w_out, norm2_g, w13, w2, final_g):
    bsz, seq, d = x.shape
    depth = w_in.shape[0]
    t = bsz * seq

    mod = _ada_mod(c, w_ada, b_ada).reshape(depth, bsz, 6, d)
    w_in_b = _prep_w_in(w_in)
    w_branch_b = w_branch.astype(BF16)
    w_out_b = w_out.astype(BF16)
    w13_b = w13.astype(BF16)
    w2_b = w2.astype(BF16)
    a_bias = _a_bias_tiles(t5_table, min(TQ_A, seq))
    fg = final_g.reshape(1, d)

    x2d = x.reshape(t, d)
    for l in range(depth):
        lam_init = 0.8 - 0.6 * math.exp(-0.3 * l)
        proj = _inproj(x2d, mod[l], norm1_g[l].reshape(1, d), w_in_b[l], seq)
        ya = _attn_a(proj, a_bias, lam_q1[l].reshape(1, -1), lam_k1[l].reshape(1, -1),
                     lam_q2[l].reshape(1, -1), lam_k2[l].reshape(1, -1), subln_g[l].reshape(1, -1),
                     bsz, seq, lam_init)
        yb = _attn_b(proj, _b_bias_tiles(rel_bias_b[l], RQ_B), bsz, seq)
        yc = _attn_c(proj, bsz, seq)
        x2d = _merge(x2d, mod[l], proj, b_gate[l].reshape(1, -1), ya, yb, yc,
                     w_branch_b[l], w_out_b[l], seq)
        x2d = _ffn(x2d, mod[l], norm2_g[l].reshape(1, d), fg, w13_b[l], w2_b[l], seq,
                   final=(l == depth - 1))
    return x2d.reshape(bsz, seq, d)
```

```python
import functools
import math

import numpy as np
import jax
import jax.numpy as jnp
from jax import lax
from jax.experimental import pallas as pl
from jax.experimental.pallas import tpu as pltpu

F32 = jnp.float32
BF16 = jnp.bfloat16

D_MODEL = 1024
DEPTH = 4
CHUNK = 64
HEAD_DIM = 64
BRANCH_WIDTH = 512
N_BRANCH = 3
A_HEADS = 4
B_HEADS = 8
C_HEADS = 4
C_HEAD_DIM = 128
B_LEFT_CHUNKS = 8
B_PAD = B_LEFT_CHUNKS * CHUNK
B_MAX_REL = 128
T5_BUCKETS = 32
T5_MAX_DIST = 128
D_FF = 2816
QKV_COLS = 3 * N_BRANCH * BRANCH_WIDTH
GATE_COLS = N_BRANCH * D_MODEL
IN_COLS = QKV_COLS + GATE_COLS
NEG_INF = -1e30
EPS = 1e-6
LOG2E = math.log2(math.e)

LANES = 128
N_COLBLK = IN_COLS // LANES
GATE_BLKS = GATE_COLS // LANES
AQ0, AK0, AV0 = GATE_BLKS, GATE_BLKS + 4, GATE_BLKS + 8
BQ0, BK0, BV0 = GATE_BLKS + 12, GATE_BLKS + 16, GATE_BLKS + 20
CQ0, CK0, CV0 = GATE_BLKS + 24, GATE_BLKS + 28, GATE_BLKS + 32

TM_PROJ = 1024
TN_PROJ = 1536
TM_MERGE = 512
TM_FFN = 512
TF_FFN = 1408
TQ_A = 512
UNROLL_A = 4
RQ_B = 128
TQ_C = 256
C_SKIP_LOG = -100.0
VMEM_LIMIT = 56 * 1024 * 1024


def _cparams(*sem, flags=None):
    return pltpu.CompilerParams(dimension_semantics=sem, vmem_limit_bytes=VMEM_LIMIT, flags=flags)


def _ada_kernel(c_ref, w_ref, b_ref, o_ref):
    c = c_ref[...]
    cs = c * jax.nn.sigmoid(c)
    o_ref[...] = jnp.dot(cs, w_ref[...], preferred_element_type=F32,
                         precision=lax.Precision.HIGHEST) + b_ref[...]


def _ada_mod(c, w_ada, b_ada):
    depth, d, e = w_ada.shape
    bsz = c.shape[0]
    nblk = e // d
    return pl.pallas_call(
        _ada_kernel,
        grid=(depth, nblk),
        in_specs=[
            pl.BlockSpec((bsz, d), lambda l, j: (0, 0)),
            pl.BlockSpec((None, d, d), lambda l, j: (l, 0, j)),
            pl.BlockSpec((None, 1, d), lambda l, j: (l, 0, j)),
        ],
        out_specs=pl.BlockSpec((None, bsz, d), lambda l, j: (l, 0, j)),
        out_shape=jax.ShapeDtypeStruct((depth, bsz, e), F32),
        compiler_params=_cparams("arbitrary", "arbitrary"),
        name="ada_mod",
    )(c, w_ada, b_ada.reshape(depth, 1, e))


def _norm_mod(x, g, shift, scale):
    ms = jnp.mean(x * x, axis=-1, keepdims=True)
    y = x * lax.rsqrt(ms + EPS) * g
    return y * (1.0 + scale) + shift


def _inproj_kernel(x_ref, mod_ref, g_ref, w_ref, o_ref, h_ref):
    @pl.when(pl.program_id(1) == 0)
    def _():
        h = _norm_mod(x_ref[...], g_ref[...], mod_ref[0:1, :], mod_ref[1:2, :])
        h_ref[...] = h.astype(BF16)

    res = jnp.dot(h_ref[...], w_ref[...], preferred_element_type=F32)
    for cb in range(o_ref.shape[0]):
        o_ref[cb] = res[:, cb * LANES:(cb + 1) * LANES].astype(o_ref.dtype)


def _inproj(x2d, mod_l, g, w, seq):
    t, d = x2d.shape
    n = w.shape[1]
    tm = min(TM_PROJ, seq)
    tn = TN_PROJ
    cpb = tn // LANES
    return pl.pallas_call(
        _inproj_kernel,
        grid=(t // tm, n // tn),
        in_specs=[
            pl.BlockSpec((tm, d), lambda i, j: (i, 0)),
            pl.BlockSpec((None, 6, d), lambda i, j: ((i * tm) // seq, 0, 0)),
            pl.BlockSpec((1, d), lambda i, j: (0, 0)),
            pl.BlockSpec((d, tn), lambda i, j: (0, j)),
        ],
        out_specs=pl.BlockSpec((cpb, tm, LANES), lambda i, j: (j, i, 0)),
        out_shape=jax.ShapeDtypeStruct((n // LANES, t, LANES), BF16),
        scratch_shapes=[pltpu.VMEM((tm, d), BF16)],
        compiler_params=_cparams("parallel", "arbitrary"),
        name="in_proj",
    )(x2d, mod_l, g, w)


def _t5_bucket_np(rel):
    nb = T5_BUCKETS // 2
    max_exact = nb // 2
    ret = np.where(rel > 0, nb, 0)
    n = np.abs(rel)
    nf = np.maximum(n, 1).astype(np.float32)
    scaled = (np.log(nf / np.float32(max_exact)) / np.float32(math.log(T5_MAX_DIST / max_exact))
              * np.float32(nb - max_exact))
    large = max_exact + scaled.astype(np.int32)
    large = np.minimum(large, nb - 1)
    return (ret + np.where(n < max_exact, n, large)).astype(np.int32)


def _a_bias_tiles(t5_table, tq, tk):
    r = np.arange(tq)[None, :]
    c = np.arange(tk)[:, None]
    far_bucket = _t5_bucket_np((c - 2 * tk) - r)
    assert (far_bucket == far_bucket[0, 0]).all() and far_bucket[0, 0] == _t5_bucket_np(np.array(-10 * tq))
    far = t5_table[int(far_bucket[0, 0])].astype(F32)
    tiles = [jnp.zeros((A_HEADS, tk, tq), F32)]
    for koff in (-tk, 0, tk):
        rel = (c + koff) - r
        bias = jnp.transpose(t5_table[_t5_bucket_np(rel)], (2, 0, 1)).astype(F32)
        bias = (bias - far[:, None, None]) * LOG2E
        allowed = ((c + koff) // CHUNK) <= (r // CHUNK)
        tiles.append(jnp.where(jnp.asarray(allowed)[None], bias, NEG_INF))
    tiles.append(jnp.full((A_HEADS, tk, tq), NEG_INF, F32))
    return jnp.stack(tiles, axis=1)


def _attn_a_kernel(lq1_ref, lk1_ref, lq2_ref, lk2_ref, subg_ref, bias_ref, q_ref, k_ref, v_ref,
                   o_ref, *, tq, tk, unroll, lam_init):
    seq = q_ref.shape[0]
    nq = seq // tq
    nkb = seq // tk
    per_q = tq // tk
    lam = (jnp.exp(jnp.sum(lq1_ref[...] * lk1_ref[...], axis=-1, keepdims=True))
           - jnp.exp(jnp.sum(lq2_ref[...] * lk2_ref[...], axis=-1, keepdims=True)) + lam_init)
    lane = lax.broadcasted_iota(jnp.int32, (1, LANES), 1)
    mask1 = (lane < HEAD_DIM).astype(BF16)
    mask2 = (lane >= HEAD_DIM).astype(BF16)
    subg = subg_ref[...] * (1.0 - lam_init)

    def qblock(i, carry):
        r0 = pl.multiple_of(i * tq, tq)
        qb = q_ref[pl.ds(r0, tq), :]
        qq = jnp.concatenate([qb * mask1, qb * mask2], axis=0)
        n_blocks = per_q * (i + 1)

        def block(js, st, with_bias):
            m, l, acc = st
            c0 = pl.multiple_of(jnp.minimum(js, nkb - 1) * tk, tk)
            kb = k_ref[pl.ds(c0, tk), :]
            vb = v_ref[pl.ds(c0, tk), :]
            s = lax.dot_general(kb, qq, (((1,), (1,)), ((), ())), preferred_element_type=F32)
            if with_bias:
                bias = bias_ref[jnp.clip(js - per_q * i + 2, 0, 4)]
                s = s + jnp.concatenate([bias, bias], axis=1)
            m_new = jnp.maximum(m, jnp.max(s, axis=0, keepdims=True))
            alpha = jnp.exp2(m - m_new)
            p = jnp.exp2(s - m_new)
            l = alpha * l + jnp.sum(p, axis=0, keepdims=True)
            pv = lax.dot_general(vb, p.astype(BF16), (((0,), (0,)), ((), ())), preferred_element_type=F32)
            return m_new, l, alpha * acc + pv

        def step(it, st, with_bias):
            for u in range(unroll):
                st = block(it * unroll + u, st, with_bias)
            return st

        st = (jnp.full((1, 2 * tq), NEG_INF, F32), jnp.zeros((1, 2 * tq), F32),
              jnp.zeros((LANES, 2 * tq), F32))
        n_plain = jnp.maximum(per_q * i - 1, 0) // unroll
        n_steps = (n_blocks + unroll - 1) // unroll
        st = lax.fori_loop(0, n_plain, lambda it, s_: step(it, s_, False), st)
        st = lax.fori_loop(n_plain, n_steps, lambda it, s_: step(it, s_, True), st)
        _, l, acc = st
        o = acc / l
        d = o[:, :tq] - lam * o[:, tq:]
        y = d * lax.rsqrt(jnp.mean(d * d, axis=0, keepdims=True) + EPS) * subg
        o_ref[pl.ds(r0, tq), :] = jnp.transpose(y).astype(o_ref.dtype)
        return carry

    lax.fori_loop(0, nq, qblock, 0)


def _attn_a(proj, bias, lq1, lk1, lq2, lk2, subg, bsz, seq, lam_init):
    tq = min(TQ_A, seq // 2)
    tk = tq // 2
    vec = lambda n: pl.BlockSpec((1, n), lambda b, h: (0, 0))
    col = lambda base: pl.BlockSpec((None, seq, LANES), lambda b, h: (base + h, b, 0))
    return pl.pallas_call(
        functools.partial(_attn_a_kernel, tq=tq, tk=tk, unroll=UNROLL_A, lam_init=lam_init),
        grid=(bsz, A_HEADS),
        in_specs=[vec(HEAD_DIM), vec(HEAD_DIM), vec(HEAD_DIM), vec(HEAD_DIM),
                  pl.BlockSpec((2 * HEAD_DIM, 1), lambda b, h: (0, 0)),
                  pl.BlockSpec((None, 5, tk, tq), lambda b, h: (h, 0, 0, 0)),
                  col(AQ0), col(AK0), col(AV0)],
        out_specs=pl.BlockSpec((None, seq, LANES), lambda b, h: (h, b, 0)),
        out_shape=jax.ShapeDtypeStruct((A_HEADS, bsz * seq, LANES), BF16),
        compiler_params=_cparams("parallel", "parallel"),
        name="attn_a",
    )(lq1, lk1, lq2, lk2, subg, bias, proj, proj, proj)


def _b_bias_tiles(rel_bias, rq):
    w = rq + B_PAD
    r = np.arange(rq)[:, None]
    c = np.arange(w)[None, :]
    idx = np.clip(r + B_PAD - c, -B_MAX_REL, B_MAX_REL) + B_MAX_REL
    lo = CHUNK * (r // CHUNK)
    in_band = (c >= lo) & (c < lo + B_PAD + CHUNK)
    bias = rel_bias[:, idx].astype(F32)
    bias = jnp.where(jnp.asarray(in_band)[None], bias, NEG_INF)
    return bias.reshape(B_HEADS // 2, 2 * rq, w)


def _attn_b_kernel(bias_ref, q_ref, k_ref, v_ref, o_ref, kp_ref, vp_ref, *, rq):
    seq = q_ref.shape[0]
    w = rq + B_PAD
    nq = seq // rq
    zeros = jnp.zeros((B_PAD, LANES), BF16)
    kp_ref[0:B_PAD, :] = zeros
    vp_ref[0:B_PAD, :] = zeros
    kp_ref[B_PAD:B_PAD + seq, :] = k_ref[...]
    vp_ref[B_PAD:B_PAD + seq, :] = v_ref[...]
    lane = lax.broadcasted_iota(jnp.int32, (1, LANES), 1)
    mask1 = (lane < HEAD_DIM).astype(BF16)
    mask2 = (lane >= HEAD_DIM).astype(BF16)
    colpos = lax.broadcasted_iota(jnp.int32, (2 * rq, w), 1)
    lane_o = lax.broadcasted_iota(jnp.int32, (rq, LANES), 1)

    def qblock(i, carry):
        r0 = pl.multiple_of(i * rq, rq)
        qb = q_ref[pl.ds(r0, rq), :]
        qq = jnp.concatenate([qb * mask1, qb * mask2], axis=0)
        kb = kp_ref[pl.ds(r0, w), :]
        vb = vp_ref[pl.ds(r0, w), :]
        s = lax.dot_general(qq, kb, (((1,), (1,)), ((), ())), preferred_element_type=F32)
        s = s + bias_ref[...]
        s = jnp.where(colpos + r0 >= B_PAD, s, NEG_INF)
        m = jnp.max(s, axis=-1, keepdims=True)
        p = jnp.exp(s - m)
        den = jnp.sum(p, axis=-1, keepdims=True)
        o = jnp.dot(p.astype(BF16), vb, preferred_element_type=F32) / den
        o_ref[pl.ds(r0, rq), :] = jnp.where(lane_o < HEAD_DIM, o[:rq], o[rq:]).astype(o_ref.dtype)
        return carry

    lax.fori_loop(0, nq, qblock, 0)


def _attn_b(proj, bias, bsz, seq):
    rq = RQ_B
    w = rq + B_PAD
    col = lambda base: pl.BlockSpec((None, seq, LANES), lambda b, h: (base + h, b, 0))
    return pl.pallas_call(
        functools.partial(_attn_b_kernel, rq=rq),
        grid=(bsz, B_HEADS // 2),
        in_specs=[pl.BlockSpec((None, 2 * rq, w), lambda b, h: (h, 0, 0)),
                  col(BQ0), col(BK0), col(BV0)],
        out_specs=pl.BlockSpec((None, seq, LANES), lambda b, h: (h, b, 0)),
        out_shape=jax.ShapeDtypeStruct((B_HEADS // 2, bsz * seq, LANES), BF16),
        scratch_shapes=[pltpu.VMEM((seq + B_PAD, LANES), BF16), pltpu.VMEM((seq + B_PAD, LANES), BF16)],
        compiler_params=_cparams("parallel", "parallel"),
        name="attn_b",
    )(bias, proj, proj, proj)


def _attn_c_kernel(q_ref, k_ref, v_ref, o_ref, *, tq):
    seq = q_ref.shape[0]
    nq = seq // tq
    row = lax.broadcasted_iota(jnp.int32, (tq, tq), 0)
    colm = lax.broadcasted_iota(jnp.int32, (tq, tq), 1)
    tri_strict = (row > colm).astype(BF16)
    before = colm < row

    def block(qb, j, carry, acc, diag):
        c0 = pl.multiple_of(j * tq, tq)
        kb = k_ref[pl.ds(c0, tq), :]
        vb = v_ref[pl.ds(c0, tq), :]
        z = lax.dot_general(qb, kb, (((1,), (1,)), ((), ())), preferred_element_type=F32)
        lm = -(jnp.maximum(z, 0.0) + jnp.log(1.0 + jnp.exp(-jnp.abs(z))))
        if diag:
            lm = jnp.where(before, lm, 0.0)
        lm_b = lm.astype(BF16)
        excl = jnp.dot(lm_b, tri_strict, preferred_element_type=F32)
        logw = (z + lm) + excl + carry
        wgt = jnp.exp(logw)
        if diag:
            wgt = jnp.where(before, wgt, 0.0)
        acc = acc + jnp.dot(wgt.astype(BF16), vb, preferred_element_type=F32)
        carry = carry + excl[:, 0:1] + lm_b[:, 0:1].astype(F32)
        return carry, acc

    def qblock(i, c):
        r0 = pl.multiple_of(i * tq, tq)
        qb = q_ref[pl.ds(r0, tq), :]
        carry, acc = block(qb, i, jnp.zeros((tq, 1), F32), jnp.zeros((tq, LANES), F32), True)

        def cond(st):
            j, cmax, _, _ = st
            return jnp.logical_and(j >= 0, cmax > C_SKIP_LOG)

        def body(st):
            j, _, carry, acc = st
            carry, acc = block(qb, j, carry, acc, False)
            return j - 1, jnp.max(carry), carry, acc

        _, _, _, acc = lax.while_loop(cond, body, (i - 1, jnp.max(carry), carry, acc))
        o_ref[pl.ds(r0, tq), :] = acc.astype(o_ref.dtype)
        return c

    lax.fori_loop(0, nq, qblock, 0)


def _attn_c(proj, bsz, seq):
    tq = min(TQ_C, seq)
    col = lambda base: pl.BlockSpec((None, seq, LANES), lambda b, h: (base + h, b, 0))
    return pl.pallas_call(
        functools.partial(_attn_c_kernel, tq=tq),
        grid=(bsz, C_HEADS),
        in_specs=[col(CQ0), col(CK0), col(CV0)],
        out_specs=pl.BlockSpec((None, seq, LANES), lambda b, h: (h, b, 0)),
        out_shape=jax.ShapeDtypeStruct((C_HEADS, bsz * seq, LANES), BF16),
        compiler_params=_cparams("parallel", "parallel"),
        name="attn_c",
    )(proj, proj, proj)


def _cat_lanes(ref, start, count):
    return jnp.concatenate([ref[start + c] for c in range(count)], axis=1)


def _merge_kernel(x_ref, mod_ref, gate_ref, bg_ref, ya_ref, yb_ref, yc_ref, wb_ref, wo_ref, o_ref):
    blks = D_MODEL // LANES
    merged = None
    for r, y_ref in enumerate((ya_ref, yb_ref, yc_ref)):
        y = _cat_lanes(y_ref, 0, BRANCH_WIDTH // LANES)
        br = jnp.dot(y, wb_ref[r], preferred_element_type=F32)
        pre = _cat_lanes(gate_ref, r * blks, blks).astype(F32) + bg_ref[:, r * D_MODEL:(r + 1) * D_MODEL]
        term = jax.nn.sigmoid(pre) * br
        merged = term if merged is None else merged + term
    out = jnp.dot(merged.astype(BF16), wo_ref[...], preferred_element_type=F32)
    o_ref[...] = x_ref[...] + mod_ref[2:3, :] * out


def _merge(x2d, mod_l, proj, b_gate, ya, yb, yc, w_branch, w_out, seq):
    t, d = x2d.shape
    tm = min(TM_MERGE, seq)
    ybs = lambda: pl.BlockSpec((BRANCH_WIDTH // LANES, tm, LANES), lambda i: (0, i, 0))
    return pl.pallas_call(
        _merge_kernel,
        grid=(t // tm,),
        in_specs=[
            pl.BlockSpec((tm, d), lambda i: (i, 0)),
            pl.BlockSpec((None, 6, d), lambda i: ((i * tm) // seq, 0, 0)),
            pl.BlockSpec((GATE_BLKS, tm, LANES), lambda i: (0, i, 0)),
            pl.BlockSpec((1, GATE_COLS), lambda i: (0, 0)),
            ybs(), ybs(), ybs(),
            pl.BlockSpec((N_BRANCH, BRANCH_WIDTH, d), lambda i: (0, 0, 0)),
            pl.BlockSpec((d, d), lambda i: (0, 0)),
        ],
        out_specs=pl.BlockSpec((tm, d), lambda i: (i, 0)),
        out_shape=jax.ShapeDtypeStruct((t, d), F32),
        compiler_params=_cparams("parallel"),
        name="merge",
    )(x2d, mod_l, proj, b_gate, ya, yb, yc, w_branch, w_out)


def _ffn_kernel(x_ref, mod_ref, g_ref, fg_ref, w1_ref, w3_ref, w2_ref, o_ref, h_ref, acc_ref, *, final):
    f = pl.program_id(1)

    @pl.when(f == 0)
    def _():
        h = _norm_mod(x_ref[...], g_ref[...], mod_ref[3:4, :], mod_ref[4:5, :])
        h_ref[...] = h.astype(BF16)

    h = h_ref[...]
    u_gate = jnp.dot(h, w1_ref[...], preferred_element_type=F32)
    u_up = jnp.dot(h, w3_ref[...], preferred_element_type=F32)
    act = (u_gate * jax.nn.sigmoid(u_gate) * u_up).astype(BF16)
    part = jnp.dot(act, w2_ref[...], preferred_element_type=F32)

    @pl.when(f == 0)
    def _():
        acc_ref[...] = part

    @pl.when(f > 0)
    def _():
        acc_ref[...] += part

    @pl.when(f == pl.num_programs(1) - 1)
    def _():
        y = x_ref[...] + mod_ref[5:6, :] * acc_ref[...]
        if final:
            y = y * lax.rsqrt(jnp.mean(y * y, axis=-1, keepdims=True) + EPS) * fg_ref[...]
        o_ref[...] = y


def _ffn(x2d, mod_l, g, final_g, w13, w2, seq, final):
    t, d = x2d.shape
    dff = w2.shape[0]
    tm = min(TM_FFN, seq)
    tf = TF_FFN
    nf = dff // tf
    return pl.pallas_call(
        functools.partial(_ffn_kernel, final=final),
        grid=(t // tm, nf),
        in_specs=[
            pl.BlockSpec((tm, d), lambda i, f: (i, 0)),
            pl.BlockSpec((None, 6, d), lambda i, f: ((i * tm) // seq, 0, 0)),
            pl.BlockSpec((1, d), lambda i, f: (0, 0)),
            pl.BlockSpec((1, d), lambda i, f: (0, 0)),
            pl.BlockSpec((d, tf), lambda i, f: (0, f)),
            pl.BlockSpec((d, tf), lambda i, f: (0, f + nf)),
            pl.BlockSpec((tf, d), lambda i, f: (f, 0)),
        ],
        out_specs=pl.BlockSpec((tm, d), lambda i, f: (i, 0)),
        out_shape=jax.ShapeDtypeStruct((t, d), F32),
        scratch_shapes=[pltpu.VMEM((tm, d), BF16), pltpu.VMEM((tm, d), F32)],
        compiler_params=_cparams("parallel", "arbitrary"),
        name="ffn",
    )(x2d, mod_l, g, final_g, w13, w13, w2)


def _prep_w_in(w_in):
    scale = np.ones((IN_COLS,), np.float32)
    scale[0 * BRANCH_WIDTH:1 * BRANCH_WIDTH] = HEAD_DIM ** -0.5 * LOG2E
    scale[3 * BRANCH_WIDTH:4 * BRANCH_WIDTH] = HEAD_DIM ** -0.5
    scale[6 * BRANCH_WIDTH:7 * BRANCH_WIDTH] = C_HEAD_DIM ** -0.5
    w = w_in * jnp.asarray(scale)
    return jnp.concatenate([w[..., QKV_COLS:], w[..., :QKV_COLS]], axis=-1).astype(BF16)


def kernel(x, c, w_ada, b_ada, norm1_g, w_in, b_gate, lam_q1, lam_k1, lam_q2, lam_k2, subln_g,
           t5_table, rel_bias_b, w_branch, w_out, norm2_g, w13, w2, final_g):
    bsz, seq, d = x.shape
    depth = w_in.shape[0]
    t = bsz * seq

    mod = _ada_mod(c, w_ada, b_ada).reshape(depth, bsz, 6, d)
    w_in_b = _prep_w_in(w_in)
    w_branch_b = w_branch.astype(BF16)
    w_out_b = w_out.astype(BF16)
    w13_b = w13.astype(BF16)
    w2_b = w2.astype(BF16)
    tq_a = min(TQ_A, seq // 2)
    a_bias = _a_bias_tiles(t5_table, tq_a, tq_a // 2)
    fg = final_g.reshape(1, d)

    x2d = x.reshape(t, d)
    for l in range(depth):
        lam_init = 0.8 - 0.6 * math.exp(-0.3 * l)
        proj = _inproj(x2d, mod[l], norm1_g[l].reshape(1, d), w_in_b[l], seq)
        ya = _attn_a(proj, a_bias, lam_q1[l].reshape(1, -1), lam_k1[l].reshape(1, -1),
                     lam_q2[l].reshape(1, -1), lam_k2[l].reshape(1, -1), subln_g[l].reshape(-1, 1),
                     bsz, seq, lam_init)
        yb = _attn_b(proj, _b_bias_tiles(rel_bias_b[l], RQ_B), bsz, seq)
        yc = _attn_c(proj, bsz, seq)
        x2d = _merge(x2d, mod[l], proj, b_gate[l].reshape(1, -1), ya, yb, yc,
                     w_branch_b[l], w_out_b[l], seq)
        x2d = _ffn(x2d, mod[l], norm2_g[l].reshape(1, d), fg, w13_b[l], w2_b[l], seq,
                   final=(l == depth - 1))
    return x2d.reshape(bsz, seq, d)
```

```python
import functools
import math

import numpy as np
import jax
import jax.numpy as jnp
from jax import lax
from jax.experimental import pallas as pl
from jax.experimental.pallas import tpu as pltpu

F32 = jnp.float32
BF16 = jnp.bfloat16

D_MODEL = 1024
DEPTH = 4
CHUNK = 64
HEAD_DIM = 64
BRANCH_WIDTH = 512
N_BRANCH = 3
A_HEADS = 4
B_HEADS = 8
C_HEADS = 4
C_HEAD_DIM = 128
B_LEFT_CHUNKS = 8
B_PAD = B_LEFT_CHUNKS * CHUNK
B_MAX_REL = 128
T5_BUCKETS = 32
T5_MAX_DIST = 128
D_FF = 2816
QKV_COLS = 3 * N_BRANCH * BRANCH_WIDTH
GATE_COLS = N_BRANCH * D_MODEL
IN_COLS = QKV_COLS + GATE_COLS
NEG_INF = -1e30
EPS = 1e-6
LOG2E = math.log2(math.e)

LANES = 128
N_COLBLK = IN_COLS // LANES
GATE_BLKS = GATE_COLS // LANES
AQ0, AK0, AV0 = GATE_BLKS, GATE_BLKS + 4, GATE_BLKS + 8
BQ0, BK0, BV0 = GATE_BLKS + 12, GATE_BLKS + 16, GATE_BLKS + 20
CQ0, CK0, CV0 = GATE_BLKS + 24, GATE_BLKS + 28, GATE_BLKS + 32

TM_PROJ = 1024
TN_PROJ = 1536
TM_MERGE = 512
TM_FFN = 512
TF_FFN = 1408
TQ_A = 1024
TK_A = 512
UNROLL_A = 2
RQ_B = 128
UNROLL_B = 4
TQ_C = 256
GROUP_C = 4
C_SKIP_LOG2 = -150.0
VMEM_LIMIT = 56 * 1024 * 1024


def _cparams(*sem, flags=None):
    return pltpu.CompilerParams(dimension_semantics=sem, vmem_limit_bytes=VMEM_LIMIT, flags=flags)


def _ada_kernel(c_ref, w_ref, b_ref, o_ref):
    c = c_ref[...]
    cs = c * jax.nn.sigmoid(c)
    o_ref[...] = jnp.dot(cs, w_ref[...], preferred_element_type=F32,
                         precision=lax.Precision.HIGHEST) + b_ref[...]


def _ada_mod(c, w_ada, b_ada):
    depth, d, e = w_ada.shape
    bsz = c.shape[0]
    nblk = e // d
    return pl.pallas_call(
        _ada_kernel,
        grid=(depth, nblk),
        in_specs=[
            pl.BlockSpec((bsz, d), lambda l, j: (0, 0)),
            pl.BlockSpec((None, d, d), lambda l, j: (l, 0, j)),
            pl.BlockSpec((None, 1, d), lambda l, j: (l, 0, j)),
        ],
        out_specs=pl.BlockSpec((None, bsz, d), lambda l, j: (l, 0, j)),
        out_shape=jax.ShapeDtypeStruct((depth, bsz, e), F32),
        compiler_params=_cparams("arbitrary", "arbitrary"),
        name="ada_mod",
    )(c, w_ada, b_ada.reshape(depth, 1, e))


def _norm_mod(x, g, shift, scale):
    ms = jnp.mean(x * x, axis=-1, keepdims=True)
    y = x * lax.rsqrt(ms + EPS) * g
    return y * (1.0 + scale) + shift


def _inproj_kernel(x_ref, mod_ref, g_ref, w_ref, o_ref, h_ref):
    @pl.when(pl.program_id(1) == 0)
    def _():
        h = _norm_mod(x_ref[...], g_ref[...], mod_ref[0:1, :], mod_ref[1:2, :])
        h_ref[...] = h.astype(BF16)

    res = jnp.dot(h_ref[...], w_ref[...], preferred_element_type=F32)
    for cb in range(o_ref.shape[0]):
        o_ref[cb] = res[:, cb * LANES:(cb + 1) * LANES].astype(o_ref.dtype)


def _inproj(x2d, mod_l, g, w, seq):
    t, d = x2d.shape
    n = w.shape[1]
    tm = min(TM_PROJ, seq)
    tn = TN_PROJ
    cpb = tn // LANES
    return pl.pallas_call(
        _inproj_kernel,
        grid=(t // tm, n // tn),
        in_specs=[
            pl.BlockSpec((tm, d), lambda i, j: (i, 0)),
            pl.BlockSpec((None, 6, d), lambda i, j: ((i * tm) // seq, 0, 0)),
            pl.BlockSpec((1, d), lambda i, j: (0, 0)),
            pl.BlockSpec((d, tn), lambda i, j: (0, j)),
        ],
        out_specs=pl.BlockSpec((cpb, tm, LANES), lambda i, j: (j, i, 0)),
        out_shape=jax.ShapeDtypeStruct((n // LANES, t, LANES), BF16),
        scratch_shapes=[pltpu.VMEM((tm, d), BF16)],
        compiler_params=_cparams("parallel", "arbitrary"),
        name="in_proj",
    )(x2d, mod_l, g, w)


def _t5_bucket_np(rel):
    nb = T5_BUCKETS // 2
    max_exact = nb // 2
    ret = np.where(rel > 0, nb, 0)
    n = np.abs(rel)
    nf = np.maximum(n, 1).astype(np.float32)
    scaled = (np.log(nf / np.float32(max_exact)) / np.float32(math.log(T5_MAX_DIST / max_exact))
              * np.float32(nb - max_exact))
    large = max_exact + scaled.astype(np.int32)
    large = np.minimum(large, nb - 1)
    return (ret + np.where(n < max_exact, n, large)).astype(np.int32)


def _a_bias_tiles(t5_table, tq, tk):
    r = np.arange(tq)[None, :]
    c = np.arange(tk)[:, None]
    far_bucket = _t5_bucket_np((c - 2 * tk) - r)
    assert (far_bucket == far_bucket[0, 0]).all() and far_bucket[0, 0] == _t5_bucket_np(np.array(-10 * tq))
    far = t5_table[int(far_bucket[0, 0])].astype(F32)
    tiles = [jnp.zeros((A_HEADS, tk, tq), F32)]
    for koff in range(-tk, tq, tk):
        rel = (c + koff) - r
        bias = jnp.transpose(t5_table[_t5_bucket_np(rel)], (2, 0, 1)).astype(F32)
        bias = (bias - far[:, None, None]) * LOG2E
        allowed = ((c + koff) // CHUNK) <= (r // CHUNK)
        tiles.append(jnp.where(jnp.asarray(allowed)[None], bias, NEG_INF))
    tiles.append(jnp.full((A_HEADS, tk, tq), NEG_INF, F32))
    return jnp.stack(tiles, axis=1)


def _attn_a_kernel(lq1_ref, lk1_ref, lq2_ref, lk2_ref, subg_ref, bias_ref, q_ref, k_ref, v_ref,
                   o_ref, *, tq, tk, unroll, lam_init):
    seq = q_ref.shape[0]
    nq = seq // tq
    nkb = seq // tk
    per_q = tq // tk
    lam = (jnp.exp(jnp.sum(lq1_ref[...] * lk1_ref[...], axis=-1, keepdims=True))
           - jnp.exp(jnp.sum(lq2_ref[...] * lk2_ref[...], axis=-1, keepdims=True)) + lam_init)
    lane = lax.broadcasted_iota(jnp.int32, (1, LANES), 1)
    mask1 = (lane < HEAD_DIM).astype(BF16)
    mask2 = (lane >= HEAD_DIM).astype(BF16)
    subg = subg_ref[...] * (1.0 - lam_init)

    def qblock(i, carry):
        r0 = pl.multiple_of(i * tq, tq)
        qb = q_ref[pl.ds(r0, tq), :]
        qq = jnp.concatenate([qb * mask1, qb * mask2], axis=0)
        n_blocks = per_q * (i + 1)

        def block(js, st, with_bias):
            m, l, acc = st
            c0 = pl.multiple_of(jnp.minimum(js, nkb - 1) * tk, tk)
            kb = k_ref[pl.ds(c0, tk), :]
            vb = v_ref[pl.ds(c0, tk), :]
            s = lax.dot_general(kb, qq, (((1,), (1,)), ((), ())), preferred_element_type=F32)
            if with_bias:
                bias = bias_ref[jnp.clip(js - per_q * i + 2, 0, per_q + 2)]
                s = s + jnp.concatenate([bias, bias], axis=1)
            m_new = jnp.maximum(m, jnp.max(s, axis=0, keepdims=True))
            alpha = jnp.exp2(m - m_new)
            p = jnp.exp2(s - m_new)
            l = alpha * l + jnp.sum(p, axis=0, keepdims=True)
            pv = lax.dot_general(vb, p.astype(BF16), (((0,), (0,)), ((), ())), preferred_element_type=F32)
            return m_new, l, alpha * acc + pv

        def step(it, st, with_bias):
            for u in range(unroll):
                st = block(it * unroll + u, st, with_bias)
            return st

        st = (jnp.full((1, 2 * tq), NEG_INF, F32), jnp.zeros((1, 2 * tq), F32),
              jnp.zeros((LANES, 2 * tq), F32))
        n_plain = jnp.maximum(per_q * i - 1, 0) // unroll
        n_steps = (n_blocks + unroll - 1) // unroll
        st = lax.fori_loop(0, n_plain, lambda it, s_: step(it, s_, False), st)
        st = lax.fori_loop(n_plain, n_steps, lambda it, s_: step(it, s_, True), st)
        _, l, acc = st
        o = acc / l
        d = o[:, :tq] - lam * o[:, tq:]
        y = d * lax.rsqrt(jnp.mean(d * d, axis=0, keepdims=True) + EPS) * subg
        o_ref[pl.ds(r0, tq), :] = jnp.transpose(y).astype(o_ref.dtype)
        return carry

    lax.fori_loop(0, nq, qblock, 0)


def _attn_a(proj, bias, lq1, lk1, lq2, lk2, subg, bsz, seq, lam_init):
    tq = min(TQ_A, seq // 2)
    tk = min(TK_A, tq)
    vec = lambda n: pl.BlockSpec((1, n), lambda b, h: (0, 0))
    col = lambda base: pl.BlockSpec((None, seq, LANES), lambda b, h: (base + h, b, 0))
    return pl.pallas_call(
        functools.partial(_attn_a_kernel, tq=tq, tk=tk, unroll=UNROLL_A, lam_init=lam_init),
        grid=(bsz, A_HEADS),
        in_specs=[vec(HEAD_DIM), vec(HEAD_DIM), vec(HEAD_DIM), vec(HEAD_DIM),
                  pl.BlockSpec((2 * HEAD_DIM, 1), lambda b, h: (0, 0)),
                  pl.BlockSpec((None, tq // tk + 3, tk, tq), lambda b, h: (h, 0, 0, 0)),
                  col(AQ0), col(AK0), col(AV0)],
        out_specs=pl.BlockSpec((None, seq, LANES), lambda b, h: (h, b, 0)),
        out_shape=jax.ShapeDtypeStruct((A_HEADS, bsz * seq, LANES), BF16),
        compiler_params=_cparams("parallel", "parallel"),
        name="attn_a",
    )(lq1, lk1, lq2, lk2, subg, bias, proj, proj, proj)


def _b_bias_tiles(rel_bias, rq):
    w = rq + B_PAD
    r = np.arange(rq)[:, None]
    c = np.arange(w)[None, :]
    idx = np.clip(r + B_PAD - c, -B_MAX_REL, B_MAX_REL) + B_MAX_REL
    lo = CHUNK * (r // CHUNK)
    in_band = (c >= lo) & (c < lo + B_PAD + CHUNK)
    bias = rel_bias[:, idx].astype(F32) * LOG2E
    bias = jnp.where(jnp.asarray(in_band)[None], bias, NEG_INF)
    return bias.reshape(B_HEADS // 2, 2 * rq, w)


def _attn_b_kernel(bias_ref, q_ref, k_ref, v_ref, o_ref, kp_ref, vp_ref, *, rq, unroll):
    seq = q_ref.shape[0]
    w = rq + B_PAD
    nq = seq // rq
    zeros = jnp.zeros((B_PAD, LANES), BF16)
    kp_ref[0:B_PAD, :] = zeros
    vp_ref[0:B_PAD, :] = zeros
    kp_ref[B_PAD:B_PAD + seq, :] = k_ref[...]
    vp_ref[B_PAD:B_PAD + seq, :] = v_ref[...]
    lane = lax.broadcasted_iota(jnp.int32, (1, LANES), 1)
    mask1 = (lane < HEAD_DIM).astype(BF16)
    mask2 = (lane >= HEAD_DIM).astype(BF16)
    colpos = lax.broadcasted_iota(jnp.int32, (2 * rq, w), 1)
    lane_o = lax.broadcasted_iota(jnp.int32, (rq, LANES), 1)

    def qblock(r0, at_start):
        qb = q_ref[pl.ds(r0, rq), :]
        qq = jnp.concatenate([qb * mask1, qb * mask2], axis=0)
        kb = kp_ref[pl.ds(r0, w), :]
        vb = vp_ref[pl.ds(r0, w), :]
        s = lax.dot_general(qq, kb, (((1,), (1,)), ((), ())), preferred_element_type=F32)
        s = s + bias_ref[...]
        if at_start:
            s = jnp.where(colpos + r0 >= B_PAD, s, NEG_INF)
        m = jnp.max(s, axis=-1, keepdims=True)
        p = jnp.exp2(s - m)
        den = jnp.sum(p, axis=-1, keepdims=True)
        o = jnp.dot(p.astype(BF16), vb, preferred_element_type=F32) / den
        o_ref[pl.ds(r0, rq), :] = jnp.where(lane_o < HEAD_DIM, o[:rq], o[rq:]).astype(o_ref.dtype)

    n_start = B_PAD // rq
    for b in range(n_start):
        qblock(b * rq, True)

    def step(it, carry):
        for u in range(unroll):
            qblock(pl.multiple_of((it * unroll + u) * rq, rq), False)
        return carry

    lax.fori_loop(n_start // unroll, nq // unroll, step, 0)


def _attn_b(proj, bias, bsz, seq):
    rq = RQ_B
    w = rq + B_PAD
    col = lambda base: pl.BlockSpec((None, seq, LANES), lambda b, h: (base + h, b, 0))
    return pl.pallas_call(
        functools.partial(_attn_b_kernel, rq=rq, unroll=UNROLL_B),
        grid=(bsz, B_HEADS // 2),
        in_specs=[pl.BlockSpec((None, 2 * rq, w), lambda b, h: (h, 0, 0)),
                  col(BQ0), col(BK0), col(BV0)],
        out_specs=pl.BlockSpec((None, seq, LANES), lambda b, h: (h, b, 0)),
        out_shape=jax.ShapeDtypeStruct((B_HEADS // 2, bsz * seq, LANES), BF16),
        scratch_shapes=[pltpu.VMEM((seq + B_PAD, LANES), BF16), pltpu.VMEM((seq + B_PAD, LANES), BF16)],
        compiler_params=_cparams("parallel", "parallel"),
        name="attn_b",
    )(bias, proj, proj, proj)


def _attn_c_kernel(q_ref, k_ref, v_ref, o_ref, *, tq, group):
    seq = q_ref.shape[0]
    nq = seq // tq
    row = lax.broadcasted_iota(jnp.int32, (tq, tq), 0)
    colm = lax.broadcasted_iota(jnp.int32, (tq, tq), 1)
    tri_strict = (row > colm).astype(BF16)
    before = colm < row

    def block(i, j, carry, acc, diag):
        qb = q_ref[pl.ds(pl.multiple_of(i * tq, tq), tq), :]
        c0 = pl.multiple_of(j * tq, tq)
        kb = k_ref[pl.ds(c0, tq), :]
        vb = v_ref[pl.ds(c0, tq), :]
        z = lax.dot_general(qb, kb, (((1,), (1,)), ((), ())), preferred_element_type=F32)
        lm = -(jnp.maximum(z, 0.0) + jnp.log2(1.0 + jnp.exp2(-jnp.abs(z))))
        if diag:
            lm = jnp.where(before, lm, 0.0)
        lm_b = lm.astype(BF16)
        excl = jnp.dot(lm_b, tri_strict, preferred_element_type=F32)
        logw = (z + lm) + excl + carry
        wgt = jnp.exp2(logw)
        if diag:
            wgt = jnp.where(before, wgt, 0.0)
        acc = acc + jnp.dot(wgt.astype(BF16), vb, preferred_element_type=F32)
        carry = carry + excl[:, 0:1] + lm_b[:, 0:1].astype(F32)
        return carry, acc

    def qgroup(gi, c):
        i0 = gi * group
        sts = [block(i0 + g, i0 + g, jnp.zeros((tq, 1), F32), jnp.zeros((tq, LANES), F32), True)
               for g in range(group)]
        carries = tuple(st[0] for st in sts)
        accs = tuple(st[1] for st in sts)

        def live(carries):
            return functools.reduce(jnp.maximum, [jnp.max(cr) for cr in carries])

        def cond(st):
            t, cmax, _, _ = st
            return jnp.logical_and(t <= i0 + group - 1, cmax > C_SKIP_LOG2)

        def body(st):
            t, _, carries, accs = st
            new = []
            for g in range(group):
                j = i0 + g - t
                cin = jnp.where(j >= 0, carries[g], NEG_INF)
                new.append(block(i0 + g, jnp.maximum(j, 0), cin, accs[g], False))
            carries = tuple(st_[0] for st_ in new)
            return t + 1, live(carries), carries, tuple(st_[1] for st_ in new)

        _, _, _, accs = lax.while_loop(cond, body, (jnp.int32(1), live(carries), carries, accs))
        for g in range(group):
            o_ref[pl.ds(pl.multiple_of((i0 + g) * tq, tq), tq), :] = accs[g].astype(o_ref.dtype)
        return c

    lax.fori_loop(0, nq // group, qgroup, 0)


def _attn_c(proj, bsz, seq):
    tq = min(TQ_C, seq)
    col = lambda base: pl.BlockSpec((None, seq, LANES), lambda b, h: (base + h, b, 0))
    return pl.pallas_call(
        functools.partial(_attn_c_kernel, tq=tq, group=GROUP_C),
        grid=(bsz, C_HEADS),
        in_specs=[col(CQ0), col(CK0), col(CV0)],
        out_specs=pl.BlockSpec((None, seq, LANES), lambda b, h: (h, b, 0)),
        out_shape=jax.ShapeDtypeStruct((C_HEADS, bsz * seq, LANES), BF16),
        compiler_params=_cparams("parallel", "parallel"),
        name="attn_c",
    )(proj, proj, proj)


def _cat_lanes(ref, start, count):
    return jnp.concatenate([ref[start + c] for c in range(count)], axis=1)


def _merge_kernel(x_ref, mod_ref, gate_ref, bg_ref, ya_ref, yb_ref, yc_ref, wb_ref, wo_ref, o_ref):
    blks = D_MODEL // LANES
    merged = None
    for r, y_ref in enumerate((ya_ref, yb_ref, yc_ref)):
        y = _cat_lanes(y_ref, 0, BRANCH_WIDTH // LANES)
        br = jnp.dot(y, wb_ref[r], preferred_element_type=F32)
        pre = _cat_lanes(gate_ref, r * blks, blks).astype(F32) + bg_ref[:, r * D_MODEL:(r + 1) * D_MODEL]
        term = jax.nn.sigmoid(pre) * br
        merged = term if merged is None else merged + term
    out = jnp.dot(merged.astype(BF16), wo_ref[...], preferred_element_type=F32)
    o_ref[...] = x_ref[...] + mod_ref[2:3, :] * out


def _merge(x2d, mod_l, proj, b_gate, ya, yb, yc, w_branch, w_out, seq):
    t, d = x2d.shape
    tm = min(TM_MERGE, seq)
    ybs = lambda: pl.BlockSpec((BRANCH_WIDTH // LANES, tm, LANES), lambda i: (0, i, 0))
    return pl.pallas_call(
        _merge_kernel,
        grid=(t // tm,),
        in_specs=[
            pl.BlockSpec((tm, d), lambda i: (i, 0)),
            pl.BlockSpec((None, 6, d), lambda i: ((i * tm) // seq, 0, 0)),
            pl.BlockSpec((GATE_BLKS, tm, LANES), lambda i: (0, i, 0)),
            pl.BlockSpec((1, GATE_COLS), lambda i: (0, 0)),
            ybs(), ybs(), ybs(),
            pl.BlockSpec((N_BRANCH, BRANCH_WIDTH, d), lambda i: (0, 0, 0)),
            pl.BlockSpec((d, d), lambda i: (0, 0)),
        ],
        out_specs=pl.BlockSpec((tm, d), lambda i: (i, 0)),
        out_shape=jax.ShapeDtypeStruct((t, d), F32),
        compiler_params=_cparams("parallel"),
        name="merge",
    )(x2d, mod_l, proj, b_gate, ya, yb, yc, w_branch, w_out)


def _ffn_kernel(x_ref, mod_ref, g_ref, fg_ref, w1_ref, w3_ref, w2_ref, o_ref, h_ref, acc_ref, *, final):
    f = pl.program_id(1)

    @pl.when(f == 0)
    def _():
        h = _norm_mod(x_ref[...], g_ref[...], mod_ref[3:4, :], mod_ref[4:5, :])
        h_ref[...] = h.astype(BF16)

    h = h_ref[...]
    u_gate = jnp.dot(h, w1_ref[...], preferred_element_type=F32)
    u_up = jnp.dot(h, w3_ref[...], preferred_element_type=F32)
    act = (u_gate * jax.nn.sigmoid(u_gate) * u_up).astype(BF16)
    part = jnp.dot(act, w2_ref[...], preferred_element_type=F32)

    @pl.when(f == 0)
    def _():
        acc_ref[...] = part

    @pl.when(f > 0)
    def _():
        acc_ref[...] += part

    @pl.when(f == pl.num_programs(1) - 1)
    def _():
        y = x_ref[...] + mod_ref[5:6, :] * acc_ref[...]
        if final:
            y = y * lax.rsqrt(jnp.mean(y * y, axis=-1, keepdims=True) + EPS) * fg_ref[...]
        o_ref[...] = y


def _ffn(x2d, mod_l, g, final_g, w13, w2, seq, final):
    t, d = x2d.shape
    dff = w2.shape[0]
    tm = min(TM_FFN, seq)
    tf = TF_FFN
    nf = dff // tf
    return pl.pallas_call(
        functools.partial(_ffn_kernel, final=final),
        grid=(t // tm, nf),
        in_specs=[
            pl.BlockSpec((tm, d), lambda i, f: (i, 0)),
            pl.BlockSpec((None, 6, d), lambda i, f: ((i * tm) // seq, 0, 0)),
            pl.BlockSpec((1, d), lambda i, f: (0, 0)),
            pl.BlockSpec((1, d), lambda i, f: (0, 0)),
            pl.BlockSpec((d, tf), lambda i, f: (0, f)),
            pl.BlockSpec((d, tf), lambda i, f: (0, f + nf)),
            pl.BlockSpec((tf, d), lambda i, f: (f, 0)),
        ],
        out_specs=pl.BlockSpec((tm, d), lambda i, f: (i, 0)),
        out_shape=jax.ShapeDtypeStruct((t, d), F32),
        scratch_shapes=[pltpu.VMEM((tm, d), BF16), pltpu.VMEM((tm, d), F32)],
        compiler_params=_cparams("parallel", "arbitrary"),
        name="ffn",
    )(x2d, mod_l, g, final_g, w13, w13, w2)


def _prep_w_in(w_in):
    scale = np.ones((IN_COLS,), np.float32)
    scale[0 * BRANCH_WIDTH:1 * BRANCH_WIDTH] = HEAD_DIM ** -0.5 * LOG2E
    scale[3 * BRANCH_WIDTH:4 * BRANCH_WIDTH] = HEAD_DIM ** -0.5 * LOG2E
    scale[6 * BRANCH_WIDTH:7 * BRANCH_WIDTH] = C_HEAD_DIM ** -0.5 * LOG2E
    w = w_in * jnp.asarray(scale)
    return jnp.concatenate([w[..., QKV_COLS:], w[..., :QKV_COLS]], axis=-1).astype(BF16)


def kernel(x, c, w_ada, b_ada, norm1_g, w_in, b_gate, lam_q1, lam_k1, lam_q2, lam_k2, subln_g,
           t5_table, rel_bias_b, w_branch, w_out, norm2_g, w13, w2, final_g):
    bsz, seq, d = x.shape
    depth = w_in.shape[0]
    t = bsz * seq

    mod = _ada_mod(c, w_ada, b_ada).reshape(depth, bsz, 6, d)
    w_in_b = _prep_w_in(w_in)
    w_branch_b = w_branch.astype(BF16)
    w_out_b = w_out.astype(BF16)
    w13_b = w13.astype(BF16)
    w2_b = w2.astype(BF16)
    tq_a = min(TQ_A, seq // 2)
    a_bias = _a_bias_tiles(t5_table, tq_a, min(TK_A, tq_a))
    fg = final_g.reshape(1, d)

    x2d = x.reshape(t, d)
    for l in range(depth):
        lam_init = 0.8 - 0.6 * math.exp(-0.3 * l)
        proj = _inproj(x2d, mod[l], norm1_g[l].reshape(1, d), w_in_b[l], seq)
        ya = _attn_a(proj, a_bias, lam_q1[l].reshape(1, -1), lam_k1[l].reshape(1, -1),
                     lam_q2[l].reshape(1, -1), lam_k2[l].reshape(1, -1), subln_g[l].reshape(-1, 1),
                     bsz, seq, lam_init)
        yb = _attn_b(proj, _b_bias_tiles(rel_bias_b[l], RQ_B), bsz, seq)
        yc = _attn_c(proj, bsz, seq)
        x2d = _merge(x2d, mod[l], proj, b_gate[l].reshape(1, -1), ya, yb, yc,
                     w_branch_b[l], w_out_b[l], seq)
        x2d = _ffn(x2d, mod[l], norm2_g[l].reshape(1, d), fg, w13_b[l], w2_b[l], seq,
                   final=(l == depth - 1))
    return x2d.reshape(bsz, seq, d)
```

```python
import functools
import math

import numpy as np
import jax
import jax.numpy as jnp
from jax import lax
from jax.experimental import pallas as pl
from jax.experimental.pallas import tpu as pltpu

F32 = jnp.float32
BF16 = jnp.bfloat16

D_MODEL = 1024
DEPTH = 4
CHUNK = 64
HEAD_DIM = 64
BRANCH_WIDTH = 512
N_BRANCH = 3
A_HEADS = 4
B_HEADS = 8
C_HEADS = 4
C_HEAD_DIM = 128
B_LEFT_CHUNKS = 8
B_PAD = B_LEFT_CHUNKS * CHUNK
B_MAX_REL = 128
T5_BUCKETS = 32
T5_MAX_DIST = 128
D_FF = 2816
QKV_COLS = 3 * N_BRANCH * BRANCH_WIDTH
GATE_COLS = N_BRANCH * D_MODEL
IN_COLS = QKV_COLS + GATE_COLS
NEG_INF = -1e30
EPS = 1e-6
LOG2E = math.log2(math.e)

LANES = 128
N_COLBLK = IN_COLS // LANES
GATE_BLKS = GATE_COLS // LANES
AQ0, AK0, AV0 = GATE_BLKS, GATE_BLKS + 4, GATE_BLKS + 8
BQ0, BK0, BV0 = GATE_BLKS + 12, GATE_BLKS + 16, GATE_BLKS + 20
CQ0, CK0, CV0 = GATE_BLKS + 24, GATE_BLKS + 28, GATE_BLKS + 32

TM_PROJ = 1024
TN_PROJ = 1536
TM_MERGE = 512
TM_FFN = 512
TF_FFN = 1408
TQ_A = 1024
TK_A = 512
UNROLL_A = 2
RQ_B = 128
UNROLL_B = 4
TQ_C = 256
GROUP_C = 4
C_SKIP_LOG2 = -150.0
VMEM_LIMIT = 56 * 1024 * 1024


def _cparams(*sem, flags=None):
    return pltpu.CompilerParams(dimension_semantics=sem, vmem_limit_bytes=VMEM_LIMIT, flags=flags)


def _ada_kernel(c_ref, w_ref, b_ref, o_ref):
    c = c_ref[...]
    cs = c * jax.nn.sigmoid(c)
    o_ref[...] = jnp.dot(cs, w_ref[...], preferred_element_type=F32,
                         precision=lax.Precision.HIGHEST) + b_ref[...]


def _ada_mod(c, w_ada, b_ada):
    depth, d, e = w_ada.shape
    bsz = c.shape[0]
    nblk = e // d
    return pl.pallas_call(
        _ada_kernel,
        grid=(depth, nblk),
        in_specs=[
            pl.BlockSpec((bsz, d), lambda l, j: (0, 0)),
            pl.BlockSpec((None, d, d), lambda l, j: (l, 0, j)),
            pl.BlockSpec((None, 1, d), lambda l, j: (l, 0, j)),
        ],
        out_specs=pl.BlockSpec((None, bsz, d), lambda l, j: (l, 0, j)),
        out_shape=jax.ShapeDtypeStruct((depth, bsz, e), F32),
        compiler_params=_cparams("arbitrary", "arbitrary"),
        name="ada_mod",
    )(c, w_ada, b_ada.reshape(depth, 1, e))


def _norm_mod(x, g, shift, scale):
    ms = jnp.mean(x * x, axis=-1, keepdims=True)
    y = x * lax.rsqrt(ms + EPS) * g
    return y * (1.0 + scale) + shift


def _inproj_kernel(x_ref, mod_ref, g_ref, w_ref, o_ref, h_ref):
    @pl.when(pl.program_id(1) == 0)
    def _():
        h = _norm_mod(x_ref[...], g_ref[...], mod_ref[0:1, :], mod_ref[1:2, :])
        h_ref[...] = h.astype(BF16)

    res = jnp.dot(h_ref[...], w_ref[...], preferred_element_type=F32)
    for cb in range(o_ref.shape[0]):
        o_ref[cb] = res[:, cb * LANES:(cb + 1) * LANES].astype(o_ref.dtype)


def _inproj(x2d, mod_l, g, w, seq):
    t, d = x2d.shape
    n = w.shape[1]
    tm = min(TM_PROJ, seq)
    tn = TN_PROJ
    cpb = tn // LANES
    return pl.pallas_call(
        _inproj_kernel,
        grid=(t // tm, n // tn),
        in_specs=[
            pl.BlockSpec((tm, d), lambda i, j: (i, 0)),
            pl.BlockSpec((None, 6, d), lambda i, j: ((i * tm) // seq, 0, 0)),
            pl.BlockSpec((1, d), lambda i, j: (0, 0)),
            pl.BlockSpec((d, tn), lambda i, j: (0, j)),
        ],
        out_specs=pl.BlockSpec((cpb, tm, LANES), lambda i, j: (j, i, 0)),
        out_shape=jax.ShapeDtypeStruct((n // LANES, t, LANES), BF16),
        scratch_shapes=[pltpu.VMEM((tm, d), BF16)],
        compiler_params=_cparams("parallel", "arbitrary"),
        name="in_proj",
    )(x2d, mod_l, g, w)


def _t5_bucket_np(rel):
    nb = T5_BUCKETS // 2
    max_exact = nb // 2
    ret = np.where(rel > 0, nb, 0)
    n = np.abs(rel)
    nf = np.maximum(n, 1).astype(np.float32)
    scaled = (np.log(nf / np.float32(max_exact)) / np.float32(math.log(T5_MAX_DIST / max_exact))
              * np.float32(nb - max_exact))
    large = max_exact + scaled.astype(np.int32)
    large = np.minimum(large, nb - 1)
    return (ret + np.where(n < max_exact, n, large)).astype(np.int32)


def _toeplitz(vec, rows, cols):
    length = rows + cols - 1
    assert vec.shape[-1] == length
    lead = vec.shape[:-1]
    ext = jnp.concatenate([vec, jnp.zeros(lead + (1,), vec.dtype)], axis=-1)
    flat = jnp.broadcast_to(ext[..., None, :], lead + (rows, length + 1)).reshape(lead + (rows * (length + 1),))
    return flat[..., :rows * length].reshape(lead + (rows, length))[..., rows - 1:rows - 1 + cols]


def _a_bias_tiles(t5_table, tq, tk):
    r = np.arange(tq)[None, :]
    c = np.arange(tk)[:, None]
    far_bucket = _t5_bucket_np((c - 2 * tk) - r)
    assert (far_bucket == far_bucket[0, 0]).all() and far_bucket[0, 0] == _t5_bucket_np(np.array(-10 * tq))
    far = t5_table[int(far_bucket[0, 0])].astype(F32)
    tiles = [jnp.zeros((A_HEADS, tk, tq), F32)]
    u = np.arange(tk + tq - 1)
    for koff in range(-tk, tq, tk):
        vec = t5_table[_t5_bucket_np(koff + tk - 1 - u)].astype(F32).T
        bias = _toeplitz((vec - far[:, None]) * LOG2E, tk, tq)
        allowed = ((c + koff) // CHUNK) <= (r // CHUNK)
        tiles.append(jnp.where(jnp.asarray(allowed)[None], bias, NEG_INF))
    tiles.append(jnp.full((A_HEADS, tk, tq), NEG_INF, F32))
    return jnp.stack(tiles, axis=1)


def _attn_a_kernel(lq1_ref, lk1_ref, lq2_ref, lk2_ref, subg_ref, bias_ref, q_ref, k_ref, v_ref,
                   o_ref, *, tq, tk, unroll, lam_init):
    seq = q_ref.shape[0]
    nq = seq // tq
    nkb = seq // tk
    per_q = tq // tk
    lam = (jnp.exp(jnp.sum(lq1_ref[...] * lk1_ref[...], axis=-1, keepdims=True))
           - jnp.exp(jnp.sum(lq2_ref[...] * lk2_ref[...], axis=-1, keepdims=True)) + lam_init)
    lane = lax.broadcasted_iota(jnp.int32, (1, LANES), 1)
    mask1 = (lane < HEAD_DIM).astype(BF16)
    mask2 = (lane >= HEAD_DIM).astype(BF16)
    subg = subg_ref[...] * (1.0 - lam_init)

    def qblock(i, carry):
        r0 = pl.multiple_of(i * tq, tq)
        qb = q_ref[pl.ds(r0, tq), :]
        qq = jnp.concatenate([qb * mask1, qb * mask2], axis=0)
        n_blocks = per_q * (i + 1)

        def block(js, st, with_bias):
            m, l, acc = st
            c0 = pl.multiple_of(jnp.minimum(js, nkb - 1) * tk, tk)
            kb = k_ref[pl.ds(c0, tk), :]
            vb = v_ref[pl.ds(c0, tk), :]
            s = lax.dot_general(kb, qq, (((1,), (1,)), ((), ())), preferred_element_type=F32)
            if with_bias:
                bias = bias_ref[jnp.clip(js - per_q * i + 2, 0, per_q + 2)]
                s = s + jnp.concatenate([bias, bias], axis=1)
            m_new = jnp.maximum(m, jnp.max(s, axis=0, keepdims=True))
            alpha = jnp.exp2(m - m_new)
            p = jnp.exp2(s - m_new)
            l = alpha * l + jnp.sum(p, axis=0, keepdims=True)
            pv = lax.dot_general(vb, p.astype(BF16), (((0,), (0,)), ((), ())), preferred_element_type=F32)
            return m_new, l, alpha * acc + pv

        def step(it, st, with_bias):
            for u in range(unroll):
                st = block(it * unroll + u, st, with_bias)
            return st

        st = (jnp.full((1, 2 * tq), NEG_INF, F32), jnp.zeros((1, 2 * tq), F32),
              jnp.zeros((LANES, 2 * tq), F32))
        n_plain = jnp.maximum(per_q * i - 1, 0) // unroll
        n_steps = (n_blocks + unroll - 1) // unroll
        st = lax.fori_loop(0, n_plain, lambda it, s_: step(it, s_, False), st)
        st = lax.fori_loop(n_plain, n_steps, lambda it, s_: step(it, s_, True), st)
        _, l, acc = st
        o = acc / l
        d = o[:, :tq] - lam * o[:, tq:]
        y = d * lax.rsqrt(jnp.mean(d * d, axis=0, keepdims=True) + EPS) * subg
        o_ref[pl.ds(r0, tq), :] = jnp.transpose(y).astype(o_ref.dtype)
        return carry

    lax.fori_loop(0, nq, qblock, 0)


def _attn_a(proj, bias, lq1, lk1, lq2, lk2, subg, bsz, seq, lam_init):
    tq = min(TQ_A, seq // 2)
    tk = min(TK_A, tq)
    vec = lambda n: pl.BlockSpec((1, n), lambda b, h: (0, 0))
    col = lambda base: pl.BlockSpec((None, seq, LANES), lambda b, h: (base + h, b, 0))
    return pl.pallas_call(
        functools.partial(_attn_a_kernel, tq=tq, tk=tk, unroll=UNROLL_A, lam_init=lam_init),
        grid=(bsz, A_HEADS),
        in_specs=[vec(HEAD_DIM), vec(HEAD_DIM), vec(HEAD_DIM), vec(HEAD_DIM),
                  pl.BlockSpec((2 * HEAD_DIM, 1), lambda b, h: (0, 0)),
                  pl.BlockSpec((None, tq // tk + 3, tk, tq), lambda b, h: (h, 0, 0, 0)),
                  col(AQ0), col(AK0), col(AV0)],
        out_specs=pl.BlockSpec((None, seq, LANES), lambda b, h: (h, b, 0)),
        out_shape=jax.ShapeDtypeStruct((A_HEADS, bsz * seq, LANES), BF16),
        compiler_params=_cparams("parallel", "parallel"),
        name="attn_a",
    )(lq1, lk1, lq2, lk2, subg, bias, proj, proj, proj)


def _b_bias_tiles(rel_bias, rq):
    w = rq + B_PAD
    r = np.arange(rq)[:, None]
    c = np.arange(w)[None, :]
    lo = CHUNK * (r // CHUNK)
    in_band = (c >= lo) & (c < lo + B_PAD + CHUNK)
    u = np.arange(rq + w - 1)
    idx = np.clip(rq - 1 - u + B_PAD, -B_MAX_REL, B_MAX_REL) + B_MAX_REL
    bias = _toeplitz(rel_bias[:, idx].astype(F32) * LOG2E, rq, w)
    bias = jnp.where(jnp.asarray(in_band)[None], bias, NEG_INF)
    return bias.reshape(B_HEADS // 2, 2 * rq, w)


def _attn_b_kernel(bias_ref, q_ref, k_ref, v_ref, o_ref, kp_ref, vp_ref, *, rq, unroll):
    seq = q_ref.shape[0]
    w = rq + B_PAD
    nq = seq // rq
    zeros = jnp.zeros((B_PAD, LANES), BF16)
    kp_ref[0:B_PAD, :] = zeros
    vp_ref[0:B_PAD, :] = zeros
    kp_ref[B_PAD:B_PAD + seq, :] = k_ref[...]
    vp_ref[B_PAD:B_PAD + seq, :] = v_ref[...]
    lane = lax.broadcasted_iota(jnp.int32, (1, LANES), 1)
    mask1 = (lane < HEAD_DIM).astype(BF16)
    mask2 = (lane >= HEAD_DIM).astype(BF16)
    colpos = lax.broadcasted_iota(jnp.int32, (2 * rq, w), 1)
    lane_o = lax.broadcasted_iota(jnp.int32, (rq, LANES), 1)

    def qblock(r0, at_start):
        qb = q_ref[pl.ds(r0, rq), :]
        qq = jnp.concatenate([qb * mask1, qb * mask2], axis=0)
        kb = kp_ref[pl.ds(r0, w), :]
        vb = vp_ref[pl.ds(r0, w), :]
        s = lax.dot_general(qq, kb, (((1,), (1,)), ((), ())), preferred_element_type=F32)
        s = s + bias_ref[...]
        if at_start:
            s = jnp.where(colpos + r0 >= B_PAD, s, NEG_INF)
        m = jnp.max(s, axis=-1, keepdims=True)
        p = jnp.exp2(s - m)
        den = jnp.sum(p, axis=-1, keepdims=True)
        o = jnp.dot(p.astype(BF16), vb, preferred_element_type=F32) / den
        o_ref[pl.ds(r0, rq), :] = jnp.where(lane_o < HEAD_DIM, o[:rq], o[rq:]).astype(o_ref.dtype)

    n_start = B_PAD // rq
    for b in range(n_start):
        qblock(b * rq, True)

    def step(it, carry):
        for u in range(unroll):
            qblock(pl.multiple_of((it * unroll + u) * rq, rq), False)
        return carry

    lax.fori_loop(n_start // unroll, nq // unroll, step, 0)


def _attn_b(proj, bias, bsz, seq):
    rq = RQ_B
    w = rq + B_PAD
    col = lambda base: pl.BlockSpec((None, seq, LANES), lambda b, h: (base + h, b, 0))
    return pl.pallas_call(
        functools.partial(_attn_b_kernel, rq=rq, unroll=UNROLL_B),
        grid=(bsz, B_HEADS // 2),
        in_specs=[pl.BlockSpec((None, 2 * rq, w), lambda b, h: (h, 0, 0)),
                  col(BQ0), col(BK0), col(BV0)],
        out_specs=pl.BlockSpec((None, seq, LANES), lambda b, h: (h, b, 0)),
        out_shape=jax.ShapeDtypeStruct((B_HEADS // 2, bsz * seq, LANES), BF16),
        scratch_shapes=[pltpu.VMEM((seq + B_PAD, LANES), BF16), pltpu.VMEM((seq + B_PAD, LANES), BF16)],
        compiler_params=_cparams("parallel", "parallel"),
        name="attn_b",
    )(bias, proj, proj, proj)


def _attn_c_kernel(q_ref, k_ref, v_ref, o_ref, *, tq, group):
    seq = q_ref.shape[0]
    nq = seq // tq
    row = lax.broadcasted_iota(jnp.int32, (tq, tq), 0)
    colm = lax.broadcasted_iota(jnp.int32, (tq, tq), 1)
    tri_strict = (row > colm).astype(BF16)
    before = colm < row

    def block(i, j, carry, acc, diag):
        qb = q_ref[pl.ds(pl.multiple_of(i * tq, tq), tq), :]
        c0 = pl.multiple_of(j * tq, tq)
        kb = k_ref[pl.ds(c0, tq), :]
        vb = v_ref[pl.ds(c0, tq), :]
        z = lax.dot_general(qb, kb, (((1,), (1,)), ((), ())), preferred_element_type=F32)
        lm = -(jnp.maximum(z, 0.0) + jnp.log2(1.0 + jnp.exp2(-jnp.abs(z))))
        if diag:
            lm = jnp.where(before, lm, 0.0)
        lm_b = lm.astype(BF16)
        excl = jnp.dot(lm_b, tri_strict, preferred_element_type=F32)
        logw = (z + lm) + excl + carry
        wgt = jnp.exp2(logw)
        if diag:
            wgt = jnp.where(before, wgt, 0.0)
        acc = acc + jnp.dot(wgt.astype(BF16), vb, preferred_element_type=F32)
        carry = carry + excl[:, 0:1] + lm_b[:, 0:1].astype(F32)
        return carry, acc

    def qgroup(gi, c):
        i0 = gi * group
        sts = [block(i0 + g, i0 + g, jnp.zeros((tq, 1), F32), jnp.zeros((tq, LANES), F32), True)
               for g in range(group)]
        carries = tuple(st[0] for st in sts)
        accs = tuple(st[1] for st in sts)

        def live(carries):
            return functools.reduce(jnp.maximum, [jnp.max(cr) for cr in carries])

        def cond(st):
            t, cmax, _, _ = st
            return jnp.logical_and(t <= i0 + group - 1, cmax > C_SKIP_LOG2)

        def body(st):
            t, _, carries, accs = st
            new = []
            for g in range(group):
                j = i0 + g - t
                cin = jnp.where(j >= 0, carries[g], NEG_INF)
                new.append(block(i0 + g, jnp.maximum(j, 0), cin, accs[g], False))
            carries = tuple(st_[0] for st_ in new)
            return t + 1, live(carries), carries, tuple(st_[1] for st_ in new)

        _, _, _, accs = lax.while_loop(cond, body, (jnp.int32(1), live(carries), carries, accs))
        for g in range(group):
            o_ref[pl.ds(pl.multiple_of((i0 + g) * tq, tq), tq), :] = accs[g].astype(o_ref.dtype)
        return c

    lax.fori_loop(0, nq // group, qgroup, 0)


def _attn_c(proj, bsz, seq):
    tq = min(TQ_C, seq)
    col = lambda base: pl.BlockSpec((None, seq, LANES), lambda b, h: (base + h, b, 0))
    return pl.pallas_call(
        functools.partial(_attn_c_kernel, tq=tq, group=GROUP_C),
        grid=(bsz, C_HEADS),
        in_specs=[col(CQ0), col(CK0), col(CV0)],
        out_specs=pl.BlockSpec((None, seq, LANES), lambda b, h: (h, b, 0)),
        out_shape=jax.ShapeDtypeStruct((C_HEADS, bsz * seq, LANES), BF16),
        compiler_params=_cparams("parallel", "parallel"),
        name="attn_c",
    )(proj, proj, proj)


def _cat_lanes(ref, start, count):
    return jnp.concatenate([ref[start + c] for c in range(count)], axis=1)


def _merge_kernel(x_ref, mod_ref, gate_ref, bg_ref, ya_ref, yb_ref, yc_ref, wb_ref, wo_ref, o_ref):
    blks = D_MODEL // LANES
    merged = None
    for r, y_ref in enumerate((ya_ref, yb_ref, yc_ref)):
        y = _cat_lanes(y_ref, 0, BRANCH_WIDTH // LANES)
        br = jnp.dot(y, wb_ref[r], preferred_element_type=F32)
        pre = _cat_lanes(gate_ref, r * blks, blks).astype(F32) + bg_ref[:, r * D_MODEL:(r + 1) * D_MODEL]
        term = jax.nn.sigmoid(pre) * br
        merged = term if merged is None else merged + term
    out = jnp.dot(merged.astype(BF16), wo_ref[...], preferred_element_type=F32)
    o_ref[...] = x_ref[...] + mod_ref[2:3, :] * out


def _merge(x2d, mod_l, proj, b_gate, ya, yb, yc, w_branch, w_out, seq):
    t, d = x2d.shape
    tm = min(TM_MERGE, seq)
    ybs = lambda: pl.BlockSpec((BRANCH_WIDTH // LANES, tm, LANES), lambda i: (0, i, 0))
    return pl.pallas_call(
        _merge_kernel,
        grid=(t // tm,),
        in_specs=[
            pl.BlockSpec((tm, d), lambda i: (i, 0)),
            pl.BlockSpec((None, 6, d), lambda i: ((i * tm) // seq, 0, 0)),
            pl.BlockSpec((GATE_BLKS, tm, LANES), lambda i: (0, i, 0)),
            pl.BlockSpec((1, GATE_COLS), lambda i: (0, 0)),
            ybs(), ybs(), ybs(),
            pl.BlockSpec((N_BRANCH, BRANCH_WIDTH, d), lambda i: (0, 0, 0)),
            pl.BlockSpec((d, d), lambda i: (0, 0)),
        ],
        out_specs=pl.BlockSpec((tm, d), lambda i: (i, 0)),
        out_shape=jax.ShapeDtypeStruct((t, d), F32),
        compiler_params=_cparams("parallel"),
        name="merge",
    )(x2d, mod_l, proj, b_gate, ya, yb, yc, w_branch, w_out)


def _ffn_kernel(x_ref, mod_ref, g_ref, fg_ref, w1_ref, w3_ref, w2_ref, o_ref, h_ref, acc_ref, *, final):
    f = pl.program_id(1)

    @pl.when(f == 0)
    def _():
        h = _norm_mod(x_ref[...], g_ref[...], mod_ref[3:4, :], mod_ref[4:5, :])
        h_ref[...] = h.astype(BF16)

    h = h_ref[...]
    u_gate = jnp.dot(h, w1_ref[...], preferred_element_type=F32)
    u_up = jnp.dot(h, w3_ref[...], preferred_element_type=F32)
    act = (u_gate * jax.nn.sigmoid(u_gate) * u_up).astype(BF16)
    part = jnp.dot(act, w2_ref[...], preferred_element_type=F32)

    @pl.when(f == 0)
    def _():
        acc_ref[...] = part

    @pl.when(f > 0)
    def _():
        acc_ref[...] += part

    @pl.when(f == pl.num_programs(1) - 1)
    def _():
        y = x_ref[...] + mod_ref[5:6, :] * acc_ref[...]
        if final:
            y = y * lax.rsqrt(jnp.mean(y * y, axis=-1, keepdims=True) + EPS) * fg_ref[...]
        o_ref[...] = y


def _ffn(x2d, mod_l, g, final_g, w13, w2, seq, final):
    t, d = x2d.shape
    dff = w2.shape[0]
    tm = min(TM_FFN, seq)
    tf = TF_FFN
    nf = dff // tf
    return pl.pallas_call(
        functools.partial(_ffn_kernel, final=final),
        grid=(t // tm, nf),
        in_specs=[
            pl.BlockSpec((tm, d), lambda i, f: (i, 0)),
            pl.BlockSpec((None, 6, d), lambda i, f: ((i * tm) // seq, 0, 0)),
            pl.BlockSpec((1, d), lambda i, f: (0, 0)),
            pl.BlockSpec((1, d), lambda i, f: (0, 0)),
            pl.BlockSpec((d, tf), lambda i, f: (0, f)),
            pl.BlockSpec((d, tf), lambda i, f: (0, f + nf)),
            pl.BlockSpec((tf, d), lambda i, f: (f, 0)),
        ],
        out_specs=pl.BlockSpec((tm, d), lambda i, f: (i, 0)),
        out_shape=jax.ShapeDtypeStruct((t, d), F32),
        scratch_shapes=[pltpu.VMEM((tm, d), BF16), pltpu.VMEM((tm, d), F32)],
        compiler_params=_cparams("parallel", "arbitrary"),
        name="ffn",
    )(x2d, mod_l, g, final_g, w13, w13, w2)


def _prep_w_in(w_in):
    scale = np.ones((IN_COLS,), np.float32)
    scale[0 * BRANCH_WIDTH:1 * BRANCH_WIDTH] = HEAD_DIM ** -0.5 * LOG2E
    scale[3 * BRANCH_WIDTH:4 * BRANCH_WIDTH] = HEAD_DIM ** -0.5 * LOG2E
    scale[6 * BRANCH_WIDTH:7 * BRANCH_WIDTH] = C_HEAD_DIM ** -0.5 * LOG2E
    w = w_in * jnp.asarray(scale)
    return jnp.concatenate([w[..., QKV_COLS:], w[..., :QKV_COLS]], axis=-1).astype(BF16)


def kernel(x, c, w_ada, b_ada, norm1_g, w_in, b_gate, lam_q1, lam_k1, lam_q2, lam_k2, subln_g,
           t5_table, rel_bias_b, w_branch, w_out, norm2_g, w13, w2, final_g):
    bsz, seq, d = x.shape
    depth = w_in.shape[0]
    t = bsz * seq

    mod = _ada_mod(c, w_ada, b_ada).reshape(depth, bsz, 6, d)
    w_in_b = _prep_w_in(w_in)
    w_branch_b = w_branch.astype(BF16)
    w_out_b = w_out.astype(BF16)
    w13_b = w13.astype(BF16)
    w2_b = w2.astype(BF16)
    tq_a = min(TQ_A, seq // 2)
    a_bias = _a_bias_tiles(t5_table, tq_a, min(TK_A, tq_a))
    fg = final_g.reshape(1, d)

    x2d = x.reshape(t, d)
    for l in range(depth):
        lam_init = 0.8 - 0.6 * math.exp(-0.3 * l)
        proj = _inproj(x2d, mod[l], norm1_g[l].reshape(1, d), w_in_b[l], seq)
        ya = _attn_a(proj, a_bias, lam_q1[l].reshape(1, -1), lam_k1[l].reshape(1, -1),
                     lam_q2[l].reshape(1, -1), lam_k2[l].reshape(1, -1), subln_g[l].reshape(-1, 1),
                     bsz, seq, lam_init)
        yb = _attn_b(proj, _b_bias_tiles(rel_bias_b[l], RQ_B), bsz, seq)
        yc = _attn_c(proj, bsz, seq)
        x2d = _merge(x2d, mod[l], proj, b_gate[l].reshape(1, -1), ya, yb, yc,
                     w_branch_b[l], w_out_b[l], seq)
        x2d = _ffn(x2d, mod[l], norm2_g[l].reshape(1, d), fg, w13_b[l], w2_b[l], seq,
                   final=(l == depth - 1))
    return x2d.reshape(bsz, seq, d)
```

```python
import functools
import math

import numpy as np
import jax
import jax.numpy as jnp
from jax import lax
from jax.experimental import pallas as pl
from jax.experimental.pallas import tpu as pltpu

F32 = jnp.float32
BF16 = jnp.bfloat16

D_MODEL = 1024
DEPTH = 4
CHUNK = 64
HEAD_DIM = 64
BRANCH_WIDTH = 512
N_BRANCH = 3
A_HEADS = 4
B_HEADS = 8
C_HEADS = 4
C_HEAD_DIM = 128
B_LEFT_CHUNKS = 8
B_PAD = B_LEFT_CHUNKS * CHUNK
B_MAX_REL = 128
T5_BUCKETS = 32
T5_MAX_DIST = 128
D_FF = 2816
QKV_COLS = 3 * N_BRANCH * BRANCH_WIDTH
GATE_COLS = N_BRANCH * D_MODEL
IN_COLS = QKV_COLS + GATE_COLS
NEG_INF = -1e30
EPS = 1e-6
LOG2E = math.log2(math.e)

LANES = 128
N_COLBLK = IN_COLS // LANES
GATE_BLKS = GATE_COLS // LANES
AQ0, AK0, AV0 = GATE_BLKS, GATE_BLKS + 4, GATE_BLKS + 8
BQ0, BK0, BV0 = GATE_BLKS + 12, GATE_BLKS + 16, GATE_BLKS + 20
CQ0, CK0, CV0 = GATE_BLKS + 24, GATE_BLKS + 28, GATE_BLKS + 32

TM_PROJ = 1024
TN_PROJ = 1536
TM_MERGE = 512
TM_FFN = 512
TF_FFN = 1408
TQ_A = 1024
TK_A = 512
TP_A = 256
UNROLL_A = 2
RQ_B = 128
UNROLL_B = 8
TQ_C = 256
GROUP_C = 4
C_SKIP_LOG2 = -150.0
VMEM_LIMIT = 56 * 1024 * 1024


def _cparams(*sem, flags=None):
    return pltpu.CompilerParams(dimension_semantics=sem, vmem_limit_bytes=VMEM_LIMIT, flags=flags)


def _ada_kernel(c_ref, w_ref, b_ref, o_ref):
    c = c_ref[...]
    cs = c * jax.nn.sigmoid(c)
    o_ref[...] = jnp.dot(cs, w_ref[...], preferred_element_type=F32,
                         precision=lax.Precision.HIGHEST) + b_ref[...]


def _ada_mod(c, w_ada, b_ada):
    depth, d, e = w_ada.shape
    bsz = c.shape[0]
    nblk = e // d
    return pl.pallas_call(
        _ada_kernel,
        grid=(depth, nblk),
        in_specs=[
            pl.BlockSpec((bsz, d), lambda l, j: (0, 0)),
            pl.BlockSpec((None, d, d), lambda l, j: (l, 0, j)),
            pl.BlockSpec((None, 1, d), lambda l, j: (l, 0, j)),
        ],
        out_specs=pl.BlockSpec((None, bsz, d), lambda l, j: (l, 0, j)),
        out_shape=jax.ShapeDtypeStruct((depth, bsz, e), F32),
        compiler_params=_cparams("arbitrary", "arbitrary"),
        name="ada_mod",
    )(c, w_ada, b_ada.reshape(depth, 1, e))


def _norm_mod(x, g, shift, scale):
    ms = jnp.mean(x * x, axis=-1, keepdims=True)
    y = x * lax.rsqrt(ms + EPS) * g
    return y * (1.0 + scale) + shift


def _inproj_kernel(x_ref, mod_ref, g_ref, w_ref, o_ref, h_ref):
    @pl.when(pl.program_id(1) == 0)
    def _():
        h = _norm_mod(x_ref[...], g_ref[...], mod_ref[0:1, :], mod_ref[1:2, :])
        h_ref[...] = h.astype(BF16)

    res = jnp.dot(h_ref[...], w_ref[...], preferred_element_type=F32)
    for cb in range(o_ref.shape[0]):
        o_ref[cb] = res[:, cb * LANES:(cb + 1) * LANES].astype(o_ref.dtype)


def _inproj(x2d, mod_l, g, w, seq):
    t, d = x2d.shape
    n = w.shape[1]
    tm = min(TM_PROJ, seq)
    tn = TN_PROJ
    cpb = tn // LANES
    return pl.pallas_call(
        _inproj_kernel,
        grid=(t // tm, n // tn),
        in_specs=[
            pl.BlockSpec((tm, d), lambda i, j: (i, 0)),
            pl.BlockSpec((None, 6, d), lambda i, j: ((i * tm) // seq, 0, 0)),
            pl.BlockSpec((1, d), lambda i, j: (0, 0)),
            pl.BlockSpec((d, tn), lambda i, j: (0, j)),
        ],
        out_specs=pl.BlockSpec((cpb, tm, LANES), lambda i, j: (j, i, 0)),
        out_shape=jax.ShapeDtypeStruct((n // LANES, t, LANES), BF16),
        scratch_shapes=[pltpu.VMEM((tm, d), BF16)],
        compiler_params=_cparams("parallel", "arbitrary"),
        name="in_proj",
    )(x2d, mod_l, g, w)


def _t5_bucket_np(rel):
    nb = T5_BUCKETS // 2
    max_exact = nb // 2
    ret = np.where(rel > 0, nb, 0)
    n = np.abs(rel)
    nf = np.maximum(n, 1).astype(np.float32)
    scaled = (np.log(nf / np.float32(max_exact)) / np.float32(math.log(T5_MAX_DIST / max_exact))
              * np.float32(nb - max_exact))
    large = max_exact + scaled.astype(np.int32)
    large = np.minimum(large, nb - 1)
    return (ret + np.where(n < max_exact, n, large)).astype(np.int32)


def _toeplitz(vec, rows, cols):
    length = rows + cols - 1
    assert vec.shape[-1] == length
    lead = vec.shape[:-1]
    ext = jnp.concatenate([vec, jnp.zeros(lead + (1,), vec.dtype)], axis=-1)
    flat = jnp.broadcast_to(ext[..., None, :], lead + (rows, length + 1)).reshape(lead + (rows * (length + 1),))
    return flat[..., :rows * length].reshape(lead + (rows, length))[..., rows - 1:rows - 1 + cols]


def _a_bias_tiles(t5_table, tq, tk):
    r = np.arange(tq)[None, :]
    c = np.arange(tk)[:, None]
    far_bucket = _t5_bucket_np((c - 2 * tk) - r)
    assert (far_bucket == far_bucket[0, 0]).all() and far_bucket[0, 0] == _t5_bucket_np(np.array(-10 * tq))
    far = t5_table[int(far_bucket[0, 0])].astype(F32)
    tiles = [jnp.zeros((A_HEADS, tk, tq), F32)]
    u = np.arange(tk + tq - 1)
    for koff in range(-tk, tq, tk):
        vec = t5_table[_t5_bucket_np(koff + tk - 1 - u)].astype(F32).T
        bias = _toeplitz((vec - far[:, None]) * LOG2E, tk, tq)
        allowed = ((c + koff) // CHUNK) <= (r // CHUNK)
        tiles.append(jnp.where(jnp.asarray(allowed)[None], bias, NEG_INF))
    tiles.append(jnp.full((A_HEADS, tk, tq), NEG_INF, F32))
    return jnp.stack(tiles, axis=1)


def _attn_a_kernel(lq1_ref, lk1_ref, lq2_ref, lk2_ref, subg_ref, bias_ref, q_ref, k_ref, v_ref,
                   o_ref, vt_ref, acc_ref, *s_refs, tq, tk, lam_init):
    seq = q_ref.shape[0]
    nq = seq // tq
    nkb = seq // tk
    per_q = tq // tk
    unroll = len(s_refs)
    tp = TP_A
    lam = (jnp.exp(jnp.sum(lq1_ref[...] * lk1_ref[...], axis=-1, keepdims=True))
           - jnp.exp(jnp.sum(lq2_ref[...] * lk2_ref[...], axis=-1, keepdims=True)) + lam_init)
    lane = lax.broadcasted_iota(jnp.int32, (1, LANES), 1)
    mask1 = (lane < HEAD_DIM).astype(BF16)
    mask2 = (lane >= HEAD_DIM).astype(BF16)
    subg = subg_ref[...] * (1.0 - lam_init)

    def transpose_v(b, carry):
        c0 = pl.multiple_of(b * tk, tk)
        vt_ref[:, pl.ds(c0, tk)] = jnp.transpose(v_ref[pl.ds(c0, tk), :].astype(F32)).astype(BF16)
        return carry

    lax.fori_loop(0, nkb, transpose_v, 0)

    def qblock(i, carry):
        r0 = pl.multiple_of(i * tq, tq)
        qb = q_ref[pl.ds(r0, tq), :]
        qq = jnp.concatenate([qb * mask1, qb * mask2], axis=0)
        n_blocks = per_q * (i + 1)
        acc_ref[...] = jnp.zeros(acc_ref.shape, F32)

        def block(js, st, with_bias, s_ref):
            m, l = st
            c0 = pl.multiple_of(jnp.minimum(js, nkb - 1) * tk, tk)
            kb = k_ref[pl.ds(c0, tk), :]
            s = lax.dot_general(kb, qq, (((1,), (1,)), ((), ())), preferred_element_type=F32)
            if with_bias:
                bias = bias_ref[jnp.clip(js - per_q * i + 2, 0, per_q + 2)]
                s = s + jnp.concatenate([bias, bias], axis=1)
            s_ref[...] = s
            m_new = jnp.maximum(m, jnp.max(s, axis=0, keepdims=True))
            alpha = jnp.exp2(m - m_new)
            sums = []
            for ct in range(2 * tq // tp):
                cols = slice(ct * tp, (ct + 1) * tp)
                m_c = m_new[:, cols]
                acc_c = alpha[:, cols] * acc_ref[:, cols]
                l_c = None
                for kt in range(tk // tp):
                    p = jnp.exp2(s_ref[kt * tp:(kt + 1) * tp, cols] - m_c)
                    psum = jnp.sum(p, axis=0, keepdims=True)
                    l_c = psum if l_c is None else l_c + psum
                    vt = vt_ref[:, pl.ds(pl.multiple_of(c0 + kt * tp, tp), tp)]
                    acc_c = acc_c + jnp.dot(vt, p.astype(BF16), preferred_element_type=F32)
                acc_ref[:, cols] = acc_c
                sums.append(l_c)
            return m_new, alpha * l + jnp.concatenate(sums, axis=1)

        def step(it, st, with_bias):
            for u in range(unroll):
                st = block(it * unroll + u, st, with_bias, s_refs[u])
            return st

        st = (jnp.full((1, 2 * tq), NEG_INF, F32), jnp.zeros((1, 2 * tq), F32))
        n_plain = jnp.maximum(per_q * i - 1, 0) // unroll
        n_steps = (n_blocks + unroll - 1) // unroll
        st = lax.fori_loop(0, n_plain, lambda it, s_: step(it, s_, False), st)
        st = lax.fori_loop(n_plain, n_steps, lambda it, s_: step(it, s_, True), st)
        _, l = st
        o = acc_ref[...] / l
        d = o[:, :tq] - lam * o[:, tq:]
        y = d * lax.rsqrt(jnp.mean(d * d, axis=0, keepdims=True) + EPS) * subg
        o_ref[pl.ds(r0, tq), :] = jnp.transpose(y).astype(o_ref.dtype)
        return carry

    lax.fori_loop(0, nq, qblock, 0)


def _attn_a(proj, bias, lq1, lk1, lq2, lk2, subg, bsz, seq, lam_init):
    tq = min(TQ_A, seq // 2)
    tk = min(TK_A, tq)
    vec = lambda n: pl.BlockSpec((1, n), lambda b, h: (0, 0))
    col = lambda base: pl.BlockSpec((None, seq, LANES), lambda b, h: (base + h, b, 0))
    return pl.pallas_call(
        functools.partial(_attn_a_kernel, tq=tq, tk=tk, lam_init=lam_init),
        grid=(bsz, A_HEADS),
        in_specs=[vec(HEAD_DIM), vec(HEAD_DIM), vec(HEAD_DIM), vec(HEAD_DIM),
                  pl.BlockSpec((2 * HEAD_DIM, 1), lambda b, h: (0, 0)),
                  pl.BlockSpec((None, tq // tk + 3, tk, tq), lambda b, h: (h, 0, 0, 0)),
                  col(AQ0), col(AK0), col(AV0)],
        out_specs=pl.BlockSpec((None, seq, LANES), lambda b, h: (h, b, 0)),
        out_shape=jax.ShapeDtypeStruct((A_HEADS, bsz * seq, LANES), BF16),
        scratch_shapes=[pltpu.VMEM((LANES, seq), BF16), pltpu.VMEM((LANES, 2 * tq), F32)]
        + [pltpu.VMEM((tk, 2 * tq), F32) for _ in range(UNROLL_A)],
        compiler_params=_cparams("parallel", "parallel"),
        name="attn_a",
    )(lq1, lk1, lq2, lk2, subg, bias, proj, proj, proj)


def _b_bias_tiles(rel_bias, rq):
    w = rq + B_PAD
    r = np.arange(rq)[:, None]
    c = np.arange(w)[None, :]
    lo = CHUNK * (r // CHUNK)
    in_band = (c >= lo) & (c < lo + B_PAD + CHUNK)
    u = np.arange(rq + w - 1)
    idx = np.clip(rq - 1 - u + B_PAD, -B_MAX_REL, B_MAX_REL) + B_MAX_REL
    bias = _toeplitz(rel_bias[:, idx].astype(F32) * LOG2E, rq, w)
    bias = jnp.where(jnp.asarray(in_band)[None], bias, NEG_INF)
    return bias.reshape(B_HEADS // 2, 2 * rq, w)


def _attn_b_kernel(bias_ref, q_ref, k_ref, v_ref, o_ref, kp_ref, vp_ref, *, rq, unroll):
    seq = q_ref.shape[0]
    w = rq + B_PAD
    nq = seq // rq
    zeros = jnp.zeros((B_PAD, LANES), BF16)
    kp_ref[0:B_PAD, :] = zeros
    vp_ref[0:B_PAD, :] = zeros
    kp_ref[B_PAD:B_PAD + seq, :] = k_ref[...]
    vp_ref[B_PAD:B_PAD + seq, :] = v_ref[...]
    lane = lax.broadcasted_iota(jnp.int32, (1, LANES), 1)
    mask1 = (lane < HEAD_DIM).astype(BF16)
    mask2 = (lane >= HEAD_DIM).astype(BF16)
    colpos = lax.broadcasted_iota(jnp.int32, (2 * rq, w), 1)
    lane_o = lax.broadcasted_iota(jnp.int32, (rq, LANES), 1)

    def qblock(r0, at_start):
        qb = q_ref[pl.ds(r0, rq), :]
        qq = jnp.concatenate([qb * mask1, qb * mask2], axis=0)
        kb = kp_ref[pl.ds(r0, w), :]
        vb = vp_ref[pl.ds(r0, w), :]
        s = lax.dot_general(qq, kb, (((1,), (1,)), ((), ())), preferred_element_type=F32)
        s = s + bias_ref[...]
        if at_start:
            s = jnp.where(colpos + r0 >= B_PAD, s, NEG_INF)
        m = jnp.max(s, axis=-1, keepdims=True)
        p = jnp.exp2(s - m)
        den = jnp.sum(p, axis=-1, keepdims=True)
        o = jnp.dot(p.astype(BF16), vb, preferred_element_type=F32) / den
        o_ref[pl.ds(r0, rq), :] = jnp.where(lane_o < HEAD_DIM, o[:rq], o[rq:]).astype(o_ref.dtype)

    n_start = unroll * pl.cdiv(B_PAD // rq, unroll)
    for b in range(n_start):
        qblock(b * rq, b * rq < B_PAD)

    def step(it, carry):
        for u in range(unroll):
            qblock(pl.multiple_of((it * unroll + u) * rq, rq), False)
        return carry

    lax.fori_loop(n_start // unroll, nq // unroll, step, 0)


def _attn_b(proj, bias, bsz, seq):
    rq = RQ_B
    w = rq + B_PAD
    col = lambda base: pl.BlockSpec((None, seq, LANES), lambda b, h: (base + h, b, 0))
    return pl.pallas_call(
        functools.partial(_attn_b_kernel, rq=rq, unroll=UNROLL_B),
        grid=(bsz, B_HEADS // 2),
        in_specs=[pl.BlockSpec((None, 2 * rq, w), lambda b, h: (h, 0, 0)),
                  col(BQ0), col(BK0), col(BV0)],
        out_specs=pl.BlockSpec((None, seq, LANES), lambda b, h: (h, b, 0)),
        out_shape=jax.ShapeDtypeStruct((B_HEADS // 2, bsz * seq, LANES), BF16),
        scratch_shapes=[pltpu.VMEM((seq + B_PAD, LANES), BF16), pltpu.VMEM((seq + B_PAD, LANES), BF16)],
        compiler_params=_cparams("parallel", "parallel"),
        name="attn_b",
    )(bias, proj, proj, proj)


def _attn_c_kernel(q_ref, k_ref, v_ref, o_ref, *, tq, group):
    seq = q_ref.shape[0]
    nq = seq // tq
    row = lax.broadcasted_iota(jnp.int32, (tq, tq), 0)
    colm = lax.broadcasted_iota(jnp.int32, (tq, tq), 1)
    tri_strict = (row > colm).astype(BF16)
    before = colm < row

    def block(i, j, carry, acc, diag):
        qb = q_ref[pl.ds(pl.multiple_of(i * tq, tq), tq), :]
        c0 = pl.multiple_of(j * tq, tq)
        kb = k_ref[pl.ds(c0, tq), :]
        vb = v_ref[pl.ds(c0, tq), :]
        z = lax.dot_general(qb, kb, (((1,), (1,)), ((), ())), preferred_element_type=F32)
        lm = -(jnp.maximum(z, 0.0) + jnp.log2(1.0 + jnp.exp2(-jnp.abs(z))))
        if diag:
            lm = jnp.where(before, lm, 0.0)
        lm_b = lm.astype(BF16)
        excl = jnp.dot(lm_b, tri_strict, preferred_element_type=F32)
        logw = (z + lm) + excl + carry
        wgt = jnp.exp2(logw)
        if diag:
            wgt = jnp.where(before, wgt, 0.0)
        acc = acc + jnp.dot(wgt.astype(BF16), vb, preferred_element_type=F32)
        carry = carry + excl[:, 0:1] + lm_b[:, 0:1].astype(F32)
        return carry, acc

    def qgroup(gi, c):
        i0 = gi * group
        sts = [block(i0 + g, i0 + g, jnp.zeros((tq, 1), F32), jnp.zeros((tq, LANES), F32), True)
               for g in range(group)]
        carries = tuple(st[0] for st in sts)
        accs = tuple(st[1] for st in sts)

        def live(carries):
            return functools.reduce(jnp.maximum, [jnp.max(cr) for cr in carries])

        def cond(st):
            t, cmax, _, _ = st
            return jnp.logical_and(t <= i0 + group - 1, cmax > C_SKIP_LOG2)

        def body(st):
            t, _, carries, accs = st
            new = []
            for g in range(group):
                j = i0 + g - t
                cin = jnp.where(j >= 0, carries[g], NEG_INF)
                new.append(block(i0 + g, jnp.maximum(j, 0), cin, accs[g], False))
            carries = tuple(st_[0] for st_ in new)
            return t + 1, live(carries), carries, tuple(st_[1] for st_ in new)

        _, _, _, accs = lax.while_loop(cond, body, (jnp.int32(1), live(carries), carries, accs))
        for g in range(group):
            o_ref[pl.ds(pl.multiple_of((i0 + g) * tq, tq), tq), :] = accs[g].astype(o_ref.dtype)
        return c

    lax.fori_loop(0, nq // group, qgroup, 0)


def _attn_c(proj, bsz, seq):
    tq = min(TQ_C, seq)
    col = lambda base: pl.BlockSpec((None, seq, LANES), lambda b, h: (base + h, b, 0))
    return pl.pallas_call(
        functools.partial(_attn_c_kernel, tq=tq, group=GROUP_C),
        grid=(bsz, C_HEADS),
        in_specs=[col(CQ0), col(CK0), col(CV0)],
        out_specs=pl.BlockSpec((None, seq, LANES), lambda b, h: (h, b, 0)),
        out_shape=jax.ShapeDtypeStruct((C_HEADS, bsz * seq, LANES), BF16),
        compiler_params=_cparams("parallel", "parallel"),
        name="attn_c",
    )(proj, proj, proj)


def _cat_lanes(ref, start, count):
    return jnp.concatenate([ref[start + c] for c in range(count)], axis=1)


def _merge_kernel(x_ref, mod_ref, gate_ref, bg_ref, ya_ref, yb_ref, yc_ref, wb_ref, wo_ref, o_ref):
    blks = D_MODEL // LANES
    merged = None
    for r, y_ref in enumerate((ya_ref, yb_ref, yc_ref)):
        y = _cat_lanes(y_ref, 0, BRANCH_WIDTH // LANES)
        br = jnp.dot(y, wb_ref[r], preferred_element_type=F32)
        pre = _cat_lanes(gate_ref, r * blks, blks).astype(F32) + bg_ref[:, r * D_MODEL:(r + 1) * D_MODEL]
        term = jax.nn.sigmoid(pre) * br
        merged = term if merged is None else merged + term
    out = jnp.dot(merged.astype(BF16), wo_ref[...], preferred_element_type=F32)
    o_ref[...] = x_ref[...] + mod_ref[2:3, :] * out


def _merge(x2d, mod_l, proj, b_gate, ya, yb, yc, w_branch, w_out, seq):
    t, d = x2d.shape
    tm = min(TM_MERGE, seq)
    ybs = lambda: pl.BlockSpec((BRANCH_WIDTH // LANES, tm, LANES), lambda i: (0, i, 0))
    return pl.pallas_call(
        _merge_kernel,
        grid=(t // tm,),
        in_specs=[
            pl.BlockSpec((tm, d), lambda i: (i, 0)),
            pl.BlockSpec((None, 6, d), lambda i: ((i * tm) // seq, 0, 0)),
            pl.BlockSpec((GATE_BLKS, tm, LANES), lambda i: (0, i, 0)),
            pl.BlockSpec((1, GATE_COLS), lambda i: (0, 0)),
            ybs(), ybs(), ybs(),
            pl.BlockSpec((N_BRANCH, BRANCH_WIDTH, d), lambda i: (0, 0, 0)),
            pl.BlockSpec((d, d), lambda i: (0, 0)),
        ],
        out_specs=pl.BlockSpec((tm, d), lambda i: (i, 0)),
        out_shape=jax.ShapeDtypeStruct((t, d), F32),
        compiler_params=_cparams("parallel"),
        name="merge",
    )(x2d, mod_l, proj, b_gate, ya, yb, yc, w_branch, w_out)


def _ffn_kernel(x_ref, mod_ref, g_ref, fg_ref, w1_ref, w3_ref, w2_ref, o_ref, h_ref, acc_ref, *, final):
    f = pl.program_id(1)

    @pl.when(f == 0)
    def _():
        h = _norm_mod(x_ref[...], g_ref[...], mod_ref[3:4, :], mod_ref[4:5, :])
        h_ref[...] = h.astype(BF16)

    h = h_ref[...]
    u_gate = jnp.dot(h, w1_ref[...], preferred_element_type=F32)
    u_up = jnp.dot(h, w3_ref[...], preferred_element_type=F32)
    act = (u_gate * jax.nn.sigmoid(u_gate) * u_up).astype(BF16)
    part = jnp.dot(act, w2_ref[...], preferred_element_type=F32)

    @pl.when(f == 0)
    def _():
        acc_ref[...] = part

    @pl.when(f > 0)
    def _():
        acc_ref[...] += part

    @pl.when(f == pl.num_programs(1) - 1)
    def _():
        y = x_ref[...] + mod_ref[5:6, :] * acc_ref[...]
        if final:
            y = y * lax.rsqrt(jnp.mean(y * y, axis=-1, keepdims=True) + EPS) * fg_ref[...]
        o_ref[...] = y


def _ffn(x2d, mod_l, g, final_g, w13, w2, seq, final):
    t, d = x2d.shape
    dff = w2.shape[0]
    tm = min(TM_FFN, seq)
    tf = TF_FFN
    nf = dff // tf
    return pl.pallas_call(
        functools.partial(_ffn_kernel, final=final),
        grid=(t // tm, nf),
        in_specs=[
            pl.BlockSpec((tm, d), lambda i, f: (i, 0)),
            pl.BlockSpec((None, 6, d), lambda i, f: ((i * tm) // seq, 0, 0)),
            pl.BlockSpec((1, d), lambda i, f: (0, 0)),
            pl.BlockSpec((1, d), lambda i, f: (0, 0)),
            pl.BlockSpec((d, tf), lambda i, f: (0, f)),
            pl.BlockSpec((d, tf), lambda i, f: (0, f + nf)),
            pl.BlockSpec((tf, d), lambda i, f: (f, 0)),
        ],
        out_specs=pl.BlockSpec((tm, d), lambda i, f: (i, 0)),
        out_shape=jax.ShapeDtypeStruct((t, d), F32),
        scratch_shapes=[pltpu.VMEM((tm, d), BF16), pltpu.VMEM((tm, d), F32)],
        compiler_params=_cparams("parallel", "arbitrary"),
        name="ffn",
    )(x2d, mod_l, g, final_g, w13, w13, w2)


def _prep_w_in(w_in):
    scale = np.ones((IN_COLS,), np.float32)
    scale[0 * BRANCH_WIDTH:1 * BRANCH_WIDTH] = HEAD_DIM ** -0.5 * LOG2E
    scale[3 * BRANCH_WIDTH:4 * BRANCH_WIDTH] = HEAD_DIM ** -0.5 * LOG2E
    scale[6 * BRANCH_WIDTH:7 * BRANCH_WIDTH] = C_HEAD_DIM ** -0.5 * LOG2E
    w = w_in * jnp.asarray(scale)
    return jnp.concatenate([w[..., QKV_COLS:], w[..., :QKV_COLS]], axis=-1).astype(BF16)


def kernel(x, c, w_ada, b_ada, norm1_g, w_in, b_gate, lam_q1, lam_k1, lam_q2, lam_k2, subln_g,
           t5_table, rel_bias_b, w_branch, w_out, norm2_g, w13, w2, final_g):
    bsz, seq, d = x.shape
    depth = w_in.shape[0]
    t = bsz * seq

    mod = _ada_mod(c, w_ada, b_ada).reshape(depth, bsz, 6, d)
    w_in_b = _prep_w_in(w_in)
    w_branch_b = w_branch.astype(BF16)
    w_out_b = w_out.astype(BF16)
    w13_b = w13.astype(BF16)
    w2_b = w2.astype(BF16)
    tq_a = min(TQ_A, seq // 2)
    a_bias = _a_bias_tiles(t5_table, tq_a, min(TK_A, tq_a))
    fg = final_g.reshape(1, d)

    x2d = x.reshape(t, d)
    for l in range(depth):
        lam_init = 0.8 - 0.6 * math.exp(-0.3 * l)
        proj = _inproj(x2d, mod[l], norm1_g[l].reshape(1, d), w_in_b[l], seq)
        ya = _attn_a(proj, a_bias, lam_q1[l].reshape(1, -1), lam_k1[l].reshape(1, -1),
                     lam_q2[l].reshape(1, -1), lam_k2[l].reshape(1, -1), subln_g[l].reshape(-1, 1),
                     bsz, seq, lam_init)
        yb = _attn_b(proj, _b_bias_tiles(rel_bias_b[l], RQ_B), bsz, seq)
        yc = _attn_c(proj, bsz, seq)
        x2d = _merge(x2d, mod[l], proj, b_gate[l].reshape(1, -1), ya, yb, yc,
                     w_branch_b[l], w_out_b[l], seq)
        x2d = _ffn(x2d, mod[l], norm2_g[l].reshape(1, d), fg, w13_b[l], w2_b[l], seq,
                   final=(l == depth - 1))
    return x2d.reshape(bsz, seq, d)
```

```python
import functools
import math

import numpy as np
import jax
import jax.numpy as jnp
from jax import lax
from jax.experimental import pallas as pl
from jax.experimental.pallas import tpu as pltpu

F32 = jnp.float32
BF16 = jnp.bfloat16

D_MODEL = 1024
DEPTH = 4
CHUNK = 64
HEAD_DIM = 64
BRANCH_WIDTH = 512
N_BRANCH = 3
A_HEADS = 4
B_HEADS = 8
C_HEADS = 4
C_HEAD_DIM = 128
B_LEFT_CHUNKS = 8
B_PAD = B_LEFT_CHUNKS * CHUNK
B_MAX_REL = 128
T5_BUCKETS = 32
T5_MAX_DIST = 128
D_FF = 2816
QKV_COLS = 3 * N_BRANCH * BRANCH_WIDTH
GATE_COLS = N_BRANCH * D_MODEL
IN_COLS = QKV_COLS + GATE_COLS
NEG_INF = -1e30
EPS = 1e-6
LOG2E = math.log2(math.e)

LANES = 128
N_COLBLK = IN_COLS // LANES
GATE_BLKS = GATE_COLS // LANES
AQ0, AK0, AV0 = GATE_BLKS, GATE_BLKS + 4, GATE_BLKS + 8
BQ0, BK0, BV0 = GATE_BLKS + 12, GATE_BLKS + 16, GATE_BLKS + 20
CQ0, CK0, CV0 = GATE_BLKS + 24, GATE_BLKS + 28, GATE_BLKS + 32

TM_PROJ = 512
TN_PROJ = 1536
TM_MERGE = 512
TM_FFN = 512
FFN_CHUNKS = ((0, 1536), (1536, 2816))
TQ_A = 1024
TK_A = 512
TP_A = 512
UNROLL_A = 2
RQ_B = 128
UNROLL_B = 8
TQ_C = 256
GROUP_C = 4
C_SKIP_LOG2 = -150.0
VMEM_LIMIT = 56 * 1024 * 1024


def _cparams(*sem, flags=None):
    return pltpu.CompilerParams(dimension_semantics=sem, vmem_limit_bytes=VMEM_LIMIT, flags=flags)


def _ada_kernel(c_ref, w_ref, b_ref, o_ref):
    c = c_ref[...]
    cs = c * jax.nn.sigmoid(c)
    o_ref[...] = jnp.dot(cs, w_ref[...], preferred_element_type=F32,
                         precision=lax.Precision.HIGHEST) + b_ref[...]


def _ada_mod(c, w_ada, b_ada):
    depth, d, e = w_ada.shape
    bsz = c.shape[0]
    nblk = e // d
    return pl.pallas_call(
        _ada_kernel,
        grid=(depth, nblk),
        in_specs=[
            pl.BlockSpec((bsz, d), lambda l, j: (0, 0)),
            pl.BlockSpec((None, d, d), lambda l, j: (l, 0, j)),
            pl.BlockSpec((None, 1, d), lambda l, j: (l, 0, j)),
        ],
        out_specs=pl.BlockSpec((None, bsz, d), lambda l, j: (l, 0, j)),
        out_shape=jax.ShapeDtypeStruct((depth, bsz, e), F32),
        compiler_params=_cparams("arbitrary", "arbitrary"),
        name="ada_mod",
    )(c, w_ada, b_ada.reshape(depth, 1, e))


def _norm_mod(x, g, shift, scale):
    ms = jnp.mean(x * x, axis=-1, keepdims=True)
    y = x * lax.rsqrt(ms + EPS) * g
    return y * (1.0 + scale) + shift


def _inproj_kernel(x_ref, mod_ref, g_ref, w_ref, o_ref):
    h = _norm_mod(x_ref[...], g_ref[...], mod_ref[0:1, :], mod_ref[1:2, :]).astype(BF16)
    n = w_ref.shape[1]
    for c0 in range(0, n, TN_PROJ):
        res = jnp.dot(h, w_ref[:, c0:c0 + TN_PROJ], preferred_element_type=F32)
        for cb in range(TN_PROJ // LANES):
            o_ref[c0 // LANES + cb] = res[:, cb * LANES:(cb + 1) * LANES].astype(o_ref.dtype)


def _inproj(x2d, mod_l, g, w, seq):
    t, d = x2d.shape
    n = w.shape[1]
    assert n % TN_PROJ == 0
    tm = min(TM_PROJ, seq)
    return pl.pallas_call(
        _inproj_kernel,
        grid=(t // tm,),
        in_specs=[
            pl.BlockSpec((tm, d), lambda i: (i, 0)),
            pl.BlockSpec((None, 6, d), lambda i: ((i * tm) // seq, 0, 0)),
            pl.BlockSpec((1, d), lambda i: (0, 0)),
            pl.BlockSpec((d, n), lambda i: (0, 0), pipeline_mode=pl.Buffered(1)),
        ],
        out_specs=pl.BlockSpec((n // LANES, tm, LANES), lambda i: (0, i, 0)),
        out_shape=jax.ShapeDtypeStruct((n // LANES, t, LANES), BF16),
        compiler_params=_cparams("parallel"),
        name="in_proj",
    )(x2d, mod_l, g, w)


def _t5_bucket_np(rel):
    nb = T5_BUCKETS // 2
    max_exact = nb // 2
    ret = np.where(rel > 0, nb, 0)
    n = np.abs(rel)
    nf = np.maximum(n, 1).astype(np.float32)
    scaled = (np.log(nf / np.float32(max_exact)) / np.float32(math.log(T5_MAX_DIST / max_exact))
              * np.float32(nb - max_exact))
    large = max_exact + scaled.astype(np.int32)
    large = np.minimum(large, nb - 1)
    return (ret + np.where(n < max_exact, n, large)).astype(np.int32)


def _toeplitz(vec, rows, cols):
    length = rows + cols - 1
    assert vec.shape[-1] == length
    lead = vec.shape[:-1]
    ext = jnp.concatenate([vec, jnp.zeros(lead + (1,), vec.dtype)], axis=-1)
    flat = jnp.broadcast_to(ext[..., None, :], lead + (rows, length + 1)).reshape(lead + (rows * (length + 1),))
    return flat[..., :rows * length].reshape(lead + (rows, length))[..., rows - 1:rows - 1 + cols]


def _a_bias_tiles(t5_table, tq, tk):
    r = np.arange(tq)[None, :]
    c = np.arange(tk)[:, None]
    far_bucket = _t5_bucket_np((c - 2 * tk) - r)
    assert (far_bucket == far_bucket[0, 0]).all() and far_bucket[0, 0] == _t5_bucket_np(np.array(-10 * tq))
    far = t5_table[int(far_bucket[0, 0])].astype(F32)
    tiles = [jnp.zeros((A_HEADS, tk, tq), F32)]
    u = np.arange(tk + tq - 1)
    for koff in range(-tk, tq, tk):
        vec = t5_table[_t5_bucket_np(koff + tk - 1 - u)].astype(F32).T
        bias = _toeplitz((vec - far[:, None]) * LOG2E, tk, tq)
        allowed = ((c + koff) // CHUNK) <= (r // CHUNK)
        tiles.append(jnp.where(jnp.asarray(allowed)[None], bias, NEG_INF))
    tiles.append(jnp.full((A_HEADS, tk, tq), NEG_INF, F32))
    return jnp.stack(tiles, axis=1)


def _attn_a_kernel(lq1_ref, lk1_ref, lq2_ref, lk2_ref, subg_ref, bias_ref, q_ref, k_ref, v_ref,
                   o_ref, vt_ref, acc_ref, *s_refs, tq, tk, lam_init):
    seq = q_ref.shape[0]
    nq = seq // tq
    nkb = seq // tk
    per_q = tq // tk
    unroll = len(s_refs)
    tp = TP_A
    lam = (jnp.exp(jnp.sum(lq1_ref[...] * lk1_ref[...], axis=-1, keepdims=True))
           - jnp.exp(jnp.sum(lq2_ref[...] * lk2_ref[...], axis=-1, keepdims=True)) + lam_init)
    lane = lax.broadcasted_iota(jnp.int32, (1, LANES), 1)
    mask1 = (lane < HEAD_DIM).astype(BF16)
    mask2 = (lane >= HEAD_DIM).astype(BF16)
    subg = subg_ref[...] * (1.0 - lam_init)

    def transpose_v(b, carry):
        c0 = pl.multiple_of(b * tk, tk)
        vt_ref[:, pl.ds(c0, tk)] = jnp.transpose(v_ref[pl.ds(c0, tk), :].astype(F32)).astype(BF16)
        return carry

    lax.fori_loop(0, nkb, transpose_v, 0)

    def qblock(i, carry):
        r0 = pl.multiple_of(i * tq, tq)
        qb = q_ref[pl.ds(r0, tq), :]
        qq = jnp.concatenate([qb * mask1, qb * mask2], axis=0)
        n_blocks = per_q * (i + 1)
        acc_ref[...] = jnp.zeros(acc_ref.shape, F32)

        def block(js, st, with_bias, s_ref):
            m, l = st
            c0 = pl.multiple_of(jnp.minimum(js, nkb - 1) * tk, tk)
            kb = k_ref[pl.ds(c0, tk), :]
            s = lax.dot_general(kb, qq, (((1,), (1,)), ((), ())), preferred_element_type=F32)
            if with_bias:
                bias = bias_ref[jnp.clip(js - per_q * i + 2, 0, per_q + 2)]
                s = s + jnp.concatenate([bias, bias], axis=1)
            s_ref[...] = s
            m_new = jnp.maximum(m, jnp.max(s, axis=0, keepdims=True))
            alpha = jnp.exp2(m - m_new)
            sums = []
            for ct in range(2 * tq // tp):
                cols = slice(ct * tp, (ct + 1) * tp)
                m_c = m_new[:, cols]
                acc_c = alpha[:, cols] * acc_ref[:, cols]
                l_c = None
                for kt in range(tk // tp):
                    p = jnp.exp2(s_ref[kt * tp:(kt + 1) * tp, cols] - m_c)
                    psum = jnp.sum(p, axis=0, keepdims=True)
                    l_c = psum if l_c is None else l_c + psum
                    vt = vt_ref[:, pl.ds(pl.multiple_of(c0 + kt * tp, tp), tp)]
                    acc_c = acc_c + jnp.dot(vt, p.astype(BF16), preferred_element_type=F32)
                acc_ref[:, cols] = acc_c
                sums.append(l_c)
            return m_new, alpha * l + jnp.concatenate(sums, axis=1)

        def step(it, st, with_bias):
            for u in range(unroll):
                st = block(it * unroll + u, st, with_bias, s_refs[u])
            return st

        st = (jnp.full((1, 2 * tq), NEG_INF, F32), jnp.zeros((1, 2 * tq), F32))
        n_plain = jnp.maximum(per_q * i - 1, 0) // unroll
        n_steps = (n_blocks + unroll - 1) // unroll
        st = lax.fori_loop(0, n_plain, lambda it, s_: step(it, s_, False), st)
        st = lax.fori_loop(n_plain, n_steps, lambda it, s_: step(it, s_, True), st)
        _, l = st
        o = acc_ref[...] / l
        d = o[:, :tq] - lam * o[:, tq:]
        y = d * lax.rsqrt(jnp.mean(d * d, axis=0, keepdims=True) + EPS) * subg
        o_ref[pl.ds(r0, tq), :] = jnp.transpose(y).astype(o_ref.dtype)
        return carry

    lax.fori_loop(0, nq, qblock, 0)


def _attn_a(proj, bias, lq1, lk1, lq2, lk2, subg, bsz, seq, lam_init):
    tq = min(TQ_A, seq // 2)
    tk = min(TK_A, tq)
    vec = lambda n: pl.BlockSpec((1, n), lambda b, h: (0, 0))
    col = lambda base: pl.BlockSpec((None, seq, LANES), lambda b, h: (base + h, b, 0))
    return pl.pallas_call(
        functools.partial(_attn_a_kernel, tq=tq, tk=tk, lam_init=lam_init),
        grid=(bsz, A_HEADS),
        in_specs=[vec(HEAD_DIM), vec(HEAD_DIM), vec(HEAD_DIM), vec(HEAD_DIM),
                  pl.BlockSpec((2 * HEAD_DIM, 1), lambda b, h: (0, 0)),
                  pl.BlockSpec((None, tq // tk + 3, tk, tq), lambda b, h: (h, 0, 0, 0)),
                  col(AQ0), col(AK0), col(AV0)],
        out_specs=pl.BlockSpec((None, seq, LANES), lambda b, h: (h, b, 0)),
        out_shape=jax.ShapeDtypeStruct((A_HEADS, bsz * seq, LANES), BF16),
        scratch_shapes=[pltpu.VMEM((LANES, seq), BF16), pltpu.VMEM((LANES, 2 * tq), F32)]
        + [pltpu.VMEM((tk, 2 * tq), F32) for _ in range(UNROLL_A)],
        compiler_params=_cparams("parallel", "parallel"),
        name="attn_a",
    )(lq1, lk1, lq2, lk2, subg, bias, proj, proj, proj)


def _b_bias_tiles(rel_bias, rq):
    w = rq + B_PAD
    r = np.arange(rq)[:, None]
    c = np.arange(w)[None, :]
    lo = CHUNK * (r // CHUNK)
    in_band = (c >= lo) & (c < lo + B_PAD + CHUNK)
    u = np.arange(rq + w - 1)
    idx = np.clip(rq - 1 - u + B_PAD, -B_MAX_REL, B_MAX_REL) + B_MAX_REL
    bias = _toeplitz(rel_bias[:, idx].astype(F32) * LOG2E, rq, w)
    bias = jnp.where(jnp.asarray(in_band)[None], bias, NEG_INF)
    return bias.reshape(B_HEADS // 2, 2 * rq, w)


def _attn_b_kernel(bias_ref, q_ref, k_ref, v_ref, o_ref, kp_ref, vp_ref, *, rq, unroll):
    seq = q_ref.shape[0]
    w = rq + B_PAD
    nq = seq // rq
    zeros = jnp.zeros((B_PAD, LANES), BF16)
    kp_ref[0:B_PAD, :] = zeros
    vp_ref[0:B_PAD, :] = zeros
    kp_ref[B_PAD:B_PAD + seq, :] = k_ref[...]
    vp_ref[B_PAD:B_PAD + seq, :] = v_ref[...]
    lane = lax.broadcasted_iota(jnp.int32, (1, LANES), 1)
    mask1 = (lane < HEAD_DIM).astype(BF16)
    mask2 = (lane >= HEAD_DIM).astype(BF16)
    colpos = lax.broadcasted_iota(jnp.int32, (2 * rq, w), 1)
    lane_o = lax.broadcasted_iota(jnp.int32, (rq, LANES), 1)

    def qblock(r0, at_start):
        qb = q_ref[pl.ds(r0, rq), :]
        qq = jnp.concatenate([qb * mask1, qb * mask2], axis=0)
        kb = kp_ref[pl.ds(r0, w), :]
        vb = vp_ref[pl.ds(r0, w), :]
        s = lax.dot_general(qq, kb, (((1,), (1,)), ((), ())), preferred_element_type=F32)
        s = s + bias_ref[...]
        if at_start:
            s = jnp.where(colpos + r0 >= B_PAD, s, NEG_INF)
        m = jnp.max(s, axis=-1, keepdims=True)
        p = jnp.exp2(s - m)
        den = jnp.sum(p, axis=-1, keepdims=True)
        o = jnp.dot(p.astype(BF16), vb, preferred_element_type=F32) / den
        o_ref[pl.ds(r0, rq), :] = jnp.where(lane_o < HEAD_DIM, o[:rq], o[rq:]).astype(o_ref.dtype)

    n_start = unroll * pl.cdiv(B_PAD // rq, unroll)
    for b in range(n_start):
        qblock(b * rq, b * rq < B_PAD)

    def step(it, carry):
        for u in range(unroll):
            qblock(pl.multiple_of((it * unroll + u) * rq, rq), False)
        return carry

    lax.fori_loop(n_start // unroll, nq // unroll, step, 0)


def _attn_b(proj, bias, bsz, seq):
    rq = RQ_B
    w = rq + B_PAD
    col = lambda base: pl.BlockSpec((None, seq, LANES), lambda b, h: (base + h, b, 0))
    return pl.pallas_call(
        functools.partial(_attn_b_kernel, rq=rq, unroll=UNROLL_B),
        grid=(bsz, B_HEADS // 2),
        in_specs=[pl.BlockSpec((None, 2 * rq, w), lambda b, h: (h, 0, 0)),
                  col(BQ0), col(BK0), col(BV0)],
        out_specs=pl.BlockSpec((None, seq, LANES), lambda b, h: (h, b, 0)),
        out_shape=jax.ShapeDtypeStruct((B_HEADS // 2, bsz * seq, LANES), BF16),
        scratch_shapes=[pltpu.VMEM((seq + B_PAD, LANES), BF16), pltpu.VMEM((seq + B_PAD, LANES), BF16)],
        compiler_params=_cparams("parallel", "parallel"),
        name="attn_b",
    )(bias, proj, proj, proj)


def _attn_c_kernel(q_ref, k_ref, v_ref, o_ref, *, tq, group):
    seq = q_ref.shape[0]
    nq = seq // tq
    row = lax.broadcasted_iota(jnp.int32, (tq, tq), 0)
    colm = lax.broadcasted_iota(jnp.int32, (tq, tq), 1)
    tri_strict = (row > colm).astype(BF16)
    before = colm < row

    def block(i, j, carry, acc, diag):
        qb = q_ref[pl.ds(pl.multiple_of(i * tq, tq), tq), :]
        c0 = pl.multiple_of(j * tq, tq)
        kb = k_ref[pl.ds(c0, tq), :]
        vb = v_ref[pl.ds(c0, tq), :]
        z = lax.dot_general(qb, kb, (((1,), (1,)), ((), ())), preferred_element_type=F32)
        lm = -(jnp.maximum(z, 0.0) + jnp.log2(1.0 + jnp.exp2(-jnp.abs(z))))
        if diag:
            lm = jnp.where(before, lm, 0.0)
        lm_b = lm.astype(BF16)
        excl = jnp.dot(lm_b, tri_strict, preferred_element_type=F32)
        logw = (z + lm) + excl + carry
        wgt = jnp.exp2(logw)
        if diag:
            wgt = jnp.where(before, wgt, 0.0)
        acc = acc + jnp.dot(wgt.astype(BF16), vb, preferred_element_type=F32)
        carry = carry + excl[:, 0:1] + lm_b[:, 0:1].astype(F32)
        return carry, acc

    def qgroup(gi, c):
        i0 = gi * group
        sts = [block(i0 + g, i0 + g, jnp.zeros((tq, 1), F32), jnp.zeros((tq, LANES), F32), True)
               for g in range(group)]
        carries = tuple(st[0] for st in sts)
        accs = tuple(st[1] for st in sts)

        def live(carries):
            return functools.reduce(jnp.maximum, [jnp.max(cr) for cr in carries])

        def cond(st):
            t, cmax, _, _ = st
            return jnp.logical_and(t <= i0 + group - 1, cmax > C_SKIP_LOG2)

        def body(st):
            t, _, carries, accs = st
            new = []
            for g in range(group):
                j = i0 + g - t
                cin = jnp.where(j >= 0, carries[g], NEG_INF)
                new.append(block(i0 + g, jnp.maximum(j, 0), cin, accs[g], False))
            carries = tuple(st_[0] for st_ in new)
            return t + 1, live(carries), carries, tuple(st_[1] for st_ in new)

        _, _, _, accs = lax.while_loop(cond, body, (jnp.int32(1), live(carries), carries, accs))
        for g in range(group):
            o_ref[pl.ds(pl.multiple_of((i0 + g) * tq, tq), tq), :] = accs[g].astype(o_ref.dtype)
        return c

    lax.fori_loop(0, nq // group, qgroup, 0)


def _attn_c(proj, bsz, seq):
    tq = min(TQ_C, seq)
    col = lambda base: pl.BlockSpec((None, seq, LANES), lambda b, h: (base + h, b, 0))
    return pl.pallas_call(
        functools.partial(_attn_c_kernel, tq=tq, group=GROUP_C),
        grid=(bsz, C_HEADS),
        in_specs=[col(CQ0), col(CK0), col(CV0)],
        out_specs=pl.BlockSpec((None, seq, LANES), lambda b, h: (h, b, 0)),
        out_shape=jax.ShapeDtypeStruct((C_HEADS, bsz * seq, LANES), BF16),
        compiler_params=_cparams("parallel", "parallel"),
        name="attn_c",
    )(proj, proj, proj)


def _cat_lanes(ref, start, count):
    return jnp.concatenate([ref[start + c] for c in range(count)], axis=1)


def _merge_kernel(x_ref, mod_ref, gate_ref, bg_ref, ya_ref, yb_ref, yc_ref, wb_ref, wo_ref, o_ref):
    blks = D_MODEL // LANES
    merged = None
    for r, y_ref in enumerate((ya_ref, yb_ref, yc_ref)):
        y = _cat_lanes(y_ref, 0, BRANCH_WIDTH // LANES)
        br = jnp.dot(y, wb_ref[r], preferred_element_type=F32)
        pre = _cat_lanes(gate_ref, r * blks, blks).astype(F32) + bg_ref[:, r * D_MODEL:(r + 1) * D_MODEL]
        term = jax.nn.sigmoid(pre) * br
        merged = term if merged is None else merged + term
    out = jnp.dot(merged.astype(BF16), wo_ref[...], preferred_element_type=F32)
    o_ref[...] = x_ref[...] + mod_ref[2:3, :] * out


def _merge(x2d, mod_l, proj, b_gate, ya, yb, yc, w_branch, w_out, seq):
    t, d = x2d.shape
    tm = min(TM_MERGE, seq)
    ybs = lambda: pl.BlockSpec((BRANCH_WIDTH // LANES, tm, LANES), lambda i: (0, i, 0))
    return pl.pallas_call(
        _merge_kernel,
        grid=(t // tm,),
        in_specs=[
            pl.BlockSpec((tm, d), lambda i: (i, 0)),
            pl.BlockSpec((None, 6, d), lambda i: ((i * tm) // seq, 0, 0)),
            pl.BlockSpec((GATE_BLKS, tm, LANES), lambda i: (0, i, 0)),
            pl.BlockSpec((1, GATE_COLS), lambda i: (0, 0)),
            ybs(), ybs(), ybs(),
            pl.BlockSpec((N_BRANCH, BRANCH_WIDTH, d), lambda i: (0, 0, 0)),
            pl.BlockSpec((d, d), lambda i: (0, 0)),
        ],
        out_specs=pl.BlockSpec((tm, d), lambda i: (i, 0)),
        out_shape=jax.ShapeDtypeStruct((t, d), F32),
        compiler_params=_cparams("parallel"),
        name="merge",
    )(x2d, mod_l, proj, b_gate, ya, yb, yc, w_branch, w_out)


def _ffn_kernel(x_ref, mod_ref, g_ref, fg_ref, w13_ref, w2_ref, o_ref, *, final):
    dff = w2_ref.shape[0]
    x = x_ref[...]
    h = _norm_mod(x, g_ref[...], mod_ref[3:4, :], mod_ref[4:5, :]).astype(BF16)
    acc = None
    for f0, f1 in FFN_CHUNKS:
        u_gate = jnp.dot(h, w13_ref[:, f0:f1], preferred_element_type=F32)
        u_up = jnp.dot(h, w13_ref[:, dff + f0:dff + f1], preferred_element_type=F32)
        act = (u_gate * jax.nn.sigmoid(u_gate) * u_up).astype(BF16)
        part = jnp.dot(act, w2_ref[f0:f1, :], preferred_element_type=F32)
        acc = part if acc is None else acc + part
    y = x + mod_ref[5:6, :] * acc
    if final:
        y = y * lax.rsqrt(jnp.mean(y * y, axis=-1, keepdims=True) + EPS) * fg_ref[...]
    o_ref[...] = y


def _ffn(x2d, mod_l, g, final_g, w13, w2, seq, final):
    t, d = x2d.shape
    dff = w2.shape[0]
    assert FFN_CHUNKS[0][0] == 0 and FFN_CHUNKS[-1][1] == dff
    tm = min(TM_FFN, seq)
    resident = lambda shape: pl.BlockSpec(shape, lambda i: (0, 0), pipeline_mode=pl.Buffered(1))
    return pl.pallas_call(
        functools.partial(_ffn_kernel, final=final),
        grid=(t // tm,),
        in_specs=[
            pl.BlockSpec((tm, d), lambda i: (i, 0)),
            pl.BlockSpec((None, 6, d), lambda i: ((i * tm) // seq, 0, 0)),
            pl.BlockSpec((1, d), lambda i: (0, 0)),
            pl.BlockSpec((1, d), lambda i: (0, 0)),
            resident((d, 2 * dff)),
            resident((dff, d)),
        ],
        out_specs=pl.BlockSpec((tm, d), lambda i: (i, 0)),
        out_shape=jax.ShapeDtypeStruct((t, d), F32),
        compiler_params=_cparams("parallel"),
        name="ffn",
    )(x2d, mod_l, g, final_g, w13, w2)


def _prep_w_in(w_in):
    scale = np.ones((IN_COLS,), np.float32)
    scale[0 * BRANCH_WIDTH:1 * BRANCH_WIDTH] = HEAD_DIM ** -0.5 * LOG2E
    scale[3 * BRANCH_WIDTH:4 * BRANCH_WIDTH] = HEAD_DIM ** -0.5 * LOG2E
    scale[6 * BRANCH_WIDTH:7 * BRANCH_WIDTH] = C_HEAD_DIM ** -0.5 * LOG2E
    w = w_in * jnp.asarray(scale)
    return jnp.concatenate([w[..., QKV_COLS:], w[..., :QKV_COLS]], axis=-1).astype(BF16)


def kernel(x, c, w_ada, b_ada, norm1_g, w_in, b_gate, lam_q1, lam_k1, lam_q2, lam_k2, subln_g,
           t5_table, rel_bias_b, w_branch, w_out, norm2_g, w13, w2, final_g):
    bsz, seq, d = x.shape
    depth = w_in.shape[0]
    t = bsz * seq

    mod = _ada_mod(c, w_ada, b_ada).reshape(depth, bsz, 6, d)
    w_in_b = _prep_w_in(w_in)
    w_branch_b = w_branch.astype(BF16)
    w_out_b = w_out.astype(BF16)
    w13_b = w13.astype(BF16)
    w2_b = w2.astype(BF16)
    tq_a = min(TQ_A, seq // 2)
    a_bias = _a_bias_tiles(t5_table, tq_a, min(TK_A, tq_a))
    fg = final_g.reshape(1, d)

    x2d = x.reshape(t, d)
    for l in range(depth):
        lam_init = 0.8 - 0.6 * math.exp(-0.3 * l)
        proj = _inproj(x2d, mod[l], norm1_g[l].reshape(1, d), w_in_b[l], seq)
        ya = _attn_a(proj, a_bias, lam_q1[l].reshape(1, -1), lam_k1[l].reshape(1, -1),
                     lam_q2[l].reshape(1, -1), lam_k2[l].reshape(1, -1), subln_g[l].reshape(-1, 1),
                     bsz, seq, lam_init)
        yb = _attn_b(proj, _b_bias_tiles(rel_bias_b[l], RQ_B), bsz, seq)
        yc = _attn_c(proj, bsz, seq)
        x2d = _merge(x2d, mod[l], proj, b_gate[l].reshape(1, -1), ya, yb, yc,
                     w_branch_b[l], w_out_b[l], seq)
        x2d = _ffn(x2d, mod[l], norm2_g[l].reshape(1, d), fg, w13_b[l], w2_b[l], seq,
                   final=(l == depth - 1))
    return x2d.reshape(bsz, seq, d)
```

```python
import functools
import math

import numpy as np
import jax
import jax.numpy as jnp
from jax import lax
from jax.experimental import pallas as pl
from jax.experimental.pallas import tpu as pltpu

F32 = jnp.float32
BF16 = jnp.bfloat16

D_MODEL = 1024
DEPTH = 4
CHUNK = 64
HEAD_DIM = 64
BRANCH_WIDTH = 512
N_BRANCH = 3
A_HEADS = 4
B_HEADS = 8
C_HEADS = 4
C_HEAD_DIM = 128
B_LEFT_CHUNKS = 8
B_PAD = B_LEFT_CHUNKS * CHUNK
B_MAX_REL = 128
T5_BUCKETS = 32
T5_MAX_DIST = 128
D_FF = 2816
QKV_COLS = 3 * N_BRANCH * BRANCH_WIDTH
GATE_COLS = N_BRANCH * D_MODEL
IN_COLS = QKV_COLS + GATE_COLS
NEG_INF = -1e30
EPS = 1e-6
LOG2E = math.log2(math.e)

LANES = 128
N_COLBLK = IN_COLS // LANES
GATE_BLKS = GATE_COLS // LANES
AQ0, AK0, AV0 = GATE_BLKS, GATE_BLKS + 4, GATE_BLKS + 8
BQ0, BK0, BV0 = GATE_BLKS + 12, GATE_BLKS + 16, GATE_BLKS + 20
CQ0, CK0, CV0 = GATE_BLKS + 24, GATE_BLKS + 28, GATE_BLKS + 32

TM_PROJ = 512
TN_PROJ = 1536
TM_MERGE = 512
TM_FFN = 512
FFN_CHUNKS = ((0, 1536), (1536, 2816))
TQ_A = 1024
TK_A = 512
TP_A = 512
TC_A = 2048
UNROLL_A = 2
NORM_SLACK_A = 1.02
L_MIN_A = 2.0 ** -100
RQ_B = 128
UNROLL_B = 8
TQ_C = 256
GROUP_C = 8
C_SKIP_LOG2 = -150.0
VMEM_LIMIT = 56 * 1024 * 1024


def _cparams(*sem, flags=None):
    return pltpu.CompilerParams(dimension_semantics=sem, vmem_limit_bytes=VMEM_LIMIT, flags=flags)


def _ada_kernel(c_ref, w_ref, b_ref, o_ref):
    c = c_ref[...]
    cs = c * jax.nn.sigmoid(c)
    o_ref[...] = jnp.dot(cs, w_ref[...], preferred_element_type=F32,
                         precision=lax.Precision.HIGHEST) + b_ref[...]


def _ada_mod(c, w_ada, b_ada):
    depth, d, e = w_ada.shape
    bsz = c.shape[0]
    nblk = e // d
    return pl.pallas_call(
        _ada_kernel,
        grid=(depth, nblk),
        in_specs=[
            pl.BlockSpec((bsz, d), lambda l, j: (0, 0)),
            pl.BlockSpec((None, d, d), lambda l, j: (l, 0, j)),
            pl.BlockSpec((None, 1, d), lambda l, j: (l, 0, j)),
        ],
        out_specs=pl.BlockSpec((None, bsz, d), lambda l, j: (l, 0, j)),
        out_shape=jax.ShapeDtypeStruct((depth, bsz, e), F32),
        compiler_params=_cparams("arbitrary", "arbitrary"),
        name="ada_mod",
    )(c, w_ada, b_ada.reshape(depth, 1, e))


def _norm_mod(x, g, shift, scale):
    ms = jnp.mean(x * x, axis=-1, keepdims=True)
    y = x * lax.rsqrt(ms + EPS) * g
    return y * (1.0 + scale) + shift


def _inproj_kernel(x_ref, mod_ref, g_ref, w_ref, o_ref):
    h = _norm_mod(x_ref[...], g_ref[...], mod_ref[0:1, :], mod_ref[1:2, :]).astype(BF16)
    n = w_ref.shape[1]
    for c0 in range(0, n, TN_PROJ):
        res = jnp.dot(h, w_ref[:, c0:c0 + TN_PROJ], preferred_element_type=F32)
        for cb in range(TN_PROJ // LANES):
            o_ref[c0 // LANES + cb] = res[:, cb * LANES:(cb + 1) * LANES].astype(o_ref.dtype)


def _inproj(x2d, mod_l, g, w, seq):
    t, d = x2d.shape
    n = w.shape[1]
    assert n % TN_PROJ == 0
    tm = min(TM_PROJ, seq)
    return pl.pallas_call(
        _inproj_kernel,
        grid=(t // tm,),
        in_specs=[
            pl.BlockSpec((tm, d), lambda i: (i, 0)),
            pl.BlockSpec((None, 6, d), lambda i: ((i * tm) // seq, 0, 0)),
            pl.BlockSpec((1, d), lambda i: (0, 0)),
            pl.BlockSpec((d, n), lambda i: (0, 0), pipeline_mode=pl.Buffered(1)),
        ],
        out_specs=pl.BlockSpec((n // LANES, tm, LANES), lambda i: (0, i, 0)),
        out_shape=jax.ShapeDtypeStruct((n // LANES, t, LANES), BF16),
        compiler_params=_cparams("parallel"),
        name="in_proj",
    )(x2d, mod_l, g, w)


def _t5_bucket_np(rel):
    nb = T5_BUCKETS // 2
    max_exact = nb // 2
    ret = np.where(rel > 0, nb, 0)
    n = np.abs(rel)
    nf = np.maximum(n, 1).astype(np.float32)
    scaled = (np.log(nf / np.float32(max_exact)) / np.float32(math.log(T5_MAX_DIST / max_exact))
              * np.float32(nb - max_exact))
    large = max_exact + scaled.astype(np.int32)
    large = np.minimum(large, nb - 1)
    return (ret + np.where(n < max_exact, n, large)).astype(np.int32)


def _toeplitz(vec, rows, cols):
    length = rows + cols - 1
    assert vec.shape[-1] == length
    lead = vec.shape[:-1]
    ext = jnp.concatenate([vec, jnp.zeros(lead + (1,), vec.dtype)], axis=-1)
    flat = jnp.broadcast_to(ext[..., None, :], lead + (rows, length + 1)).reshape(lead + (rows * (length + 1),))
    return flat[..., :rows * length].reshape(lead + (rows, length))[..., rows - 1:rows - 1 + cols]


def _a_bias_tiles(t5_table, tq, tk):
    r = np.arange(tq)[None, :]
    c = np.arange(tk)[:, None]
    far_bucket = _t5_bucket_np((c - 2 * tk) - r)
    assert (far_bucket == far_bucket[0, 0]).all() and far_bucket[0, 0] == _t5_bucket_np(np.array(-10 * tq))
    far = t5_table[int(far_bucket[0, 0])].astype(F32)
    tiles = [jnp.zeros((A_HEADS, tk, tq), F32)]
    bmax = jnp.zeros((A_HEADS,), F32)
    u = np.arange(tk + tq - 1)
    for koff in range(-tk, tq, tk):
        vec = t5_table[_t5_bucket_np(koff + tk - 1 - u)].astype(F32).T
        vec = (vec - far[:, None]) * LOG2E
        bmax = jnp.maximum(bmax, jnp.max(vec, axis=1))
        bias = _toeplitz(vec, tk, tq)
        allowed = ((c + koff) // CHUNK) <= (r // CHUNK)
        tiles.append(jnp.where(jnp.asarray(allowed)[None], bias, NEG_INF))
    tiles.append(jnp.full((A_HEADS, tk, tq), NEG_INF, F32))
    return jnp.stack(tiles, axis=1), jnp.broadcast_to(bmax[:, None, None], (A_HEADS, 1, LANES))


def _attn_a_kernel(lq1_ref, lk1_ref, lq2_ref, lk2_ref, subg_ref, bmax_ref, bias_ref, q_ref, k_ref, v_ref,
                   o_ref, vt_ref, acc_ref, *s_refs, tq, tk, lam_init):
    seq = q_ref.shape[0]
    nq = seq // tq
    nkb = seq // tk
    per_q = tq // tk
    unroll = len(s_refs)
    tp = TP_A
    tc = TC_A
    lam = (jnp.exp(jnp.sum(lq1_ref[...] * lk1_ref[...], axis=-1, keepdims=True))
           - jnp.exp(jnp.sum(lq2_ref[...] * lk2_ref[...], axis=-1, keepdims=True)) + lam_init)
    lane = lax.broadcasted_iota(jnp.int32, (1, LANES), 1)
    mask1 = (lane < HEAD_DIM).astype(BF16)
    mask2 = (lane >= HEAD_DIM).astype(BF16)
    subg = subg_ref[...] * (1.0 - lam_init)
    sel = (lax.broadcasted_iota(jnp.int32, (8, LANES), 0)
           == lax.broadcasted_iota(jnp.int32, (8, LANES), 1) // HEAD_DIM).astype(BF16)

    def prepare(b, kn2):
        c0 = pl.multiple_of(b * tk, tk)
        vt_ref[:, pl.ds(c0, tk)] = jnp.transpose(v_ref[pl.ds(c0, tk), :].astype(F32)).astype(BF16)
        kb = k_ref[pl.ds(c0, tk), :]
        blk = lax.dot_general(sel, kb * kb, (((1,), (1,)), ((), ())), preferred_element_type=F32)
        return jnp.maximum(kn2, blk)

    kn2 = lax.fori_loop(0, nkb, prepare, jnp.zeros((8, tk), F32))
    kmax2 = jnp.max(kn2, axis=1, keepdims=True) * NORM_SLACK_A
    kmax = jnp.sqrt(jnp.concatenate([jnp.broadcast_to(kmax2[0:1], (1, tq)),
                                     jnp.broadcast_to(kmax2[1:2], (1, tq))], axis=1))

    def finish(i, acc, l):
        o = acc / l
        d = o[:, :tq] - lam * o[:, tq:]
        y = d * lax.rsqrt(jnp.mean(d * d, axis=0, keepdims=True) + EPS) * subg
        o_ref[pl.ds(pl.multiple_of(i * tq, tq), tq), :] = jnp.transpose(y).astype(o_ref.dtype)

    def load_q(i):
        qb = q_ref[pl.ds(pl.multiple_of(i * tq, tq), tq), :]
        return jnp.concatenate([qb * mask1, qb * mask2], axis=0)

    def steps(i, step, st):
        n_plain = jnp.maximum(per_q * i - 1, 0) // unroll
        n_steps = (per_q * (i + 1) + unroll - 1) // unroll
        st = lax.fori_loop(0, n_plain, lambda it, s_: step(it, s_, False), st)
        return lax.fori_loop(n_plain, n_steps, lambda it, s_: step(it, s_, True), st)

    def bounded_pass(i):
        qq = load_q(i)
        qn2 = lax.dot_general(jnp.ones((8, LANES), BF16), qq * qq, (((1,), (1,)), ((), ())),
                              preferred_element_type=F32)[0:1]
        shift = jnp.sqrt(qn2 * NORM_SLACK_A) * kmax + (bmax_ref[:, 0:1] + 1.0)
        acc_ref[...] = jnp.zeros(acc_ref.shape, F32)

        def block(js, l, with_bias):
            c0 = pl.multiple_of(jnp.minimum(js, nkb - 1) * tk, tk)
            kb = k_ref[pl.ds(c0, tk), :]
            vt = vt_ref[:, pl.ds(c0, tk)]
            if with_bias:
                bias = bias_ref[jnp.clip(js - per_q * i + 2, 0, per_q + 2)]
            sums = []
            for ct in range(2 * tq // tc):
                cols = slice(ct * tc, (ct + 1) * tc)
                s = lax.dot_general(kb, qq[cols], (((1,), (1,)), ((), ())), preferred_element_type=F32)
                if with_bias:
                    s = s + jnp.concatenate([bias, bias], axis=1)[:, cols]
                p = jnp.exp2(s - shift[:, cols])
                sums.append(jnp.sum(p, axis=0, keepdims=True))
                acc_ref[:, cols] += jnp.dot(vt, p.astype(BF16), preferred_element_type=F32)
            return l + jnp.concatenate(sums, axis=1)

        def step(it, l, with_bias):
            for u in range(unroll):
                l = block(it * unroll + u, l, with_bias)
            return l

        l = steps(i, step, jnp.zeros((1, 2 * tq), F32))
        finish(i, acc_ref[...], l)
        return jnp.min(l)

    def running_max_pass(i):
        qq = load_q(i)
        acc_ref[...] = jnp.zeros(acc_ref.shape, F32)

        def block(js, st, with_bias, s_ref):
            m, l = st
            c0 = pl.multiple_of(jnp.minimum(js, nkb - 1) * tk, tk)
            kb = k_ref[pl.ds(c0, tk), :]
            s = lax.dot_general(kb, qq, (((1,), (1,)), ((), ())), preferred_element_type=F32)
            if with_bias:
                bias = bias_ref[jnp.clip(js - per_q * i + 2, 0, per_q + 2)]
                s = s + jnp.concatenate([bias, bias], axis=1)
            s_ref[...] = s
            m_new = jnp.maximum(m, jnp.max(s, axis=0, keepdims=True))
            alpha = jnp.exp2(m - m_new)
            sums = []
            for ct in range(2 * tq // tp):
                cols = slice(ct * tp, (ct + 1) * tp)
                p = jnp.exp2(s_ref[:, cols] - m_new[:, cols])
                sums.append(jnp.sum(p, axis=0, keepdims=True))
                vt = vt_ref[:, pl.ds(c0, tk)]
                acc_ref[:, cols] = (alpha[:, cols] * acc_ref[:, cols]
                                    + jnp.dot(vt, p.astype(BF16), preferred_element_type=F32))
            return m_new, alpha * l + jnp.concatenate(sums, axis=1)

        def step(it, st, with_bias):
            for u in range(unroll):
                st = block(it * unroll + u, st, with_bias, s_refs[u])
            return st

        _, l = steps(i, step, (jnp.full((1, 2 * tq), NEG_INF, F32), jnp.zeros((1, 2 * tq), F32)))
        finish(i, acc_ref[...], l)

    def qtile(i, carry):
        l_min = bounded_pass(i)

        @pl.when(jnp.logical_not(l_min > L_MIN_A))
        def _():
            running_max_pass(i)

        return carry

    lax.fori_loop(0, nq, qtile, 0)


def _attn_a(proj, bias, bmax, lq1, lk1, lq2, lk2, subg, bsz, seq, lam_init):
    tq = min(TQ_A, seq // 2)
    tk = min(TK_A, tq)
    assert tk == TP_A
    vec = lambda n: pl.BlockSpec((1, n), lambda b, h: (0, 0))
    col = lambda base: pl.BlockSpec((None, seq, LANES), lambda b, h: (base + h, b, 0))
    return pl.pallas_call(
        functools.partial(_attn_a_kernel, tq=tq, tk=tk, lam_init=lam_init),
        grid=(bsz, A_HEADS),
        in_specs=[vec(HEAD_DIM), vec(HEAD_DIM), vec(HEAD_DIM), vec(HEAD_DIM),
                  pl.BlockSpec((2 * HEAD_DIM, 1), lambda b, h: (0, 0)),
                  pl.BlockSpec((None, 1, LANES), lambda b, h: (h, 0, 0)),
                  pl.BlockSpec((None, tq // tk + 3, tk, tq), lambda b, h: (h, 0, 0, 0)),
                  col(AQ0), col(AK0), col(AV0)],
        out_specs=pl.BlockSpec((None, seq, LANES), lambda b, h: (h, b, 0)),
        out_shape=jax.ShapeDtypeStruct((A_HEADS, bsz * seq, LANES), BF16),
        scratch_shapes=[pltpu.VMEM((LANES, seq), BF16), pltpu.VMEM((LANES, 2 * tq), F32)]
        + [pltpu.VMEM((tk, 2 * tq), F32) for _ in range(UNROLL_A)],
        compiler_params=_cparams("parallel", "parallel"),
        name="attn_a",
    )(lq1, lk1, lq2, lk2, subg, bmax, bias, proj, proj, proj)


def _b_bias_tiles(rel_bias, rq):
    w = rq + B_PAD
    r = np.arange(rq)[:, None]
    c = np.arange(w)[None, :]
    lo = CHUNK * (r // CHUNK)
    in_band = (c >= lo) & (c < lo + B_PAD + CHUNK)
    u = np.arange(rq + w - 1)
    idx = np.clip(rq - 1 - u + B_PAD, -B_MAX_REL, B_MAX_REL) + B_MAX_REL
    bias = _toeplitz(rel_bias[:, idx].astype(F32) * LOG2E, rq, w)
    bias = jnp.where(jnp.asarray(in_band)[None], bias, NEG_INF)
    return bias.reshape(B_HEADS // 2, 2 * rq, w)


def _attn_b_kernel(bias_ref, q_ref, k_ref, v_ref, o_ref, kp_ref, vp_ref, *, rq, unroll):
    seq = q_ref.shape[0]
    w = rq + B_PAD
    nq = seq // rq
    zeros = jnp.zeros((B_PAD, LANES), BF16)
    kp_ref[0:B_PAD, :] = zeros
    vp_ref[0:B_PAD, :] = zeros
    kp_ref[B_PAD:B_PAD + seq, :] = k_ref[...]
    vp_ref[B_PAD:B_PAD + seq, :] = v_ref[...]
    lane = lax.broadcasted_iota(jnp.int32, (1, LANES), 1)
    mask1 = (lane < HEAD_DIM).astype(BF16)
    mask2 = (lane >= HEAD_DIM).astype(BF16)
    colpos = lax.broadcasted_iota(jnp.int32, (2 * rq, w), 1)
    lane_o = lax.broadcasted_iota(jnp.int32, (rq, LANES), 1)

    def qblock(r0, at_start):
        qb = q_ref[pl.ds(r0, rq), :]
        qq = jnp.concatenate([qb * mask1, qb * mask2], axis=0)
        kb = kp_ref[pl.ds(r0, w), :]
        vb = vp_ref[pl.ds(r0, w), :]
        s = lax.dot_general(qq, kb, (((1,), (1,)), ((), ())), preferred_element_type=F32)
        s = s + bias_ref[...]
        if at_start:
            s = jnp.where(colpos + r0 >= B_PAD, s, NEG_INF)
        m = jnp.max(s, axis=-1, keepdims=True)
        p = jnp.exp2(s - m)
        den = jnp.sum(p, axis=-1, keepdims=True)
        o = jnp.dot(p.astype(BF16), vb, preferred_element_type=F32) / den
        o_ref[pl.ds(r0, rq), :] = jnp.where(lane_o < HEAD_DIM, o[:rq], o[rq:]).astype(o_ref.dtype)

    n_start = unroll * pl.cdiv(B_PAD // rq, unroll)
    for b in range(n_start):
        qblock(b * rq, b * rq < B_PAD)

    def step(it, carry):
        for u in range(unroll):
            qblock(pl.multiple_of((it * unroll + u) * rq, rq), False)
        return carry

    lax.fori_loop(n_start // unroll, nq // unroll, step, 0)


def _attn_b(proj, bias, bsz, seq):
    rq = RQ_B
    w = rq + B_PAD
    col = lambda base: pl.BlockSpec((None, seq, LANES), lambda b, h: (base + h, b, 0))
    return pl.pallas_call(
        functools.partial(_attn_b_kernel, rq=rq, unroll=UNROLL_B),
        grid=(bsz, B_HEADS // 2),
        in_specs=[pl.BlockSpec((None, 2 * rq, w), lambda b, h: (h, 0, 0)),
                  col(BQ0), col(BK0), col(BV0)],
        out_specs=pl.BlockSpec((None, seq, LANES), lambda b, h: (h, b, 0)),
        out_shape=jax.ShapeDtypeStruct((B_HEADS // 2, bsz * seq, LANES), BF16),
        scratch_shapes=[pltpu.VMEM((seq + B_PAD, LANES), BF16), pltpu.VMEM((seq + B_PAD, LANES), BF16)],
        compiler_params=_cparams("parallel", "parallel"),
        name="attn_b",
    )(bias, proj, proj, proj)


def _attn_c_kernel(q_ref, k_ref, v_ref, o_ref, *, tq, group):
    seq = q_ref.shape[0]
    nq = seq // tq
    row = lax.broadcasted_iota(jnp.int32, (tq, tq), 0)
    colm = lax.broadcasted_iota(jnp.int32, (tq, tq), 1)
    tri_strict = (row > colm).astype(BF16)
    before = colm < row

    def block(i, j, carry, acc, diag):
        qb = q_ref[pl.ds(pl.multiple_of(i * tq, tq), tq), :]
        c0 = pl.multiple_of(j * tq, tq)
        kb = k_ref[pl.ds(c0, tq), :]
        vb = v_ref[pl.ds(c0, tq), :]
        z = lax.dot_general(qb, kb, (((1,), (1,)), ((), ())), preferred_element_type=F32)
        lm = -(jnp.maximum(z, 0.0) + jnp.log2(1.0 + jnp.exp2(-jnp.abs(z))))
        if diag:
            lm = jnp.where(before, lm, 0.0)
        lm_b = lm.astype(BF16)
        excl = jnp.dot(lm_b, tri_strict, preferred_element_type=F32)
        logw = (z + lm) + excl + carry
        wgt = jnp.exp2(logw)
        if diag:
            wgt = jnp.where(before, wgt, 0.0)
        acc = acc + jnp.dot(wgt.astype(BF16), vb, preferred_element_type=F32)
        carry = carry + excl[:, 0:1] + lm_b[:, 0:1].astype(F32)
        return carry, acc

    def qgroup(gi, c):
        i0 = gi * group
        sts = [block(i0 + g, i0 + g, jnp.zeros((tq, 1), F32), jnp.zeros((tq, LANES), F32), True)
               for g in range(group)]
        carries = tuple(st[0] for st in sts)
        accs = tuple(st[1] for st in sts)

        def live(carries):
            return functools.reduce(jnp.maximum, [jnp.max(cr) for cr in carries])

        def cond(st):
            t, cmax, _, _ = st
            return jnp.logical_and(t <= i0 + group - 1, cmax > C_SKIP_LOG2)

        def body(st):
            t, _, carries, accs = st
            new = []
            for g in range(group):
                j = i0 + g - t
                cin = jnp.where(j >= 0, carries[g], NEG_INF)
                new.append(block(i0 + g, jnp.maximum(j, 0), cin, accs[g], False))
            carries = tuple(st_[0] for st_ in new)
            return t + 1, live(carries), carries, tuple(st_[1] for st_ in new)

        _, _, _, accs = lax.while_loop(cond, body, (jnp.int32(1), live(carries), carries, accs))
        for g in range(group):
            o_ref[pl.ds(pl.multiple_of((i0 + g) * tq, tq), tq), :] = accs[g].astype(o_ref.dtype)
        return c

    lax.fori_loop(0, nq // group, qgroup, 0)


def _attn_c(proj, bsz, seq):
    tq = min(TQ_C, seq)
    col = lambda base: pl.BlockSpec((None, seq, LANES), lambda b, h: (base + h, b, 0))
    return pl.pallas_call(
        functools.partial(_attn_c_kernel, tq=tq, group=GROUP_C),
        grid=(bsz, C_HEADS),
        in_specs=[col(CQ0), col(CK0), col(CV0)],
        out_specs=pl.BlockSpec((None, seq, LANES), lambda b, h: (h, b, 0)),
        out_shape=jax.ShapeDtypeStruct((C_HEADS, bsz * seq, LANES), BF16),
        compiler_params=_cparams("parallel", "parallel"),
        name="attn_c",
    )(proj, proj, proj)


def _cat_lanes(ref, start, count):
    return jnp.concatenate([ref[start + c] for c in range(count)], axis=1)


def _merge_kernel(x_ref, mod_ref, gate_ref, bg_ref, ya_ref, yb_ref, yc_ref, wb_ref, wo_ref, o_ref):
    blks = D_MODEL // LANES
    merged = None
    for r, y_ref in enumerate((ya_ref, yb_ref, yc_ref)):
        y = _cat_lanes(y_ref, 0, BRANCH_WIDTH // LANES)
        br = jnp.dot(y, wb_ref[r], preferred_element_type=F32)
        pre = _cat_lanes(gate_ref, r * blks, blks).astype(F32) + bg_ref[:, r * D_MODEL:(r + 1) * D_MODEL]
        term = jax.nn.sigmoid(pre) * br
        merged = term if merged is None else merged + term
    out = jnp.dot(merged.astype(BF16), wo_ref[...], preferred_element_type=F32)
    o_ref[...] = x_ref[...] + mod_ref[2:3, :] * out


def _merge(x2d, mod_l, proj, b_gate, ya, yb, yc, w_branch, w_out, seq):
    t, d = x2d.shape
    tm = min(TM_MERGE, seq)
    ybs = lambda: pl.BlockSpec((BRANCH_WIDTH // LANES, tm, LANES), lambda i: (0, i, 0))
    return pl.pallas_call(
        _merge_kernel,
        grid=(t // tm,),
        in_specs=[
            pl.BlockSpec((tm, d), lambda i: (i, 0)),
            pl.BlockSpec((None, 6, d), lambda i: ((i * tm) // seq, 0, 0)),
            pl.BlockSpec((GATE_BLKS, tm, LANES), lambda i: (0, i, 0)),
            pl.BlockSpec((1, GATE_COLS), lambda i: (0, 0)),
            ybs(), ybs(), ybs(),
            pl.BlockSpec((N_BRANCH, BRANCH_WIDTH, d), lambda i: (0, 0, 0)),
            pl.BlockSpec((d, d), lambda i: (0, 0)),
        ],
        out_specs=pl.BlockSpec((tm, d), lambda i: (i, 0)),
        out_shape=jax.ShapeDtypeStruct((t, d), F32),
        compiler_params=_cparams("parallel"),
        name="merge",
    )(x2d, mod_l, proj, b_gate, ya, yb, yc, w_branch, w_out)


def _ffn_kernel(x_ref, mod_ref, g_ref, fg_ref, w13_ref, w2_ref, o_ref, *, final):
    dff = w2_ref.shape[0]
    x = x_ref[...]
    h = _norm_mod(x, g_ref[...], mod_ref[3:4, :], mod_ref[4:5, :]).astype(BF16)
    acc = None
    for f0, f1 in FFN_CHUNKS:
        u_gate = jnp.dot(h, w13_ref[:, f0:f1], preferred_element_type=F32)
        u_up = jnp.dot(h, w13_ref[:, dff + f0:dff + f1], preferred_element_type=F32)
        act = (u_gate * jax.nn.sigmoid(u_gate) * u_up).astype(BF16)
        part = jnp.dot(act, w2_ref[f0:f1, :], preferred_element_type=F32)
        acc = part if acc is None else acc + part
    y = x + mod_ref[5:6, :] * acc
    if final:
        y = y * lax.rsqrt(jnp.mean(y * y, axis=-1, keepdims=True) + EPS) * fg_ref[...]
    o_ref[...] = y


def _ffn(x2d, mod_l, g, final_g, w13, w2, seq, final):
    t, d = x2d.shape
    dff = w2.shape[0]
    assert FFN_CHUNKS[0][0] == 0 and FFN_CHUNKS[-1][1] == dff
    tm = min(TM_FFN, seq)
    resident = lambda shape: pl.BlockSpec(shape, lambda i: (0, 0), pipeline_mode=pl.Buffered(1))
    return pl.pallas_call(
        functools.partial(_ffn_kernel, final=final),
        grid=(t // tm,),
        in_specs=[
            pl.BlockSpec((tm, d), lambda i: (i, 0)),
            pl.BlockSpec((None, 6, d), lambda i: ((i * tm) // seq, 0, 0)),
            pl.BlockSpec((1, d), lambda i: (0, 0)),
            pl.BlockSpec((1, d), lambda i: (0, 0)),
            resident((d, 2 * dff)),
            resident((dff, d)),
        ],
        out_specs=pl.BlockSpec((tm, d), lambda i: (i, 0)),
        out_shape=jax.ShapeDtypeStruct((t, d), F32),
        compiler_params=_cparams("parallel"),
        name="ffn",
    )(x2d, mod_l, g, final_g, w13, w2)


def _prep_w_in(w_in):
    scale = np.ones((IN_COLS,), np.float32)
    scale[0 * BRANCH_WIDTH:1 * BRANCH_WIDTH] = HEAD_DIM ** -0.5 * LOG2E
    scale[3 * BRANCH_WIDTH:4 * BRANCH_WIDTH] = HEAD_DIM ** -0.5 * LOG2E
    scale[6 * BRANCH_WIDTH:7 * BRANCH_WIDTH] = C_HEAD_DIM ** -0.5 * LOG2E
    w = w_in * jnp.asarray(scale)
    return jnp.concatenate([w[..., QKV_COLS:], w[..., :QKV_COLS]], axis=-1).astype(BF16)


def kernel(x, c, w_ada, b_ada, norm1_g, w_in, b_gate, lam_q1, lam_k1, lam_q2, lam_k2, subln_g,
           t5_table, rel_bias_b, w_branch, w_out, norm2_g, w13, w2, final_g):
    bsz, seq, d = x.shape
    depth = w_in.shape[0]
    t = bsz * seq

    mod = _ada_mod(c, w_ada, b_ada).reshape(depth, bsz, 6, d)
    w_in_b = _prep_w_in(w_in)
    w_branch_b = w_branch.astype(BF16)
    w_out_b = w_out.astype(BF16)
    w13_b = w13.astype(BF16)
    w2_b = w2.astype(BF16)
    tq_a = min(TQ_A, seq // 2)
    a_bias, a_bmax = _a_bias_tiles(t5_table, tq_a, min(TK_A, tq_a))
    fg = final_g.reshape(1, d)

    x2d = x.reshape(t, d)
    for l in range(depth):
        lam_init = 0.8 - 0.6 * math.exp(-0.3 * l)
        proj = _inproj(x2d, mod[l], norm1_g[l].reshape(1, d), w_in_b[l], seq)
        ya = _attn_a(proj, a_bias, a_bmax, lam_q1[l].reshape(1, -1), lam_k1[l].reshape(1, -1),
                     lam_q2[l].reshape(1, -1), lam_k2[l].reshape(1, -1), subln_g[l].reshape(-1, 1),
                     bsz, seq, lam_init)
        yb = _attn_b(proj, _b_bias_tiles(rel_bias_b[l], RQ_B), bsz, seq)
        yc = _attn_c(proj, bsz, seq)
        x2d = _merge(x2d, mod[l], proj, b_gate[l].reshape(1, -1), ya, yb, yc,
                     w_branch_b[l], w_out_b[l], seq)
        x2d = _ffn(x2d, mod[l], norm2_g[l].reshape(1, d), fg, w13_b[l], w2_b[l], seq,
                   final=(l == depth - 1))
    return x2d.reshape(bsz, seq, d)
```

```python
import functools
import math

import numpy as np
import jax
import jax.numpy as jnp
from jax import lax
from jax.experimental import pallas as pl
from jax.experimental.pallas import tpu as pltpu

F32 = jnp.float32
BF16 = jnp.bfloat16

D_MODEL = 1024
DEPTH = 4
CHUNK = 64
HEAD_DIM = 64
BRANCH_WIDTH = 512
N_BRANCH = 3
A_HEADS = 4
B_HEADS = 8
C_HEADS = 4
C_HEAD_DIM = 128
B_LEFT_CHUNKS = 8
B_PAD = B_LEFT_CHUNKS * CHUNK
B_MAX_REL = 128
T5_BUCKETS = 32
T5_MAX_DIST = 128
D_FF = 2816
QKV_COLS = 3 * N_BRANCH * BRANCH_WIDTH
GATE_COLS = N_BRANCH * D_MODEL
IN_COLS = QKV_COLS + GATE_COLS
NEG_INF = -1e30
EPS = 1e-6
LOG2E = math.log2(math.e)

LANES = 128
N_COLBLK = IN_COLS // LANES
GATE_BLKS = GATE_COLS // LANES
AQ0, AK0, AV0 = GATE_BLKS, GATE_BLKS + 4, GATE_BLKS + 8
BQ0, BK0, BV0 = GATE_BLKS + 12, GATE_BLKS + 16, GATE_BLKS + 20
CQ0, CK0, CV0 = GATE_BLKS + 24, GATE_BLKS + 28, GATE_BLKS + 32

TM_PROJ = 512
TN_PROJ = 1536
TM_MERGE = 512
TM_FFN = 512
FFN_CHUNKS = ((0, 1536), (1536, 2816))
TQ_A = 1024
TK_A = 512
TP_A = 512
TC_A = 1024
UNROLL_A = 2
NORM_SLACK_A = 1.02
L_MIN_A = 2.0 ** -100
TQ_C = 256
GROUP_C = 8
C_SKIP_LOG2 = -150.0
VMEM_LIMIT = 56 * 1024 * 1024


def _cparams(*sem, flags=None):
    return pltpu.CompilerParams(dimension_semantics=sem, vmem_limit_bytes=VMEM_LIMIT, flags=flags)


def _ada_kernel(c_ref, w_ref, b_ref, o_ref):
    c = c_ref[...]
    cs = c * jax.nn.sigmoid(c)
    o_ref[...] = jnp.dot(cs, w_ref[...], preferred_element_type=F32,
                         precision=lax.Precision.HIGHEST) + b_ref[...]


def _ada_mod(c, w_ada, b_ada):
    depth, d, e = w_ada.shape
    bsz = c.shape[0]
    nblk = e // d
    return pl.pallas_call(
        _ada_kernel,
        grid=(depth, nblk),
        in_specs=[
            pl.BlockSpec((bsz, d), lambda l, j: (0, 0)),
            pl.BlockSpec((None, d, d), lambda l, j: (l, 0, j)),
            pl.BlockSpec((None, 1, d), lambda l, j: (l, 0, j)),
        ],
        out_specs=pl.BlockSpec((None, bsz, d), lambda l, j: (l, 0, j)),
        out_shape=jax.ShapeDtypeStruct((depth, bsz, e), F32),
        compiler_params=_cparams("arbitrary", "arbitrary"),
        name="ada_mod",
    )(c, w_ada, b_ada.reshape(depth, 1, e))


def _norm_mod(x, g, shift, scale):
    ms = jnp.mean(x * x, axis=-1, keepdims=True)
    y = x * lax.rsqrt(ms + EPS) * g
    return y * (1.0 + scale) + shift


def _inproj_kernel(x_ref, mod_ref, g_ref, w_ref, o_ref):
    h = _norm_mod(x_ref[...], g_ref[...], mod_ref[0:1, :], mod_ref[1:2, :]).astype(BF16)
    n = w_ref.shape[1]
    for c0 in range(0, n, TN_PROJ):
        res = jnp.dot(h, w_ref[:, c0:c0 + TN_PROJ], preferred_element_type=F32)
        for cb in range(TN_PROJ // LANES):
            o_ref[c0 // LANES + cb] = res[:, cb * LANES:(cb + 1) * LANES].astype(o_ref.dtype)


def _inproj(x2d, mod_l, g, w, seq):
    t, d = x2d.shape
    n = w.shape[1]
    assert n % TN_PROJ == 0
    tm = min(TM_PROJ, seq)
    return pl.pallas_call(
        _inproj_kernel,
        grid=(t // tm,),
        in_specs=[
            pl.BlockSpec((tm, d), lambda i: (i, 0)),
            pl.BlockSpec((None, 6, d), lambda i: ((i * tm) // seq, 0, 0)),
            pl.BlockSpec((1, d), lambda i: (0, 0)),
            pl.BlockSpec((d, n), lambda i: (0, 0), pipeline_mode=pl.Buffered(1)),
        ],
        out_specs=pl.BlockSpec((n // LANES, tm, LANES), lambda i: (0, i, 0)),
        out_shape=jax.ShapeDtypeStruct((n // LANES, t, LANES), BF16),
        compiler_params=_cparams("parallel"),
        name="in_proj",
    )(x2d, mod_l, g, w)


def _t5_bucket_np(rel):
    nb = T5_BUCKETS // 2
    max_exact = nb // 2
    ret = np.where(rel > 0, nb, 0)
    n = np.abs(rel)
    nf = np.maximum(n, 1).astype(np.float32)
    scaled = (np.log(nf / np.float32(max_exact)) / np.float32(math.log(T5_MAX_DIST / max_exact))
              * np.float32(nb - max_exact))
    large = max_exact + scaled.astype(np.int32)
    large = np.minimum(large, nb - 1)
    return (ret + np.where(n < max_exact, n, large)).astype(np.int32)


def _toeplitz(vec, rows, cols):
    length = rows + cols - 1
    assert vec.shape[-1] == length
    lead = vec.shape[:-1]
    ext = jnp.concatenate([vec, jnp.zeros(lead + (1,), vec.dtype)], axis=-1)
    flat = jnp.broadcast_to(ext[..., None, :], lead + (rows, length + 1)).reshape(lead + (rows * (length + 1),))
    return flat[..., :rows * length].reshape(lead + (rows, length))[..., rows - 1:rows - 1 + cols]


def _a_bias_tiles(t5_table, tq, tk):
    r = np.arange(tq)[None, :]
    c = np.arange(tk)[:, None]
    far_bucket = _t5_bucket_np((c - 2 * tk) - r)
    assert (far_bucket == far_bucket[0, 0]).all() and far_bucket[0, 0] == _t5_bucket_np(np.array(-10 * tq))
    far = t5_table[int(far_bucket[0, 0])].astype(F32)
    tiles = [jnp.zeros((A_HEADS, tk, tq), F32)]
    bmax = jnp.zeros((A_HEADS,), F32)
    u = np.arange(tk + tq - 1)
    for koff in range(-tk, tq, tk):
        vec = t5_table[_t5_bucket_np(koff + tk - 1 - u)].astype(F32).T
        vec = (vec - far[:, None]) * LOG2E
        bmax = jnp.maximum(bmax, jnp.max(vec, axis=1))
        bias = _toeplitz(vec, tk, tq)
        allowed = ((c + koff) // CHUNK) <= (r // CHUNK)
        tiles.append(jnp.where(jnp.asarray(allowed)[None], bias, NEG_INF))
    tiles.append(jnp.full((A_HEADS, tk, tq), NEG_INF, F32))
    return jnp.stack(tiles, axis=1), jnp.broadcast_to(bmax[:, None, None], (A_HEADS, 1, LANES))


def _attn_a_kernel(lq1_ref, lk1_ref, lq2_ref, lk2_ref, subg_ref, bmax_ref, bias_ref, q_ref, k_ref, v_ref,
                   o_ref, vt_ref, acc_ref, *s_refs, tq, tk, lam_init):
    seq = q_ref.shape[0]
    nq = seq // tq
    nkb = seq // tk
    per_q = tq // tk
    unroll = len(s_refs)
    tp = TP_A
    tc = TC_A
    lam = (jnp.exp(jnp.sum(lq1_ref[...] * lk1_ref[...], axis=-1, keepdims=True))
           - jnp.exp(jnp.sum(lq2_ref[...] * lk2_ref[...], axis=-1, keepdims=True)) + lam_init)
    lane = lax.broadcasted_iota(jnp.int32, (1, LANES), 1)
    mask1 = (lane < HEAD_DIM).astype(BF16)
    mask2 = (lane >= HEAD_DIM).astype(BF16)
    subg = subg_ref[...] * (1.0 - lam_init)
    sel = (lax.broadcasted_iota(jnp.int32, (8, LANES), 0)
           == lax.broadcasted_iota(jnp.int32, (8, LANES), 1) // HEAD_DIM).astype(BF16)

    def prepare(b, kn2):
        c0 = pl.multiple_of(b * tk, tk)
        vt_ref[:, pl.ds(c0, tk)] = jnp.transpose(v_ref[pl.ds(c0, tk), :].astype(F32)).astype(BF16)
        kb = k_ref[pl.ds(c0, tk), :]
        blk = lax.dot_general(sel, kb * kb, (((1,), (1,)), ((), ())), preferred_element_type=F32)
        return jnp.maximum(kn2, blk)

    kn2 = lax.fori_loop(0, nkb, prepare, jnp.zeros((8, tk), F32))
    kmax2 = jnp.max(kn2, axis=1, keepdims=True) * NORM_SLACK_A
    kmax = jnp.sqrt(jnp.concatenate([jnp.broadcast_to(kmax2[0:1], (1, tq)),
                                     jnp.broadcast_to(kmax2[1:2], (1, tq))], axis=1))

    def finish(i, acc, l):
        o = acc / l
        d = o[:, :tq] - lam * o[:, tq:]
        y = d * lax.rsqrt(jnp.mean(d * d, axis=0, keepdims=True) + EPS) * subg
        o_ref[pl.ds(pl.multiple_of(i * tq, tq), tq), :] = jnp.transpose(y).astype(o_ref.dtype)

    def load_q(i):
        qb = q_ref[pl.ds(pl.multiple_of(i * tq, tq), tq), :]
        return jnp.concatenate([qb * mask1, qb * mask2], axis=0)

    def steps(i, step, st):
        n_plain = jnp.maximum(per_q * i - 1, 0) // unroll
        n_steps = (per_q * (i + 1) + unroll - 1) // unroll
        st = lax.fori_loop(0, n_plain, lambda it, s_: step(it, s_, False), st)
        return lax.fori_loop(n_plain, n_steps, lambda it, s_: step(it, s_, True), st)

    def bounded_pass(i):
        qq = load_q(i)
        qn2 = lax.dot_general(jnp.ones((8, LANES), BF16), qq * qq, (((1,), (1,)), ((), ())),
                              preferred_element_type=F32)[0:1]
        shift = jnp.sqrt(qn2 * NORM_SLACK_A) * kmax + (bmax_ref[:, 0:1] + 1.0)
        acc_ref[...] = jnp.zeros(acc_ref.shape, F32)

        def block(js, l, with_bias):
            c0 = pl.multiple_of(jnp.minimum(js, nkb - 1) * tk, tk)
            kb = k_ref[pl.ds(c0, tk), :]
            vt = vt_ref[:, pl.ds(c0, tk)]
            if with_bias:
                bias = bias_ref[jnp.clip(js - per_q * i + 2, 0, per_q + 2)]
            sums = []
            for ct in range(2 * tq // tc):
                cols = slice(ct * tc, (ct + 1) * tc)
                s = lax.dot_general(kb, qq[cols], (((1,), (1,)), ((), ())), preferred_element_type=F32)
                if with_bias:
                    s = s + jnp.concatenate([bias, bias], axis=1)[:, cols]
                p = jnp.exp2(s - shift[:, cols])
                sums.append(jnp.sum(p, axis=0, keepdims=True))
                acc_ref[:, cols] += jnp.dot(vt, p.astype(BF16), preferred_element_type=F32)
            return l + jnp.concatenate(sums, axis=1)

        def step(it, l, with_bias):
            for u in range(unroll):
                l = block(it * unroll + u, l, with_bias)
            return l

        l = steps(i, step, jnp.zeros((1, 2 * tq), F32))
        finish(i, acc_ref[...], l)
        return jnp.min(l)

    def running_max_pass(i):
        qq = load_q(i)
        acc_ref[...] = jnp.zeros(acc_ref.shape, F32)

        def block(js, st, with_bias, s_ref):
            m, l = st
            c0 = pl.multiple_of(jnp.minimum(js, nkb - 1) * tk, tk)
            kb = k_ref[pl.ds(c0, tk), :]
            s = lax.dot_general(kb, qq, (((1,), (1,)), ((), ())), preferred_element_type=F32)
            if with_bias:
                bias = bias_ref[jnp.clip(js - per_q * i + 2, 0, per_q + 2)]
                s = s + jnp.concatenate([bias, bias], axis=1)
            s_ref[...] = s
            m_new = jnp.maximum(m, jnp.max(s, axis=0, keepdims=True))
            alpha = jnp.exp2(m - m_new)
            sums = []
            for ct in range(2 * tq // tp):
                cols = slice(ct * tp, (ct + 1) * tp)
                p = jnp.exp2(s_ref[:, cols] - m_new[:, cols])
                sums.append(jnp.sum(p, axis=0, keepdims=True))
                vt = vt_ref[:, pl.ds(c0, tk)]
                acc_ref[:, cols] = (alpha[:, cols] * acc_ref[:, cols]
                                    + jnp.dot(vt, p.astype(BF16), preferred_element_type=F32))
            return m_new, alpha * l + jnp.concatenate(sums, axis=1)

        def step(it, st, with_bias):
            for u in range(unroll):
                st = block(it * unroll + u, st, with_bias, s_refs[u])
            return st

        _, l = steps(i, step, (jnp.full((1, 2 * tq), NEG_INF, F32), jnp.zeros((1, 2 * tq), F32)))
        finish(i, acc_ref[...], l)

    def qtile(i, carry):
        l_min = bounded_pass(i)

        @pl.when(jnp.logical_not(l_min > L_MIN_A))
        def _():
            running_max_pass(i)

        return carry

    lax.fori_loop(0, nq, qtile, 0)


def _attn_a(proj, bias, bmax, lq1, lk1, lq2, lk2, subg, bsz, seq, lam_init):
    tq = min(TQ_A, seq // 2)
    tk = min(TK_A, tq)
    assert tk == TP_A
    vec = lambda n: pl.BlockSpec((1, n), lambda b, h: (0, 0))
    col = lambda base: pl.BlockSpec((None, seq, LANES), lambda b, h: (base + h, b, 0))
    return pl.pallas_call(
        functools.partial(_attn_a_kernel, tq=tq, tk=tk, lam_init=lam_init),
        grid=(bsz, A_HEADS),
        in_specs=[vec(HEAD_DIM), vec(HEAD_DIM), vec(HEAD_DIM), vec(HEAD_DIM),
                  pl.BlockSpec((2 * HEAD_DIM, 1), lambda b, h: (0, 0)),
                  pl.BlockSpec((None, 1, LANES), lambda b, h: (h, 0, 0)),
                  pl.BlockSpec((None, tq // tk + 3, tk, tq), lambda b, h: (h, 0, 0, 0)),
                  col(AQ0), col(AK0), col(AV0)],
        out_specs=pl.BlockSpec((None, seq, LANES), lambda b, h: (h, b, 0)),
        out_shape=jax.ShapeDtypeStruct((A_HEADS, bsz * seq, LANES), BF16),
        scratch_shapes=[pltpu.VMEM((LANES, seq), BF16), pltpu.VMEM((LANES, 2 * tq), F32)]
        + [pltpu.VMEM((tk, 2 * tq), F32) for _ in range(UNROLL_A)],
        compiler_params=_cparams("parallel", "parallel"),
        name="attn_a",
    )(lq1, lk1, lq2, lk2, subg, bmax, bias, proj, proj, proj)


def _b_bias_tiles(rel_bias, tq):
    assert tq == B_PAD
    r = np.arange(tq)[None, :]
    c = np.arange(tq)[:, None]
    lo = CHUNK * (r // CHUNK)
    u = np.arange(2 * tq - 1)
    tiles = []
    for koff in (-tq, 0):
        in_band = (c + koff >= lo - B_PAD) & (c + koff < lo + CHUNK)
        idx = np.clip(u - (tq - 1) - koff, -B_MAX_REL, B_MAX_REL) + B_MAX_REL
        bias = _toeplitz(rel_bias[:, idx].astype(F32) * LOG2E, tq, tq)
        bias = jnp.where(jnp.asarray(in_band)[None], bias, NEG_INF)
        tiles.append(bias.reshape(B_HEADS // 2, 2, tq, tq).transpose(0, 2, 1, 3).reshape(B_HEADS // 2, tq, 2 * tq))
    bmax = jnp.max((rel_bias.astype(F32) * LOG2E).reshape(B_HEADS // 2, -1), axis=1)
    return jnp.stack(tiles, axis=1), jnp.broadcast_to(bmax[:, None, None], (B_HEADS // 2, 1, LANES))


def _attn_b_kernel(bmax_ref, bias_ref, q_ref, k_ref, v_ref, o_ref, vt_ref, *s_refs, tq):
    seq = q_ref.shape[0]
    nq = seq // tq
    lane = lax.broadcasted_iota(jnp.int32, (1, LANES), 1)
    mask1 = (lane < HEAD_DIM).astype(BF16)
    mask2 = (lane >= HEAD_DIM).astype(BF16)
    feat = lax.broadcasted_iota(jnp.int32, (LANES, tq), 0)
    sel = (lax.broadcasted_iota(jnp.int32, (8, LANES), 0)
           == lax.broadcasted_iota(jnp.int32, (8, LANES), 1) // HEAD_DIM).astype(BF16)

    def prepare(b, kn2):
        c0 = pl.multiple_of(b * tq, tq)
        vt_ref[:, pl.ds(c0, tq)] = jnp.transpose(v_ref[pl.ds(c0, tq), :].astype(F32)).astype(BF16)
        kb = k_ref[pl.ds(c0, tq), :]
        blk = lax.dot_general(sel, kb * kb, (((1,), (1,)), ((), ())), preferred_element_type=F32)
        return jnp.maximum(kn2, blk)

    kn2 = lax.fori_loop(0, nq, prepare, jnp.zeros((8, tq), F32))
    kmax2 = jnp.max(kn2, axis=1, keepdims=True) * NORM_SLACK_A
    kmax = jnp.sqrt(jnp.concatenate([jnp.broadcast_to(kmax2[0:1], (1, tq)),
                                     jnp.broadcast_to(kmax2[1:2], (1, tq))], axis=1))

    def tile_blocks(i, first):
        r0 = i * tq if isinstance(i, int) else pl.multiple_of(i * tq, tq)
        prev = r0 - tq if isinstance(i, int) else pl.multiple_of(r0 - tq, tq)
        qb = q_ref[pl.ds(r0, tq), :]
        qq = jnp.concatenate([qb * mask1, qb * mask2], axis=0)
        return r0, qq, (((r0, 1),) if first else ((prev, 0), (r0, 1)))

    def finish(r0, acc, den):
        o = acc / den
        o = jnp.where(feat < HEAD_DIM, o[:, :tq], o[:, tq:])
        o_ref[pl.ds(r0, tq), :] = jnp.transpose(o).astype(o_ref.dtype)

    def bounded_tile(i, first):
        r0, qq, blocks = tile_blocks(i, first)
        qn2 = lax.dot_general(jnp.ones((8, LANES), BF16), qq * qq, (((1,), (1,)), ((), ())),
                              preferred_element_type=F32)[0:1]
        shift = jnp.sqrt(qn2 * NORM_SLACK_A) * kmax + (bmax_ref[:, 0:1] + 1.0)
        acc = None
        den = None
        for c0, t in blocks:
            s = lax.dot_general(k_ref[pl.ds(c0, tq), :], qq, (((1,), (1,)), ((), ())),
                                preferred_element_type=F32) + bias_ref[t]
            p = jnp.exp2(s - shift)
            ps = jnp.sum(p, axis=0, keepdims=True)
            pv = jnp.dot(vt_ref[:, pl.ds(c0, tq)], p.astype(BF16), preferred_element_type=F32)
            den = ps if den is None else den + ps
            acc = pv if acc is None else acc + pv
        finish(r0, acc, den)
        return jnp.min(den)

    def exact_tile(i, refs, first):
        r0, qq, blocks = tile_blocks(i, first)
        m = None
        for (c0, t), s_ref in zip(blocks, refs):
            s = lax.dot_general(k_ref[pl.ds(c0, tq), :], qq, (((1,), (1,)), ((), ())),
                                preferred_element_type=F32) + bias_ref[t]
            s_ref[...] = s
            bm = jnp.max(s, axis=0, keepdims=True)
            m = bm if m is None else jnp.maximum(m, bm)
        acc = None
        den = None
        for (c0, t), s_ref in zip(blocks, refs):
            p = jnp.exp2(s_ref[...] - m)
            ps = jnp.sum(p, axis=0, keepdims=True)
            pv = jnp.dot(vt_ref[:, pl.ds(c0, tq)], p.astype(BF16), preferred_element_type=F32)
            den = ps if den is None else den + ps
            acc = pv if acc is None else acc + pv
        finish(r0, acc, den)

    def tiles(idx, first=False):
        mins = [bounded_tile(i, first) for i in idx]
        for i, l_min in zip(idx, mins):
            @pl.when(jnp.logical_not(l_min > L_MIN_A))
            def _():
                exact_tile(i, s_refs, first)

    tiles([0], first=True)

    def step(it, carry):
        tiles([2 * it + 1, 2 * it + 2])
        return carry

    lax.fori_loop(0, (nq - 1) // 2, step, 0)
    if (nq - 1) % 2:
        tiles([nq - 1])


def _attn_b(proj, bias, bmax, bsz, seq):
    tq = B_PAD
    col = lambda base: pl.BlockSpec((None, seq, LANES), lambda b, h: (base + h, b, 0))
    return pl.pallas_call(
        functools.partial(_attn_b_kernel, tq=tq),
        grid=(bsz, B_HEADS // 2),
        in_specs=[pl.BlockSpec((None, 1, LANES), lambda b, h: (h, 0, 0)),
                  pl.BlockSpec((None, 2, tq, 2 * tq), lambda b, h: (h, 0, 0, 0)),
                  col(BQ0), col(BK0), col(BV0)],
        out_specs=pl.BlockSpec((None, seq, LANES), lambda b, h: (h, b, 0)),
        out_shape=jax.ShapeDtypeStruct((B_HEADS // 2, bsz * seq, LANES), BF16),
        scratch_shapes=[pltpu.VMEM((LANES, seq), BF16)] + [pltpu.VMEM((tq, 2 * tq), F32) for _ in range(2)],
        compiler_params=_cparams("parallel", "parallel"),
        name="attn_b",
    )(bmax, bias, proj, proj, proj)


def _attn_c_kernel(q_ref, k_ref, v_ref, o_ref, *, tq, group):
    seq = q_ref.shape[0]
    nq = seq // tq
    row = lax.broadcasted_iota(jnp.int32, (tq, tq), 0)
    colm = lax.broadcasted_iota(jnp.int32, (tq, tq), 1)
    tri_strict = (row > colm).astype(BF16)
    before = colm < row

    def block(i, j, carry, acc, diag):
        qb = q_ref[pl.ds(pl.multiple_of(i * tq, tq), tq), :]
        c0 = pl.multiple_of(j * tq, tq)
        kb = k_ref[pl.ds(c0, tq), :]
        vb = v_ref[pl.ds(c0, tq), :]
        z = lax.dot_general(qb, kb, (((1,), (1,)), ((), ())), preferred_element_type=F32)
        lm = -(jnp.maximum(z, 0.0) + jnp.log2(1.0 + jnp.exp2(-jnp.abs(z))))
        if diag:
            lm = jnp.where(before, lm, 0.0)
        lm_b = lm.astype(BF16)
        excl = jnp.dot(lm_b, tri_strict, preferred_element_type=F32)
        logw = (z + lm) + excl + carry
        wgt = jnp.exp2(logw)
        if diag:
            wgt = jnp.where(before, wgt, 0.0)
        acc = acc + jnp.dot(wgt.astype(BF16), vb, preferred_element_type=F32)
        carry = carry + excl[:, 0:1] + lm_b[:, 0:1].astype(F32)
        return carry, acc

    def qgroup(gi, c):
        i0 = gi * group
        sts = [block(i0 + g, i0 + g, jnp.zeros((tq, 1), F32), jnp.zeros((tq, LANES), F32), True)
               for g in range(group)]
        carries = tuple(st[0] for st in sts)
        accs = tuple(st[1] for st in sts)

        def live(carries):
            return functools.reduce(jnp.maximum, [jnp.max(cr) for cr in carries])

        def cond(st):
            t, cmax, _, _ = st
            return jnp.logical_and(t <= i0 + group - 1, cmax > C_SKIP_LOG2)

        def body(st):
            t, _, carries, accs = st
            new = []
            for g in range(group):
                j = i0 + g - t
                cin = jnp.where(j >= 0, carries[g], NEG_INF)
                new.append(block(i0 + g, jnp.maximum(j, 0), cin, accs[g], False))
            carries = tuple(st_[0] for st_ in new)
            return t + 1, live(carries), carries, tuple(st_[1] for st_ in new)

        _, _, _, accs = lax.while_loop(cond, body, (jnp.int32(1), live(carries), carries, accs))
        for g in range(group):
            o_ref[pl.ds(pl.multiple_of((i0 + g) * tq, tq), tq), :] = accs[g].astype(o_ref.dtype)
        return c

    lax.fori_loop(0, nq // group, qgroup, 0)


def _attn_c(proj, bsz, seq):
    tq = min(TQ_C, seq)
    col = lambda base: pl.BlockSpec((None, seq, LANES), lambda b, h: (base + h, b, 0))
    return pl.pallas_call(
        functools.partial(_attn_c_kernel, tq=tq, group=GROUP_C),
        grid=(bsz, C_HEADS),
        in_specs=[col(CQ0), col(CK0), col(CV0)],
        out_specs=pl.BlockSpec((None, seq, LANES), lambda b, h: (h, b, 0)),
        out_shape=jax.ShapeDtypeStruct((C_HEADS, bsz * seq, LANES), BF16),
        compiler_params=_cparams("parallel", "parallel"),
        name="attn_c",
    )(proj, proj, proj)


def _cat_lanes(ref, start, count):
    return jnp.concatenate([ref[start + c] for c in range(count)], axis=1)


def _merge_kernel(x_ref, mod_ref, gate_ref, bg_ref, ya_ref, yb_ref, yc_ref, wb_ref, wo_ref, o_ref):
    blks = D_MODEL // LANES
    merged = None
    for r, y_ref in enumerate((ya_ref, yb_ref, yc_ref)):
        y = _cat_lanes(y_ref, 0, BRANCH_WIDTH // LANES)
        br = jnp.dot(y, wb_ref[r], preferred_element_type=F32)
        pre = _cat_lanes(gate_ref, r * blks, blks).astype(F32) + bg_ref[:, r * D_MODEL:(r + 1) * D_MODEL]
        term = jax.nn.sigmoid(pre) * br
        merged = term if merged is None else merged + term
    out = jnp.dot(merged.astype(BF16), wo_ref[...], preferred_element_type=F32)
    o_ref[...] = x_ref[...] + mod_ref[2:3, :] * out


def _merge(x2d, mod_l, proj, b_gate, ya, yb, yc, w_branch, w_out, seq):
    t, d = x2d.shape
    tm = min(TM_MERGE, seq)
    ybs = lambda: pl.BlockSpec((BRANCH_WIDTH // LANES, tm, LANES), lambda i: (0, i, 0))
    return pl.pallas_call(
        _merge_kernel,
        grid=(t // tm,),
        in_specs=[
            pl.BlockSpec((tm, d), lambda i: (i, 0)),
            pl.BlockSpec((None, 6, d), lambda i: ((i * tm) // seq, 0, 0)),
            pl.BlockSpec((GATE_BLKS, tm, LANES), lambda i: (0, i, 0)),
            pl.BlockSpec((1, GATE_COLS), lambda i: (0, 0)),
            ybs(), ybs(), ybs(),
            pl.BlockSpec((N_BRANCH, BRANCH_WIDTH, d), lambda i: (0, 0, 0)),
            pl.BlockSpec((d, d), lambda i: (0, 0)),
        ],
        out_specs=pl.BlockSpec((tm, d), lambda i: (i, 0)),
        out_shape=jax.ShapeDtypeStruct((t, d), F32),
        compiler_params=_cparams("parallel"),
        name="merge",
    )(x2d, mod_l, proj, b_gate, ya, yb, yc, w_branch, w_out)


def _ffn_kernel(x_ref, mod_ref, g_ref, fg_ref, w13_ref, w2_ref, o_ref, *, final):
    dff = w2_ref.shape[0]
    x = x_ref[...]
    h = _norm_mod(x, g_ref[...], mod_ref[3:4, :], mod_ref[4:5, :]).astype(BF16)
    acc = None
    for f0, f1 in FFN_CHUNKS:
        u_gate = jnp.dot(h, w13_ref[:, f0:f1], preferred_element_type=F32)
        u_up = jnp.dot(h, w13_ref[:, dff + f0:dff + f1], preferred_element_type=F32)
        act = (u_gate * jax.nn.sigmoid(u_gate) * u_up).astype(BF16)
        part = jnp.dot(act, w2_ref[f0:f1, :], preferred_element_type=F32)
        acc = part if acc is None else acc + part
    y = x + mod_ref[5:6, :] * acc
    if final:
        y = y * lax.rsqrt(jnp.mean(y * y, axis=-1, keepdims=True) + EPS) * fg_ref[...]
    o_ref[...] = y


def _ffn(x2d, mod_l, g, final_g, w13, w2, seq, final):
    t, d = x2d.shape
    dff = w2.shape[0]
    assert FFN_CHUNKS[0][0] == 0 and FFN_CHUNKS[-1][1] == dff
    tm = min(TM_FFN, seq)
    resident = lambda shape: pl.BlockSpec(shape, lambda i: (0, 0), pipeline_mode=pl.Buffered(1))
    return pl.pallas_call(
        functools.partial(_ffn_kernel, final=final),
        grid=(t // tm,),
        in_specs=[
            pl.BlockSpec((tm, d), lambda i: (i, 0)),
            pl.BlockSpec((None, 6, d), lambda i: ((i * tm) // seq, 0, 0)),
            pl.BlockSpec((1, d), lambda i: (0, 0)),
            pl.BlockSpec((1, d), lambda i: (0, 0)),
            resident((d, 2 * dff)),
            resident((dff, d)),
        ],
        out_specs=pl.BlockSpec((tm, d), lambda i: (i, 0)),
        out_shape=jax.ShapeDtypeStruct((t, d), F32),
        compiler_params=_cparams("parallel"),
        name="ffn",
    )(x2d, mod_l, g, final_g, w13, w2)


def _prep_w_in(w_in):
    scale = np.ones((IN_COLS,), np.float32)
    scale[0 * BRANCH_WIDTH:1 * BRANCH_WIDTH] = HEAD_DIM ** -0.5 * LOG2E
    scale[3 * BRANCH_WIDTH:4 * BRANCH_WIDTH] = HEAD_DIM ** -0.5 * LOG2E
    scale[6 * BRANCH_WIDTH:7 * BRANCH_WIDTH] = C_HEAD_DIM ** -0.5 * LOG2E
    w = w_in * jnp.asarray(scale)
    return jnp.concatenate([w[..., QKV_COLS:], w[..., :QKV_COLS]], axis=-1).astype(BF16)


def kernel(x, c, w_ada, b_ada, norm1_g, w_in, b_gate, lam_q1, lam_k1, lam_q2, lam_k2, subln_g,
           t5_table, rel_bias_b, w_branch, w_out, norm2_g, w13, w2, final_g):
    bsz, seq, d = x.shape
    depth = w_in.shape[0]
    t = bsz * seq

    mod = _ada_mod(c, w_ada, b_ada).reshape(depth, bsz, 6, d)
    w_in_b = _prep_w_in(w_in)
    w_branch_b = w_branch.astype(BF16)
    w_out_b = w_out.astype(BF16)
    w13_b = w13.astype(BF16)
    w2_b = w2.astype(BF16)
    tq_a = min(TQ_A, seq // 2)
    a_bias, a_bmax = _a_bias_tiles(t5_table, tq_a, min(TK_A, tq_a))
    fg = final_g.reshape(1, d)

    x2d = x.reshape(t, d)
    for l in range(depth):
        lam_init = 0.8 - 0.6 * math.exp(-0.3 * l)
        proj = _inproj(x2d, mod[l], norm1_g[l].reshape(1, d), w_in_b[l], seq)
        ya = _attn_a(proj, a_bias, a_bmax, lam_q1[l].reshape(1, -1), lam_k1[l].reshape(1, -1),
                     lam_q2[l].reshape(1, -1), lam_k2[l].reshape(1, -1), subln_g[l].reshape(-1, 1),
                     bsz, seq, lam_init)
        yb = _attn_b(proj, *_b_bias_tiles(rel_bias_b[l], B_PAD), bsz, seq)
        yc = _attn_c(proj, bsz, seq)
        x2d = _merge(x2d, mod[l], proj, b_gate[l].reshape(1, -1), ya, yb, yc,
                     w_branch_b[l], w_out_b[l], seq)
        x2d = _ffn(x2d, mod[l], norm2_g[l].reshape(1, d), fg, w13_b[l], w2_b[l], seq,
                   final=(l == depth - 1))
    return x2d.reshape(bsz, seq, d)
```

```python
import functools
import math

import numpy as np
import jax
import jax.numpy as jnp
from jax import lax
from jax.experimental import pallas as pl
from jax.experimental.pallas import tpu as pltpu

F32 = jnp.float32
BF16 = jnp.bfloat16

D_MODEL = 1024
DEPTH = 4
CHUNK = 64
HEAD_DIM = 64
BRANCH_WIDTH = 512
N_BRANCH = 3
A_HEADS = 4
B_HEADS = 8
C_HEADS = 4
C_HEAD_DIM = 128
B_LEFT_CHUNKS = 8
B_PAD = B_LEFT_CHUNKS * CHUNK
B_MAX_REL = 128
T5_BUCKETS = 32
T5_MAX_DIST = 128
D_FF = 2816
QKV_COLS = 3 * N_BRANCH * BRANCH_WIDTH
GATE_COLS = N_BRANCH * D_MODEL
IN_COLS = QKV_COLS + GATE_COLS
NEG_INF = -1e30
EPS = 1e-6
LOG2E = math.log2(math.e)

LANES = 128
N_COLBLK = IN_COLS // LANES
GATE_BLKS = GATE_COLS // LANES
AQ0, AK0, AV0 = GATE_BLKS, GATE_BLKS + 4, GATE_BLKS + 8
BQ0, BK0, BV0 = GATE_BLKS + 12, GATE_BLKS + 16, GATE_BLKS + 20
CQ0, CK0, CV0 = GATE_BLKS + 24, GATE_BLKS + 28, GATE_BLKS + 32

TM_PROJ = 512
TN_PROJ = 1536
TM_MERGE = 512
TM_FFN = 512
FFN_CHUNKS = ((0, 1536), (1536, 2816))
TQ_A = 1024
TK_A = 512
TP_A = 512
TC_A = 1024
UNROLL_A = 2
NORM_SLACK_A = 1.02
L_MIN_A = 2.0 ** -100
TQ_B = 512
UNROLL_B = 3
TQ_C = 256
GROUP_C = 8
C_SKIP_LOG2 = -150.0
VMEM_LIMIT = 56 * 1024 * 1024


def _cparams(*sem, flags=None):
    return pltpu.CompilerParams(dimension_semantics=sem, vmem_limit_bytes=VMEM_LIMIT, flags=flags)


def _ada_kernel(c_ref, w_ref, b_ref, o_ref):
    c = c_ref[...]
    cs = c * jax.nn.sigmoid(c)
    o_ref[...] = jnp.dot(cs, w_ref[...], preferred_element_type=F32,
                         precision=lax.Precision.HIGHEST) + b_ref[...]


def _ada_mod(c, w_ada, b_ada):
    depth, d, e = w_ada.shape
    bsz = c.shape[0]
    nblk = e // d
    return pl.pallas_call(
        _ada_kernel,
        grid=(depth, nblk),
        in_specs=[
            pl.BlockSpec((bsz, d), lambda l, j: (0, 0)),
            pl.BlockSpec((None, d, d), lambda l, j: (l, 0, j)),
            pl.BlockSpec((None, 1, d), lambda l, j: (l, 0, j)),
        ],
        out_specs=pl.BlockSpec((None, bsz, d), lambda l, j: (l, 0, j)),
        out_shape=jax.ShapeDtypeStruct((depth, bsz, e), F32),
        compiler_params=_cparams("arbitrary", "arbitrary"),
        name="ada_mod",
    )(c, w_ada, b_ada.reshape(depth, 1, e))


def _norm_mod(x, g, shift, scale):
    ms = jnp.mean(x * x, axis=-1, keepdims=True)
    y = x * lax.rsqrt(ms + EPS) * g
    return y * (1.0 + scale) + shift


def _inproj_kernel(x_ref, mod_ref, g_ref, w_ref, o_ref):
    h = _norm_mod(x_ref[...], g_ref[...], mod_ref[0:1, :], mod_ref[1:2, :]).astype(BF16)
    n = w_ref.shape[1]
    for c0 in range(0, n, TN_PROJ):
        res = jnp.dot(h, w_ref[:, c0:c0 + TN_PROJ], preferred_element_type=F32)
        for cb in range(TN_PROJ // LANES):
            o_ref[c0 // LANES + cb] = res[:, cb * LANES:(cb + 1) * LANES].astype(o_ref.dtype)


def _inproj(x2d, mod_l, g, w, seq):
    t, d = x2d.shape
    n = w.shape[1]
    assert n % TN_PROJ == 0
    tm = min(TM_PROJ, seq)
    return pl.pallas_call(
        _inproj_kernel,
        grid=(t // tm,),
        in_specs=[
            pl.BlockSpec((tm, d), lambda i: (i, 0)),
            pl.BlockSpec((None, 6, d), lambda i: ((i * tm) // seq, 0, 0)),
            pl.BlockSpec((1, d), lambda i: (0, 0)),
            pl.BlockSpec((d, n), lambda i: (0, 0), pipeline_mode=pl.Buffered(1)),
        ],
        out_specs=pl.BlockSpec((n // LANES, tm, LANES), lambda i: (0, i, 0)),
        out_shape=jax.ShapeDtypeStruct((n // LANES, t, LANES), BF16),
        compiler_params=_cparams("parallel"),
        name="in_proj",
    )(x2d, mod_l, g, w)


def _t5_bucket_np(rel):
    nb = T5_BUCKETS // 2
    max_exact = nb // 2
    ret = np.where(rel > 0, nb, 0)
    n = np.abs(rel)
    nf = np.maximum(n, 1).astype(np.float32)
    scaled = (np.log(nf / np.float32(max_exact)) / np.float32(math.log(T5_MAX_DIST / max_exact))
              * np.float32(nb - max_exact))
    large = max_exact + scaled.astype(np.int32)
    large = np.minimum(large, nb - 1)
    return (ret + np.where(n < max_exact, n, large)).astype(np.int32)


def _toeplitz(vec, rows, cols):
    length = rows + cols - 1
    assert vec.shape[-1] == length
    lead = vec.shape[:-1]
    ext = jnp.concatenate([vec, jnp.zeros(lead + (1,), vec.dtype)], axis=-1)
    flat = jnp.broadcast_to(ext[..., None, :], lead + (rows, length + 1)).reshape(lead + (rows * (length + 1),))
    return flat[..., :rows * length].reshape(lead + (rows, length))[..., rows - 1:rows - 1 + cols]


def _a_bias_tiles(t5_table, tq, tk):
    r = np.arange(tq)[None, :]
    c = np.arange(tk)[:, None]
    far_bucket = _t5_bucket_np((c - 2 * tk) - r)
    assert (far_bucket == far_bucket[0, 0]).all() and far_bucket[0, 0] == _t5_bucket_np(np.array(-10 * tq))
    far = t5_table[int(far_bucket[0, 0])].astype(F32)
    tiles = [jnp.zeros((A_HEADS, tk, tq), F32)]
    bmax = jnp.zeros((A_HEADS,), F32)
    u = np.arange(tk + tq - 1)
    for koff in range(-tk, tq, tk):
        vec = t5_table[_t5_bucket_np(koff + tk - 1 - u)].astype(F32).T
        vec = (vec - far[:, None]) * LOG2E
        bmax = jnp.maximum(bmax, jnp.max(vec, axis=1))
        bias = _toeplitz(vec, tk, tq)
        allowed = ((c + koff) // CHUNK) <= (r // CHUNK)
        tiles.append(jnp.where(jnp.asarray(allowed)[None], bias, NEG_INF))
    tiles.append(jnp.full((A_HEADS, tk, tq), NEG_INF, F32))
    return jnp.stack(tiles, axis=1), jnp.broadcast_to(bmax[:, None, None], (A_HEADS, 1, LANES))


def _attn_a_kernel(lq1_ref, lk1_ref, lq2_ref, lk2_ref, subg_ref, bmax_ref, bias_ref, q_ref, k_ref, v_ref,
                   o_ref, vt_ref, acc_ref, *s_refs, tq, tk, lam_init):
    seq = q_ref.shape[0]
    nq = seq // tq
    nkb = seq // tk
    per_q = tq // tk
    unroll = len(s_refs)
    tp = TP_A
    tc = TC_A
    lam = (jnp.exp(jnp.sum(lq1_ref[...] * lk1_ref[...], axis=-1, keepdims=True))
           - jnp.exp(jnp.sum(lq2_ref[...] * lk2_ref[...], axis=-1, keepdims=True)) + lam_init)
    lane = lax.broadcasted_iota(jnp.int32, (1, LANES), 1)
    mask1 = (lane < HEAD_DIM).astype(BF16)
    mask2 = (lane >= HEAD_DIM).astype(BF16)
    subg = subg_ref[...] * (1.0 - lam_init)
    sel = (lax.broadcasted_iota(jnp.int32, (8, LANES), 0)
           == lax.broadcasted_iota(jnp.int32, (8, LANES), 1) // HEAD_DIM).astype(BF16)

    def prepare(b, kn2):
        c0 = pl.multiple_of(b * tk, tk)
        vt_ref[:, pl.ds(c0, tk)] = jnp.transpose(v_ref[pl.ds(c0, tk), :])
        kb = k_ref[pl.ds(c0, tk), :]
        blk = lax.dot_general(sel, kb * kb, (((1,), (1,)), ((), ())), preferred_element_type=F32)
        return jnp.maximum(kn2, blk)

    kn2 = lax.fori_loop(0, nkb, prepare, jnp.zeros((8, tk), F32))
    kmax2 = jnp.max(kn2, axis=1, keepdims=True) * NORM_SLACK_A
    kmax = jnp.sqrt(jnp.concatenate([jnp.broadcast_to(kmax2[0:1], (1, tq)),
                                     jnp.broadcast_to(kmax2[1:2], (1, tq))], axis=1))

    def finish(i, acc, l):
        o = acc * (1.0 / l)
        d = o[:, :tq] - lam * o[:, tq:]
        y = d * lax.rsqrt(jnp.mean(d * d, axis=0, keepdims=True) + EPS) * subg
        o_ref[pl.ds(pl.multiple_of(i * tq, tq), tq), :] = jnp.transpose(y).astype(o_ref.dtype)

    def load_q(i):
        qb = q_ref[pl.ds(pl.multiple_of(i * tq, tq), tq), :]
        return jnp.concatenate([qb * mask1, qb * mask2], axis=0)

    def steps(i, step, st):
        n_plain = jnp.maximum(per_q * i - 1, 0) // unroll
        n_steps = (per_q * (i + 1) + unroll - 1) // unroll
        st = lax.fori_loop(0, n_plain, lambda it, s_: step(it, s_, False), st)
        return lax.fori_loop(n_plain, n_steps, lambda it, s_: step(it, s_, True), st)

    def bounded_pass(i):
        qq = load_q(i)
        qn2 = lax.dot_general(jnp.ones((8, LANES), BF16), qq * qq, (((1,), (1,)), ((), ())),
                              preferred_element_type=F32)[0:1]
        shift = jnp.sqrt(qn2 * NORM_SLACK_A) * kmax + (bmax_ref[:, 0:1] + 1.0)
        acc_ref[...] = jnp.zeros(acc_ref.shape, F32)

        def block(js, l, with_bias):
            c0 = pl.multiple_of(jnp.minimum(js, nkb - 1) * tk, tk)
            kb = k_ref[pl.ds(c0, tk), :]
            vt = vt_ref[:, pl.ds(c0, tk)]
            if with_bias:
                bias = bias_ref[jnp.clip(js - per_q * i + 2, 0, per_q + 2)]
            sums = []
            for ct in range(2 * tq // tc):
                cols = slice(ct * tc, (ct + 1) * tc)
                s = lax.dot_general(kb, qq[cols], (((1,), (1,)), ((), ())), preferred_element_type=F32)
                if with_bias:
                    s = s + jnp.concatenate([bias, bias], axis=1)[:, cols]
                p = jnp.exp2(s - shift[:, cols])
                sums.append(jnp.sum(p, axis=0, keepdims=True))
                acc_ref[:, cols] += jnp.dot(vt, p.astype(BF16), preferred_element_type=F32)
            return l + jnp.concatenate(sums, axis=1)

        def step(it, l, with_bias):
            for u in range(unroll):
                l = block(it * unroll + u, l, with_bias)
            return l

        l = steps(i, step, jnp.zeros((1, 2 * tq), F32))
        finish(i, acc_ref[...], l)
        return jnp.min(l)

    def running_max_pass(i):
        qq = load_q(i)
        acc_ref[...] = jnp.zeros(acc_ref.shape, F32)

        def block(js, st, with_bias, s_ref):
            m, l = st
            c0 = pl.multiple_of(jnp.minimum(js, nkb - 1) * tk, tk)
            kb = k_ref[pl.ds(c0, tk), :]
            s = lax.dot_general(kb, qq, (((1,), (1,)), ((), ())), preferred_element_type=F32)
            if with_bias:
                bias = bias_ref[jnp.clip(js - per_q * i + 2, 0, per_q + 2)]
                s = s + jnp.concatenate([bias, bias], axis=1)
            s_ref[...] = s
            m_new = jnp.maximum(m, jnp.max(s, axis=0, keepdims=True))
            alpha = jnp.exp2(m - m_new)
            sums = []
            for ct in range(2 * tq // tp):
                cols = slice(ct * tp, (ct + 1) * tp)
                p = jnp.exp2(s_ref[:, cols] - m_new[:, cols])
                sums.append(jnp.sum(p, axis=0, keepdims=True))
                vt = vt_ref[:, pl.ds(c0, tk)]
                acc_ref[:, cols] = (alpha[:, cols] * acc_ref[:, cols]
                                    + jnp.dot(vt, p.astype(BF16), preferred_element_type=F32))
            return m_new, alpha * l + jnp.concatenate(sums, axis=1)

        def step(it, st, with_bias):
            for u in range(unroll):
                st = block(it * unroll + u, st, with_bias, s_refs[u])
            return st

        _, l = steps(i, step, (jnp.full((1, 2 * tq), NEG_INF, F32), jnp.zeros((1, 2 * tq), F32)))
        finish(i, acc_ref[...], l)

    def qtile(i, carry):
        l_min = bounded_pass(i)

        @pl.when(jnp.logical_not(l_min > L_MIN_A))
        def _():
            running_max_pass(i)

        return carry

    lax.fori_loop(0, nq, qtile, 0)


def _attn_a(proj, bias, bmax, lq1, lk1, lq2, lk2, subg, bsz, seq, lam_init):
    tq = min(TQ_A, seq // 2)
    tk = min(TK_A, tq)
    assert tk == TP_A
    vec = lambda n: pl.BlockSpec((1, n), lambda b, h: (0, 0))
    col = lambda base: pl.BlockSpec((None, seq, LANES), lambda b, h: (base + h, b, 0))
    return pl.pallas_call(
        functools.partial(_attn_a_kernel, tq=tq, tk=tk, lam_init=lam_init),
        grid=(bsz, A_HEADS),
        in_specs=[vec(HEAD_DIM), vec(HEAD_DIM), vec(HEAD_DIM), vec(HEAD_DIM),
                  pl.BlockSpec((2 * HEAD_DIM, 1), lambda b, h: (0, 0)),
                  pl.BlockSpec((None, 1, LANES), lambda b, h: (h, 0, 0)),
                  pl.BlockSpec((None, tq // tk + 3, tk, tq), lambda b, h: (h, 0, 0, 0)),
                  col(AQ0), col(AK0), col(AV0)],
        out_specs=pl.BlockSpec((None, seq, LANES), lambda b, h: (h, b, 0)),
        out_shape=jax.ShapeDtypeStruct((A_HEADS, bsz * seq, LANES), BF16),
        scratch_shapes=[pltpu.VMEM((LANES, seq), BF16), pltpu.VMEM((LANES, 2 * tq), F32)]
        + [pltpu.VMEM((tk, 2 * tq), F32) for _ in range(UNROLL_A)],
        compiler_params=_cparams("parallel", "parallel"),
        name="attn_a",
    )(lq1, lk1, lq2, lk2, subg, bmax, bias, proj, proj, proj)


def _b_bias_tiles(rel_bias, tq):
    assert B_PAD % tq == 0
    r = np.arange(tq)[None, :]
    c = np.arange(tq)[:, None]
    lo = CHUNK * (r // CHUNK)
    u = np.arange(2 * tq - 1)
    tiles = []
    for koff in range(-B_PAD, tq, tq):
        in_band = (c + koff >= lo - B_PAD) & (c + koff < lo + CHUNK)
        idx = np.clip(u - (tq - 1) - koff, -B_MAX_REL, B_MAX_REL) + B_MAX_REL
        bias = _toeplitz(rel_bias[:, idx].astype(F32) * LOG2E, tq, tq)
        bias = jnp.where(jnp.asarray(in_band)[None], bias, NEG_INF)
        tiles.append(bias.reshape(B_HEADS // 2, 2, tq, tq).transpose(0, 2, 1, 3).reshape(B_HEADS // 2, tq, 2 * tq))
    bmax = jnp.max((rel_bias.astype(F32) * LOG2E).reshape(B_HEADS // 2, -1), axis=1)
    return jnp.stack(tiles, axis=1), jnp.broadcast_to(bmax[:, None, None], (B_HEADS // 2, 1, LANES))


def _attn_b_kernel(bmax_ref, bias_ref, q_ref, k_ref, v_ref, o_ref, vt_ref, *s_refs, tq):
    seq = q_ref.shape[0]
    nq = seq // tq
    max_prev = B_PAD // tq
    lane = lax.broadcasted_iota(jnp.int32, (1, LANES), 1)
    mask1 = (lane < HEAD_DIM).astype(BF16)
    mask2 = (lane >= HEAD_DIM).astype(BF16)
    feat = lax.broadcasted_iota(jnp.int32, (LANES, tq), 0)
    sel = (lax.broadcasted_iota(jnp.int32, (8, LANES), 0)
           == lax.broadcasted_iota(jnp.int32, (8, LANES), 1) // HEAD_DIM).astype(BF16)

    def prepare(b, kn2):
        c0 = pl.multiple_of(b * tq, tq)
        vt_ref[:, pl.ds(c0, tq)] = jnp.transpose(v_ref[pl.ds(c0, tq), :])
        kb = k_ref[pl.ds(c0, tq), :]
        blk = lax.dot_general(sel, kb * kb, (((1,), (1,)), ((), ())), preferred_element_type=F32)
        return jnp.maximum(kn2, blk)

    kn2 = lax.fori_loop(0, nq, prepare, jnp.zeros((8, tq), F32))
    kmax2 = jnp.max(kn2, axis=1, keepdims=True) * NORM_SLACK_A
    kmax = jnp.sqrt(jnp.concatenate([jnp.broadcast_to(kmax2[0:1], (1, tq)),
                                     jnp.broadcast_to(kmax2[1:2], (1, tq))], axis=1))

    def tile_blocks(i, n_prev):
        static = isinstance(i, int)
        r0 = i * tq if static else pl.multiple_of(i * tq, tq)
        qb = q_ref[pl.ds(r0, tq), :]
        qq = jnp.concatenate([qb * mask1, qb * mask2], axis=0)
        starts = [r0 - d * tq for d in range(n_prev, 0, -1)] + [r0]
        blocks = tuple((c0 if static else pl.multiple_of(c0, tq), max_prev - n_prev + j)
                       for j, c0 in enumerate(starts))
        return r0, qq, blocks

    def finish(r0, acc, den):
        o = acc * (1.0 / den)
        o = jnp.where(feat < HEAD_DIM, o[:, :tq], o[:, tq:])
        o_ref[pl.ds(r0, tq), :] = jnp.transpose(o).astype(o_ref.dtype)

    def bounded_tile(i, n_prev):
        r0, qq, blocks = tile_blocks(i, n_prev)
        qn2 = lax.dot_general(jnp.ones((8, LANES), BF16), qq * qq, (((1,), (1,)), ((), ())),
                              preferred_element_type=F32)[0:1]
        shift = jnp.sqrt(qn2 * NORM_SLACK_A) * kmax + (bmax_ref[:, 0:1] + 1.0)
        acc = None
        den = None
        for c0, t in blocks:
            s = lax.dot_general(k_ref[pl.ds(c0, tq), :], qq, (((1,), (1,)), ((), ())),
                                preferred_element_type=F32) + bias_ref[t]
            p = jnp.exp2(s - shift)
            ps = jnp.sum(p, axis=0, keepdims=True)
            pv = jnp.dot(vt_ref[:, pl.ds(c0, tq)], p.astype(BF16), preferred_element_type=F32)
            den = ps if den is None else den + ps
            acc = pv if acc is None else acc + pv
        finish(r0, acc, den)
        return jnp.min(den)

    def exact_tile(i, refs, n_prev):
        r0, qq, blocks = tile_blocks(i, n_prev)
        m = None
        for (c0, t), s_ref in zip(blocks, refs):
            s = lax.dot_general(k_ref[pl.ds(c0, tq), :], qq, (((1,), (1,)), ((), ())),
                                preferred_element_type=F32) + bias_ref[t]
            s_ref[...] = s
            bm = jnp.max(s, axis=0, keepdims=True)
            m = bm if m is None else jnp.maximum(m, bm)
        acc = None
        den = None
        for (c0, t), s_ref in zip(blocks, refs):
            p = jnp.exp2(s_ref[...] - m)
            ps = jnp.sum(p, axis=0, keepdims=True)
            pv = jnp.dot(vt_ref[:, pl.ds(c0, tq)], p.astype(BF16), preferred_element_type=F32)
            den = ps if den is None else den + ps
            acc = pv if acc is None else acc + pv
        finish(r0, acc, den)

    def tiles(idx, n_prev):
        mins = [bounded_tile(i, n_prev) for i in idx]
        for i, l_min in zip(idx, mins):
            @pl.when(jnp.logical_not(l_min > L_MIN_A))
            def _():
                exact_tile(i, s_refs, n_prev)

    for i in range(min(max_prev, nq)):
        tiles([i], i)

    def step(it, carry):
        tiles([max_prev + UNROLL_B * it + u for u in range(UNROLL_B)], max_prev)
        return carry

    n_rest = max(nq - max_prev, 0)
    lax.fori_loop(0, n_rest // UNROLL_B, step, 0)
    for i in range(nq - n_rest % UNROLL_B, nq):
        tiles([i], max_prev)


def _attn_b(proj, bias, bmax, bsz, seq):
    tq = TQ_B
    col = lambda base: pl.BlockSpec((None, seq, LANES), lambda b, h: (base + h, b, 0))
    return pl.pallas_call(
        functools.partial(_attn_b_kernel, tq=tq),
        grid=(bsz, B_HEADS // 2),
        in_specs=[pl.BlockSpec((None, 1, LANES), lambda b, h: (h, 0, 0)),
                  pl.BlockSpec((None, B_PAD // tq + 1, tq, 2 * tq), lambda b, h: (h, 0, 0, 0)),
                  col(BQ0), col(BK0), col(BV0)],
        out_specs=pl.BlockSpec((None, seq, LANES), lambda b, h: (h, b, 0)),
        out_shape=jax.ShapeDtypeStruct((B_HEADS // 2, bsz * seq, LANES), BF16),
        scratch_shapes=[pltpu.VMEM((LANES, seq), BF16)]
        + [pltpu.VMEM((tq, 2 * tq), F32) for _ in range(B_PAD // tq + 1)],
        compiler_params=_cparams("parallel", "parallel"),
        name="attn_b",
    )(bmax, bias, proj, proj, proj)


def _attn_c_kernel(q_ref, k_ref, v_ref, o_ref, *, tq, group):
    seq = q_ref.shape[0]
    nq = seq // tq
    row = lax.broadcasted_iota(jnp.int32, (tq, tq), 0)
    colm = lax.broadcasted_iota(jnp.int32, (tq, tq), 1)
    tri_strict = (row > colm).astype(BF16)
    before = colm < row

    def block(i, j, carry, acc, diag):
        qb = q_ref[pl.ds(pl.multiple_of(i * tq, tq), tq), :]
        c0 = pl.multiple_of(j * tq, tq)
        kb = k_ref[pl.ds(c0, tq), :]
        vb = v_ref[pl.ds(c0, tq), :]
        z = lax.dot_general(qb, kb, (((1,), (1,)), ((), ())), preferred_element_type=F32)
        lm = -(jnp.maximum(z, 0.0) + jnp.log2(1.0 + jnp.exp2(-jnp.abs(z))))
        if diag:
            lm = jnp.where(before, lm, 0.0)
        lm_b = lm.astype(BF16)
        excl = jnp.dot(lm_b, tri_strict, preferred_element_type=F32)
        logw = (z + lm) + excl + carry
        wgt = jnp.exp2(logw)
        if diag:
            wgt = jnp.where(before, wgt, 0.0)
        acc = acc + jnp.dot(wgt.astype(BF16), vb, preferred_element_type=F32)
        carry = carry + excl[:, 0:1] + lm_b[:, 0:1].astype(F32)
        return carry, acc

    def qgroup(gi, c):
        i0 = gi * group
        sts = [block(i0 + g, i0 + g, jnp.zeros((tq, 1), F32), jnp.zeros((tq, LANES), F32), True)
               for g in range(group)]
        carries = tuple(st[0] for st in sts)
        accs = tuple(st[1] for st in sts)

        def live(carries):
            return functools.reduce(jnp.maximum, [jnp.max(cr) for cr in carries])

        def cond(st):
            t, cmax, _, _ = st
            return jnp.logical_and(t <= i0 + group - 1, cmax > C_SKIP_LOG2)

        def body(st):
            t, _, carries, accs = st
            new = []
            for g in range(group):
                j = i0 + g - t
                cin = jnp.where(j >= 0, carries[g], NEG_INF)
                new.append(block(i0 + g, jnp.maximum(j, 0), cin, accs[g], False))
            carries = tuple(st_[0] for st_ in new)
            return t + 1, live(carries), carries, tuple(st_[1] for st_ in new)

        _, _, _, accs = lax.while_loop(cond, body, (jnp.int32(1), live(carries), carries, accs))
        for g in range(group):
            o_ref[pl.ds(pl.multiple_of((i0 + g) * tq, tq), tq), :] = accs[g].astype(o_ref.dtype)
        return c

    lax.fori_loop(0, nq // group, qgroup, 0)


def _attn_c(proj, bsz, seq):
    tq = min(TQ_C, seq)
    group = min(GROUP_C, seq // tq)
    assert (seq // tq) % group == 0
    col = lambda base: pl.BlockSpec((None, seq, LANES), lambda b, h: (base + h, b, 0))
    return pl.pallas_call(
        functools.partial(_attn_c_kernel, tq=tq, group=group),
        grid=(bsz, C_HEADS),
        in_specs=[col(CQ0), col(CK0), col(CV0)],
        out_specs=pl.BlockSpec((None, seq, LANES), lambda b, h: (h, b, 0)),
        out_shape=jax.ShapeDtypeStruct((C_HEADS, bsz * seq, LANES), BF16),
        compiler_params=_cparams("parallel", "parallel"),
        name="attn_c",
    )(proj, proj, proj)


def _cat_lanes(ref, start, count):
    return jnp.concatenate([ref[start + c] for c in range(count)], axis=1)


def _merge_kernel(x_ref, mod_ref, gate_ref, bg_ref, ya_ref, yb_ref, yc_ref, wb_ref, wo_ref, o_ref):
    blks = D_MODEL // LANES
    merged = None
    for r, y_ref in enumerate((ya_ref, yb_ref, yc_ref)):
        y = _cat_lanes(y_ref, 0, BRANCH_WIDTH // LANES)
        br = jnp.dot(y, wb_ref[r], preferred_element_type=F32)
        pre = _cat_lanes(gate_ref, r * blks, blks).astype(F32) + bg_ref[:, r * D_MODEL:(r + 1) * D_MODEL]
        term = jax.nn.sigmoid(pre) * br
        merged = term if merged is None else merged + term
    out = jnp.dot(merged.astype(BF16), wo_ref[...], preferred_element_type=F32)
    o_ref[...] = x_ref[...] + mod_ref[2:3, :] * out


def _merge(x2d, mod_l, proj, b_gate, ya, yb, yc, w_branch, w_out, seq):
    t, d = x2d.shape
    tm = min(TM_MERGE, seq)
    ybs = lambda: pl.BlockSpec((BRANCH_WIDTH // LANES, tm, LANES), lambda i: (0, i, 0))
    return pl.pallas_call(
        _merge_kernel,
        grid=(t // tm,),
        in_specs=[
            pl.BlockSpec((tm, d), lambda i: (i, 0)),
            pl.BlockSpec((None, 6, d), lambda i: ((i * tm) // seq, 0, 0)),
            pl.BlockSpec((GATE_BLKS, tm, LANES), lambda i: (0, i, 0)),
            pl.BlockSpec((1, GATE_COLS), lambda i: (0, 0)),
            ybs(), ybs(), ybs(),
            pl.BlockSpec((N_BRANCH, BRANCH_WIDTH, d), lambda i: (0, 0, 0)),
            pl.BlockSpec((d, d), lambda i: (0, 0)),
        ],
        out_specs=pl.BlockSpec((tm, d), lambda i: (i, 0)),
        out_shape=jax.ShapeDtypeStruct((t, d), F32),
        compiler_params=_cparams("parallel"),
        name="merge",
    )(x2d, mod_l, proj, b_gate, ya, yb, yc, w_branch, w_out)


def _ffn_kernel(x_ref, mod_ref, g_ref, fg_ref, w13_ref, w2_ref, o_ref, *, final):
    dff = w2_ref.shape[0]
    x = x_ref[...]
    h = _norm_mod(x, g_ref[...], mod_ref[3:4, :], mod_ref[4:5, :]).astype(BF16)
    acc = None
    for f0, f1 in FFN_CHUNKS:
        u_gate = jnp.dot(h, w13_ref[:, f0:f1], preferred_element_type=F32)
        u_up = jnp.dot(h, w13_ref[:, dff + f0:dff + f1], preferred_element_type=F32)
        act = (u_gate * jax.nn.sigmoid(u_gate) * u_up).astype(BF16)
        part = jnp.dot(act, w2_ref[f0:f1, :], preferred_element_type=F32)
        acc = part if acc is None else acc + part
    y = x + mod_ref[5:6, :] * acc
    if final:
        y = y * lax.rsqrt(jnp.mean(y * y, axis=-1, keepdims=True) + EPS) * fg_ref[...]
    o_ref[...] = y


def _ffn(x2d, mod_l, g, final_g, w13, w2, seq, final):
    t, d = x2d.shape
    dff = w2.shape[0]
    assert FFN_CHUNKS[0][0] == 0 and FFN_CHUNKS[-1][1] == dff
    tm = min(TM_FFN, seq)
    resident = lambda shape: pl.BlockSpec(shape, lambda i: (0, 0), pipeline_mode=pl.Buffered(1))
    return pl.pallas_call(
        functools.partial(_ffn_kernel, final=final),
        grid=(t // tm,),
        in_specs=[
            pl.BlockSpec((tm, d), lambda i: (i, 0)),
            pl.BlockSpec((None, 6, d), lambda i: ((i * tm) // seq, 0, 0)),
            pl.BlockSpec((1, d), lambda i: (0, 0)),
            pl.BlockSpec((1, d), lambda i: (0, 0)),
            resident((d, 2 * dff)),
            resident((dff, d)),
        ],
        out_specs=pl.BlockSpec((tm, d), lambda i: (i, 0)),
        out_shape=jax.ShapeDtypeStruct((t, d), F32),
        compiler_params=_cparams("parallel"),
        name="ffn",
    )(x2d, mod_l, g, final_g, w13, w2)


def _prep_w_in(w_in):
    scale = np.ones((IN_COLS,), np.float32)
    scale[0 * BRANCH_WIDTH:1 * BRANCH_WIDTH] = HEAD_DIM ** -0.5 * LOG2E
    scale[3 * BRANCH_WIDTH:4 * BRANCH_WIDTH] = HEAD_DIM ** -0.5 * LOG2E
    scale[6 * BRANCH_WIDTH:7 * BRANCH_WIDTH] = C_HEAD_DIM ** -0.5 * LOG2E
    w = w_in * jnp.asarray(scale)
    return jnp.concatenate([w[..., QKV_COLS:], w[..., :QKV_COLS]], axis=-1).astype(BF16)


def kernel(x, c, w_ada, b_ada, norm1_g, w_in, b_gate, lam_q1, lam_k1, lam_q2, lam_k2, subln_g,
           t5_table, rel_bias_b, w_branch, w_out, norm2_g, w13, w2, final_g):
    bsz, seq, d = x.shape
    depth = w_in.shape[0]
    t = bsz * seq
    assert d == D_MODEL and w_in.shape[2] == IN_COLS and w2.shape[1] == D_FF
    assert seq % (2 * B_PAD) == 0, "sequence tiles assume a multiple of 1024 frames"

    mod = _ada_mod(c, w_ada, b_ada).reshape(depth, bsz, 6, d)
    w_in_b = _prep_w_in(w_in)
    w_branch_b = w_branch.astype(BF16)
    w_out_b = w_out.astype(BF16)
    w13_b = w13.astype(BF16)
    w2_b = w2.astype(BF16)
    tq_a = min(TQ_A, seq // 2)
    a_bias, a_bmax = _a_bias_tiles(t5_table, tq_a, min(TK_A, tq_a))
    fg = final_g.reshape(1, d)

    x2d = x.reshape(t, d)
    for l in range(depth):
        lam_init = 0.8 - 0.6 * math.exp(-0.3 * l)
        proj = _inproj(x2d, mod[l], norm1_g[l].reshape(1, d), w_in_b[l], seq)
        ya = _attn_a(proj, a_bias, a_bmax, lam_q1[l].reshape(1, -1), lam_k1[l].reshape(1, -1),
                     lam_q2[l].reshape(1, -1), lam_k2[l].reshape(1, -1), subln_g[l].reshape(-1, 1),
                     bsz, seq, lam_init)
        yb = _attn_b(proj, *_b_bias_tiles(rel_bias_b[l], TQ_B), bsz, seq)
        yc = _attn_c(proj, bsz, seq)
        x2d = _merge(x2d, mod[l], proj, b_gate[l].reshape(1, -1), ya, yb, yc,
                     w_branch_b[l], w_out_b[l], seq)
        x2d = _ffn(x2d, mod[l], norm2_g[l].reshape(1, d), fg, w13_b[l], w2_b[l], seq,
                   final=(l == depth - 1))
    return x2d.reshape(bsz, seq, d)
```

```python
import functools
import math

import numpy as np
import jax
import jax.numpy as jnp
from jax import lax
from jax.experimental import pallas as pl
from jax.experimental.pallas import tpu as pltpu

F32 = jnp.float32
BF16 = jnp.bfloat16

D_MODEL = 1024
DEPTH = 4
CHUNK = 64
HEAD_DIM = 64
BRANCH_WIDTH = 512
N_BRANCH = 3
A_HEADS = 4
B_HEADS = 8
C_HEADS = 4
C_HEAD_DIM = 128
B_LEFT_CHUNKS = 8
B_PAD = B_LEFT_CHUNKS * CHUNK
B_MAX_REL = 128
T5_BUCKETS = 32
T5_MAX_DIST = 128
D_FF = 2816
QKV_COLS = 3 * N_BRANCH * BRANCH_WIDTH
GATE_COLS = N_BRANCH * D_MODEL
IN_COLS = QKV_COLS + GATE_COLS
NEG_INF = -1e30
EPS = 1e-6
LOG2E = math.log2(math.e)

LANES = 128
N_COLBLK = IN_COLS // LANES
GATE_BLKS = GATE_COLS // LANES
AQ0, AK0, AV0 = GATE_BLKS, GATE_BLKS + 4, GATE_BLKS + 8
BQ0, BK0, BV0 = GATE_BLKS + 12, GATE_BLKS + 16, GATE_BLKS + 20
CQ0, CK0, CV0 = GATE_BLKS + 24, GATE_BLKS + 28, GATE_BLKS + 32

TM_PROJ = 512
TN_PROJ = 1536
TM_MERGE = 512
TM_FFN = 512
FFN_CHUNKS = ((0, 1536), (1536, 2816))
TQ_A = 1024
TK_A = 512
TP_A = 512
TC_A = 1024
UNROLL_A = 2
NORM_SLACK_A = 1.02
L_MIN_A = 2.0 ** -100
TQ_B = 512
UNROLL_B = 3
TQ_C = 256
GROUP_C = 8
C_SKIP_LOG2 = -150.0
VMEM_LIMIT = 56 * 1024 * 1024


def _cparams(*sem, flags=None):
    return pltpu.CompilerParams(dimension_semantics=sem, vmem_limit_bytes=VMEM_LIMIT, flags=flags)


def _ada_kernel(c_ref, w_ref, b_ref, o_ref):
    c = c_ref[...]
    cs = c * jax.nn.sigmoid(c)
    o_ref[...] = jnp.dot(cs, w_ref[...], preferred_element_type=F32,
                         precision=lax.Precision.HIGHEST) + b_ref[...]


def _ada_mod(c, w_ada, b_ada):
    depth, d, e = w_ada.shape
    bsz = c.shape[0]
    nblk = e // d
    return pl.pallas_call(
        _ada_kernel,
        grid=(depth, nblk),
        in_specs=[
            pl.BlockSpec((bsz, d), lambda l, j: (0, 0)),
            pl.BlockSpec((None, d, d), lambda l, j: (l, 0, j)),
            pl.BlockSpec((None, 1, d), lambda l, j: (l, 0, j)),
        ],
        out_specs=pl.BlockSpec((None, bsz, d), lambda l, j: (l, 0, j)),
        out_shape=jax.ShapeDtypeStruct((depth, bsz, e), F32),
        compiler_params=_cparams("arbitrary", "arbitrary"),
        name="ada_mod",
    )(c, w_ada, b_ada.reshape(depth, 1, e))


def _norm_mod(x, g, shift, scale):
    ms = jnp.mean(x * x, axis=-1, keepdims=True)
    y = x * lax.rsqrt(ms + EPS) * g
    return y * (1.0 + scale) + shift


def _inproj_kernel(x_ref, mod_ref, g_ref, w_ref, o_ref):
    h = _norm_mod(x_ref[...], g_ref[...], mod_ref[0:1, :], mod_ref[1:2, :]).astype(BF16)
    n = w_ref.shape[1]
    for c0 in range(0, n, TN_PROJ):
        res = jnp.dot(h, w_ref[:, c0:c0 + TN_PROJ], preferred_element_type=F32)
        for cb in range(TN_PROJ // LANES):
            o_ref[c0 // LANES + cb] = res[:, cb * LANES:(cb + 1) * LANES].astype(o_ref.dtype)


def _inproj(x2d, mod, g, w, layer, seq):
    t, d = x2d.shape
    n = w.shape[2]
    assert n % TN_PROJ == 0
    tm = min(TM_PROJ, seq)
    return pl.pallas_call(
        _inproj_kernel,
        grid=(t // tm,),
        in_specs=[
            pl.BlockSpec((tm, d), lambda i: (i, 0)),
            pl.BlockSpec((None, None, 6, d), lambda i: (layer, (i * tm) // seq, 0, 0)),
            pl.BlockSpec((None, 1, d), lambda i: (layer, 0, 0)),
            pl.BlockSpec((None, d, n), lambda i: (layer, 0, 0), pipeline_mode=pl.Buffered(1)),
        ],
        out_specs=pl.BlockSpec((n // LANES, tm, LANES), lambda i: (0, i, 0)),
        out_shape=jax.ShapeDtypeStruct((n // LANES, t, LANES), BF16),
        compiler_params=_cparams("parallel"),
        name="in_proj",
    )(x2d, mod, g, w)


def _t5_bucket_np(rel):
    nb = T5_BUCKETS // 2
    max_exact = nb // 2
    ret = np.where(rel > 0, nb, 0)
    n = np.abs(rel)
    nf = np.maximum(n, 1).astype(np.float32)
    scaled = (np.log(nf / np.float32(max_exact)) / np.float32(math.log(T5_MAX_DIST / max_exact))
              * np.float32(nb - max_exact))
    large = max_exact + scaled.astype(np.int32)
    large = np.minimum(large, nb - 1)
    return (ret + np.where(n < max_exact, n, large)).astype(np.int32)


def _toeplitz(vec, rows, cols):
    length = rows + cols - 1
    assert vec.shape[-1] == length
    lead = vec.shape[:-1]
    ext = jnp.concatenate([vec, jnp.zeros(lead + (1,), vec.dtype)], axis=-1)
    flat = jnp.broadcast_to(ext[..., None, :], lead + (rows, length + 1)).reshape(lead + (rows * (length + 1),))
    return flat[..., :rows * length].reshape(lead + (rows, length))[..., rows - 1:rows - 1 + cols]


def _a_bias_tiles(t5_table, tq, tk):
    r = np.arange(tq)[None, :]
    c = np.arange(tk)[:, None]
    far_bucket = _t5_bucket_np((c - 2 * tk) - r)
    assert (far_bucket == far_bucket[0, 0]).all() and far_bucket[0, 0] == _t5_bucket_np(np.array(-10 * tq))
    far = t5_table[int(far_bucket[0, 0])].astype(F32)
    tiles = [jnp.zeros((A_HEADS, tk, tq), F32)]
    bmax = jnp.zeros((A_HEADS,), F32)
    u = np.arange(tk + tq - 1)
    for koff in range(-tk, tq, tk):
        vec = t5_table[_t5_bucket_np(koff + tk - 1 - u)].astype(F32).T
        vec = (vec - far[:, None]) * LOG2E
        bmax = jnp.maximum(bmax, jnp.max(vec, axis=1))
        bias = _toeplitz(vec, tk, tq)
        allowed = ((c + koff) // CHUNK) <= (r // CHUNK)
        tiles.append(jnp.where(jnp.asarray(allowed)[None], bias, NEG_INF))
    tiles.append(jnp.full((A_HEADS, tk, tq), NEG_INF, F32))
    return jnp.stack(tiles, axis=1), jnp.broadcast_to(bmax[:, None, None], (A_HEADS, 1, LANES))


def _attn_a_kernel(lq1_ref, lk1_ref, lq2_ref, lk2_ref, subg_ref, bmax_ref, bias_ref, q_ref, k_ref, v_ref,
                   o_ref, vt_ref, acc_ref, *s_refs, tq, tk, lam_init):
    seq = q_ref.shape[0]
    nq = seq // tq
    nkb = seq // tk
    per_q = tq // tk
    unroll = len(s_refs)
    tp = TP_A
    tc = TC_A
    lam = (jnp.exp(jnp.sum(lq1_ref[...] * lk1_ref[...], axis=-1, keepdims=True))
           - jnp.exp(jnp.sum(lq2_ref[...] * lk2_ref[...], axis=-1, keepdims=True)) + lam_init)
    lane = lax.broadcasted_iota(jnp.int32, (1, LANES), 1)
    mask1 = (lane < HEAD_DIM).astype(BF16)
    mask2 = (lane >= HEAD_DIM).astype(BF16)
    subg = subg_ref[...] * (1.0 - lam_init)
    sel = (lax.broadcasted_iota(jnp.int32, (8, LANES), 0)
           == lax.broadcasted_iota(jnp.int32, (8, LANES), 1) // HEAD_DIM).astype(BF16)

    def prepare(b, kn2):
        c0 = pl.multiple_of(b * tk, tk)
        vt_ref[:, pl.ds(c0, tk)] = jnp.transpose(v_ref[pl.ds(c0, tk), :])
        kb = k_ref[pl.ds(c0, tk), :]
        blk = lax.dot_general(sel, kb * kb, (((1,), (1,)), ((), ())), preferred_element_type=F32)
        return jnp.maximum(kn2, blk)

    kn2 = lax.fori_loop(0, nkb, prepare, jnp.zeros((8, tk), F32))
    kmax2 = jnp.max(kn2, axis=1, keepdims=True) * NORM_SLACK_A
    kmax = jnp.sqrt(jnp.concatenate([jnp.broadcast_to(kmax2[0:1], (1, tq)),
                                     jnp.broadcast_to(kmax2[1:2], (1, tq))], axis=1))

    def finish(i, acc, l):
        o = acc * (1.0 / l)
        d = o[:, :tq] - lam * o[:, tq:]
        y = d * lax.rsqrt(jnp.mean(d * d, axis=0, keepdims=True) + EPS) * subg
        o_ref[pl.ds(pl.multiple_of(i * tq, tq), tq), :] = jnp.transpose(y).astype(o_ref.dtype)

    def load_q(i):
        qb = q_ref[pl.ds(pl.multiple_of(i * tq, tq), tq), :]
        return jnp.concatenate([qb * mask1, qb * mask2], axis=0)

    def steps(i, step, st):
        n_plain = jnp.maximum(per_q * i - 1, 0) // unroll
        n_steps = (per_q * (i + 1) + unroll - 1) // unroll
        st = lax.fori_loop(0, n_plain, lambda it, s_: step(it, s_, False), st)
        return lax.fori_loop(n_plain, n_steps, lambda it, s_: step(it, s_, True), st)

    def bounded_pass(i):
        qq = load_q(i)
        qn2 = lax.dot_general(jnp.ones((8, LANES), BF16), qq * qq, (((1,), (1,)), ((), ())),
                              preferred_element_type=F32)[0:1]
        shift = jnp.sqrt(qn2 * NORM_SLACK_A) * kmax + (bmax_ref[:, 0:1] + 1.0)
        acc_ref[...] = jnp.zeros(acc_ref.shape, F32)

        def block(js, l, with_bias):
            c0 = pl.multiple_of(jnp.minimum(js, nkb - 1) * tk, tk)
            kb = k_ref[pl.ds(c0, tk), :]
            vt = vt_ref[:, pl.ds(c0, tk)]
            if with_bias:
                bias = bias_ref[jnp.clip(js - per_q * i + 2, 0, per_q + 2)]
            sums = []
            for ct in range(2 * tq // tc):
                cols = slice(ct * tc, (ct + 1) * tc)
                s = lax.dot_general(kb, qq[cols], (((1,), (1,)), ((), ())), preferred_element_type=F32)
                if with_bias:
                    s = s + jnp.concatenate([bias, bias], axis=1)[:, cols]
                p = jnp.exp2(s - shift[:, cols])
                sums.append(jnp.sum(p, axis=0, keepdims=True))
                acc_ref[:, cols] += jnp.dot(vt, p.astype(BF16), preferred_element_type=F32)
            return l + jnp.concatenate(sums, axis=1)

        def step(it, l, with_bias):
            for u in range(unroll):
                l = block(it * unroll + u, l, with_bias)
            return l

        l = steps(i, step, jnp.zeros((1, 2 * tq), F32))
        finish(i, acc_ref[...], l)
        return jnp.min(l)

    def running_max_pass(i):
        qq = load_q(i)
        acc_ref[...] = jnp.zeros(acc_ref.shape, F32)

        def block(js, st, with_bias, s_ref):
            m, l = st
            c0 = pl.multiple_of(jnp.minimum(js, nkb - 1) * tk, tk)
            kb = k_ref[pl.ds(c0, tk), :]
            s = lax.dot_general(kb, qq, (((1,), (1,)), ((), ())), preferred_element_type=F32)
            if with_bias:
                bias = bias_ref[jnp.clip(js - per_q * i + 2, 0, per_q + 2)]
                s = s + jnp.concatenate([bias, bias], axis=1)
            s_ref[...] = s
            m_new = jnp.maximum(m, jnp.max(s, axis=0, keepdims=True))
            alpha = jnp.exp2(m - m_new)
            sums = []
            for ct in range(2 * tq // tp):
                cols = slice(ct * tp, (ct + 1) * tp)
                p = jnp.exp2(s_ref[:, cols] - m_new[:, cols])
                sums.append(jnp.sum(p, axis=0, keepdims=True))
                vt = vt_ref[:, pl.ds(c0, tk)]
                acc_ref[:, cols] = (alpha[:, cols] * acc_ref[:, cols]
                                    + jnp.dot(vt, p.astype(BF16), preferred_element_type=F32))
            return m_new, alpha * l + jnp.concatenate(sums, axis=1)

        def step(it, st, with_bias):
            for u in range(unroll):
                st = block(it * unroll + u, st, with_bias, s_refs[u])
            return st

        _, l = steps(i, step, (jnp.full((1, 2 * tq), NEG_INF, F32), jnp.zeros((1, 2 * tq), F32)))
        finish(i, acc_ref[...], l)

    def qtile(i, carry):
        l_min = bounded_pass(i)

        @pl.when(jnp.logical_not(l_min > L_MIN_A))
        def _():
            running_max_pass(i)

        return carry

    lax.fori_loop(0, nq, qtile, 0)


def _attn_a(proj, bias, bmax, lq1, lk1, lq2, lk2, subg, layer, bsz, seq, lam_init):
    tq = min(TQ_A, seq // 2)
    tk = min(TK_A, tq)
    assert tk == TP_A
    vec = lambda n: pl.BlockSpec((None, 1, n), lambda b, h: (layer, 0, 0))
    col = lambda base: pl.BlockSpec((None, seq, LANES), lambda b, h: (base + h, b, 0))
    return pl.pallas_call(
        functools.partial(_attn_a_kernel, tq=tq, tk=tk, lam_init=lam_init),
        grid=(bsz, A_HEADS),
        in_specs=[vec(HEAD_DIM), vec(HEAD_DIM), vec(HEAD_DIM), vec(HEAD_DIM),
                  pl.BlockSpec((None, 2 * HEAD_DIM, 1), lambda b, h: (layer, 0, 0)),
                  pl.BlockSpec((None, 1, LANES), lambda b, h: (h, 0, 0)),
                  pl.BlockSpec((None, tq // tk + 3, tk, tq), lambda b, h: (h, 0, 0, 0)),
                  col(AQ0), col(AK0), col(AV0)],
        out_specs=pl.BlockSpec((None, seq, LANES), lambda b, h: (h, b, 0)),
        out_shape=jax.ShapeDtypeStruct((A_HEADS, bsz * seq, LANES), BF16),
        scratch_shapes=[pltpu.VMEM((LANES, seq), BF16), pltpu.VMEM((LANES, 2 * tq), F32)]
        + [pltpu.VMEM((tk, 2 * tq), F32) for _ in range(UNROLL_A)],
        compiler_params=_cparams("parallel", "parallel"),
        name="attn_a",
    )(lq1, lk1, lq2, lk2, subg, bmax, bias, proj, proj, proj)


def _b_bias_tiles(rel_bias, tq):
    depth = rel_bias.shape[0]
    assert B_PAD % tq == 0
    r = np.arange(tq)[None, :]
    c = np.arange(tq)[:, None]
    lo = CHUNK * (r // CHUNK)
    u = np.arange(2 * tq - 1)
    tiles = []
    for koff in range(-B_PAD, tq, tq):
        in_band = (c + koff >= lo - B_PAD) & (c + koff < lo + CHUNK)
        idx = np.clip(u - (tq - 1) - koff, -B_MAX_REL, B_MAX_REL) + B_MAX_REL
        bias = _toeplitz(rel_bias[..., idx].astype(F32) * LOG2E, tq, tq)
        bias = jnp.where(jnp.asarray(in_band), bias, NEG_INF)
        tiles.append(bias.reshape(depth, B_HEADS // 2, 2, tq, tq).transpose(0, 1, 3, 2, 4)
                     .reshape(depth, B_HEADS // 2, tq, 2 * tq))
    bmax = jnp.max((rel_bias.astype(F32) * LOG2E).reshape(depth, B_HEADS // 2, -1), axis=2)
    return (jnp.stack(tiles, axis=2),
            jnp.broadcast_to(bmax[:, :, None, None], (depth, B_HEADS // 2, 1, LANES)))


def _attn_b_kernel(bmax_ref, bias_ref, q_ref, k_ref, v_ref, o_ref, vt_ref, *s_refs, tq):
    seq = q_ref.shape[0]
    nq = seq // tq
    max_prev = B_PAD // tq
    lane = lax.broadcasted_iota(jnp.int32, (1, LANES), 1)
    mask1 = (lane < HEAD_DIM).astype(BF16)
    mask2 = (lane >= HEAD_DIM).astype(BF16)
    feat = lax.broadcasted_iota(jnp.int32, (LANES, tq), 0)
    sel = (lax.broadcasted_iota(jnp.int32, (8, LANES), 0)
           == lax.broadcasted_iota(jnp.int32, (8, LANES), 1) // HEAD_DIM).astype(BF16)

    def prepare(b, kn2):
        c0 = pl.multiple_of(b * tq, tq)
        vt_ref[:, pl.ds(c0, tq)] = jnp.transpose(v_ref[pl.ds(c0, tq), :])
        kb = k_ref[pl.ds(c0, tq), :]
        blk = lax.dot_general(sel, kb * kb, (((1,), (1,)), ((), ())), preferred_element_type=F32)
        return jnp.maximum(kn2, blk)

    kn2 = lax.fori_loop(0, nq, prepare, jnp.zeros((8, tq), F32))
    kmax2 = jnp.max(kn2, axis=1, keepdims=True) * NORM_SLACK_A
    kmax = jnp.sqrt(jnp.concatenate([jnp.broadcast_to(kmax2[0:1], (1, tq)),
                                     jnp.broadcast_to(kmax2[1:2], (1, tq))], axis=1))

    def tile_blocks(i, n_prev):
        static = isinstance(i, int)
        r0 = i * tq if static else pl.multiple_of(i * tq, tq)
        qb = q_ref[pl.ds(r0, tq), :]
        qq = jnp.concatenate([qb * mask1, qb * mask2], axis=0)
        starts = [r0 - d * tq for d in range(n_prev, 0, -1)] + [r0]
        blocks = tuple((c0 if static else pl.multiple_of(c0, tq), max_prev - n_prev + j)
                       for j, c0 in enumerate(starts))
        return r0, qq, blocks

    def finish(r0, acc, den):
        o = acc * (1.0 / den)
        o = jnp.where(feat < HEAD_DIM, o[:, :tq], o[:, tq:])
        o_ref[pl.ds(r0, tq), :] = jnp.transpose(o).astype(o_ref.dtype)

    def bounded_tile(i, n_prev):
        r0, qq, blocks = tile_blocks(i, n_prev)
        qn2 = lax.dot_general(jnp.ones((8, LANES), BF16), qq * qq, (((1,), (1,)), ((), ())),
                              preferred_element_type=F32)[0:1]
        shift = jnp.sqrt(qn2 * NORM_SLACK_A) * kmax + (bmax_ref[:, 0:1] + 1.0)
        acc = None
        den = None
        for c0, t in blocks:
            s = lax.dot_general(k_ref[pl.ds(c0, tq), :], qq, (((1,), (1,)), ((), ())),
                                preferred_element_type=F32) + bias_ref[t]
            p = jnp.exp2(s - shift)
            ps = jnp.sum(p, axis=0, keepdims=True)
            pv = jnp.dot(vt_ref[:, pl.ds(c0, tq)], p.astype(BF16), preferred_element_type=F32)
            den = ps if den is None else den + ps
            acc = pv if acc is None else acc + pv
        finish(r0, acc, den)
        return jnp.min(den)

    def exact_tile(i, refs, n_prev):
        r0, qq, blocks = tile_blocks(i, n_prev)
        m = None
        for (c0, t), s_ref in zip(blocks, refs):
            s = lax.dot_general(k_ref[pl.ds(c0, tq), :], qq, (((1,), (1,)), ((), ())),
                                preferred_element_type=F32) + bias_ref[t]
            s_ref[...] = s
            bm = jnp.max(s, axis=0, keepdims=True)
            m = bm if m is None else jnp.maximum(m, bm)
        acc = None
        den = None
        for (c0, t), s_ref in zip(blocks, refs):
            p = jnp.exp2(s_ref[...] - m)
            ps = jnp.sum(p, axis=0, keepdims=True)
            pv = jnp.dot(vt_ref[:, pl.ds(c0, tq)], p.astype(BF16), preferred_element_type=F32)
            den = ps if den is None else den + ps
            acc = pv if acc is None else acc + pv
        finish(r0, acc, den)

    def tiles(idx, n_prev):
        mins = [bounded_tile(i, n_prev) for i in idx]
        for i, l_min in zip(idx, mins):
            @pl.when(jnp.logical_not(l_min > L_MIN_A))
            def _():
                exact_tile(i, s_refs, n_prev)

    for i in range(min(max_prev, nq)):
        tiles([i], i)

    def step(it, carry):
        tiles([max_prev + UNROLL_B * it + u for u in range(UNROLL_B)], max_prev)
        return carry

    n_rest = max(nq - max_prev, 0)
    lax.fori_loop(0, n_rest // UNROLL_B, step, 0)
    for i in range(nq - n_rest % UNROLL_B, nq):
        tiles([i], max_prev)


def _attn_b(proj, bias, bmax, layer, bsz, seq):
    tq = TQ_B
    col = lambda base: pl.BlockSpec((None, seq, LANES), lambda b, h: (base + h, b, 0))
    return pl.pallas_call(
        functools.partial(_attn_b_kernel, tq=tq),
        grid=(bsz, B_HEADS // 2),
        in_specs=[pl.BlockSpec((None, None, 1, LANES), lambda b, h: (layer, h, 0, 0)),
                  pl.BlockSpec((None, None, B_PAD // tq + 1, tq, 2 * tq), lambda b, h: (layer, h, 0, 0, 0)),
                  col(BQ0), col(BK0), col(BV0)],
        out_specs=pl.BlockSpec((None, seq, LANES), lambda b, h: (h, b, 0)),
        out_shape=jax.ShapeDtypeStruct((B_HEADS // 2, bsz * seq, LANES), BF16),
        scratch_shapes=[pltpu.VMEM((LANES, seq), BF16)]
        + [pltpu.VMEM((tq, 2 * tq), F32) for _ in range(B_PAD // tq + 1)],
        compiler_params=_cparams("parallel", "parallel"),
        name="attn_b",
    )(bmax, bias, proj, proj, proj)


def _attn_c_kernel(q_ref, k_ref, v_ref, o_ref, *, tq, group):
    seq = q_ref.shape[0]
    nq = seq // tq
    row = lax.broadcasted_iota(jnp.int32, (tq, tq), 0)
    colm = lax.broadcasted_iota(jnp.int32, (tq, tq), 1)
    tri_strict = (row > colm).astype(BF16)
    before = colm < row

    def block(i, j, carry, acc, diag):
        qb = q_ref[pl.ds(pl.multiple_of(i * tq, tq), tq), :]
        c0 = pl.multiple_of(j * tq, tq)
        kb = k_ref[pl.ds(c0, tq), :]
        vb = v_ref[pl.ds(c0, tq), :]
        z = lax.dot_general(qb, kb, (((1,), (1,)), ((), ())), preferred_element_type=F32)
        lm = -(jnp.maximum(z, 0.0) + jnp.log2(1.0 + jnp.exp2(-jnp.abs(z))))
        if diag:
            lm = jnp.where(before, lm, 0.0)
        lm_b = lm.astype(BF16)
        excl = jnp.dot(lm_b, tri_strict, preferred_element_type=F32)
        logw = (z + lm) + excl + carry
        wgt = jnp.exp2(logw)
        if diag:
            wgt = jnp.where(before, wgt, 0.0)
        acc = acc + jnp.dot(wgt.astype(BF16), vb, preferred_element_type=F32)
        carry = carry + excl[:, 0:1] + lm_b[:, 0:1].astype(F32)
        return carry, acc

    def qgroup(gi, c):
        i0 = gi * group
        sts = [block(i0 + g, i0 + g, jnp.zeros((tq, 1), F32), jnp.zeros((tq, LANES), F32), True)
               for g in range(group)]
        carries = tuple(st[0] for st in sts)
        accs = tuple(st[1] for st in sts)

        def live(carries):
            return functools.reduce(jnp.maximum, [jnp.max(cr) for cr in carries])

        def cond(st):
            t, cmax, _, _ = st
            return jnp.logical_and(t <= i0 + group - 1, cmax > C_SKIP_LOG2)

        def body(st):
            t, _, carries, accs = st
            new = []
            for g in range(group):
                j = i0 + g - t
                cin = jnp.where(j >= 0, carries[g], NEG_INF)
                new.append(block(i0 + g, jnp.maximum(j, 0), cin, accs[g], False))
            carries = tuple(st_[0] for st_ in new)
            return t + 1, live(carries), carries, tuple(st_[1] for st_ in new)

        _, _, _, accs = lax.while_loop(cond, body, (jnp.int32(1), live(carries), carries, accs))
        for g in range(group):
            o_ref[pl.ds(pl.multiple_of((i0 + g) * tq, tq), tq), :] = accs[g].astype(o_ref.dtype)
        return c

    lax.fori_loop(0, nq // group, qgroup, 0)


def _attn_c(proj, bsz, seq):
    tq = min(TQ_C, seq)
    group = min(GROUP_C, seq // tq)
    assert (seq // tq) % group == 0
    col = lambda base: pl.BlockSpec((None, seq, LANES), lambda b, h: (base + h, b, 0))
    return pl.pallas_call(
        functools.partial(_attn_c_kernel, tq=tq, group=group),
        grid=(bsz, C_HEADS),
        in_specs=[col(CQ0), col(CK0), col(CV0)],
        out_specs=pl.BlockSpec((None, seq, LANES), lambda b, h: (h, b, 0)),
        out_shape=jax.ShapeDtypeStruct((C_HEADS, bsz * seq, LANES), BF16),
        compiler_params=_cparams("parallel", "parallel"),
        name="attn_c",
    )(proj, proj, proj)


def _cat_lanes(ref, start, count):
    return jnp.concatenate([ref[start + c] for c in range(count)], axis=1)


def _merge_kernel(x_ref, mod_ref, gate_ref, bg_ref, ya_ref, yb_ref, yc_ref, wb_ref, wo_ref, o_ref):
    blks = D_MODEL // LANES
    merged = None
    for r, y_ref in enumerate((ya_ref, yb_ref, yc_ref)):
        y = _cat_lanes(y_ref, 0, BRANCH_WIDTH // LANES)
        br = jnp.dot(y, wb_ref[r], preferred_element_type=F32)
        pre = _cat_lanes(gate_ref, r * blks, blks).astype(F32) + bg_ref[:, r * D_MODEL:(r + 1) * D_MODEL]
        term = jax.nn.sigmoid(pre) * br
        merged = term if merged is None else merged + term
    out = jnp.dot(merged.astype(BF16), wo_ref[...], preferred_element_type=F32)
    o_ref[...] = x_ref[...] + mod_ref[2:3, :] * out


def _merge(x2d, mod, proj, b_gate, ya, yb, yc, w_branch, w_out, layer, seq):
    t, d = x2d.shape
    tm = min(TM_MERGE, seq)
    ybs = lambda: pl.BlockSpec((BRANCH_WIDTH // LANES, tm, LANES), lambda i: (0, i, 0))
    return pl.pallas_call(
        _merge_kernel,
        grid=(t // tm,),
        in_specs=[
            pl.BlockSpec((tm, d), lambda i: (i, 0)),
            pl.BlockSpec((None, None, 6, d), lambda i: (layer, (i * tm) // seq, 0, 0)),
            pl.BlockSpec((GATE_BLKS, tm, LANES), lambda i: (0, i, 0)),
            pl.BlockSpec((None, 1, GATE_COLS), lambda i: (layer, 0, 0)),
            ybs(), ybs(), ybs(),
            pl.BlockSpec((None, N_BRANCH, BRANCH_WIDTH, d), lambda i: (layer, 0, 0, 0)),
            pl.BlockSpec((None, d, d), lambda i: (layer, 0, 0)),
        ],
        out_specs=pl.BlockSpec((tm, d), lambda i: (i, 0)),
        out_shape=jax.ShapeDtypeStruct((t, d), F32),
        compiler_params=_cparams("parallel"),
        name="merge",
    )(x2d, mod, proj, b_gate, ya, yb, yc, w_branch, w_out)


def _ffn_kernel(x_ref, mod_ref, g_ref, fg_ref, w13_ref, w2_ref, o_ref, *, final):
    dff = w2_ref.shape[0]
    x = x_ref[...]
    h = _norm_mod(x, g_ref[...], mod_ref[3:4, :], mod_ref[4:5, :]).astype(BF16)
    acc = None
    for f0, f1 in FFN_CHUNKS:
        u_gate = jnp.dot(h, w13_ref[:, f0:f1], preferred_element_type=F32)
        u_up = jnp.dot(h, w13_ref[:, dff + f0:dff + f1], preferred_element_type=F32)
        act = (u_gate * jax.nn.sigmoid(u_gate) * u_up).astype(BF16)
        part = jnp.dot(act, w2_ref[f0:f1, :], preferred_element_type=F32)
        acc = part if acc is None else acc + part
    y = x + mod_ref[5:6, :] * acc
    if final:
        y = y * lax.rsqrt(jnp.mean(y * y, axis=-1, keepdims=True) + EPS) * fg_ref[...]
    o_ref[...] = y


def _ffn(x2d, mod, g, final_g, w13, w2, layer, seq, final):
    t, d = x2d.shape
    dff = w2.shape[1]
    assert FFN_CHUNKS[0][0] == 0 and FFN_CHUNKS[-1][1] == dff
    tm = min(TM_FFN, seq)
    resident = lambda shape: pl.BlockSpec((None,) + shape, lambda i: (layer, 0, 0), pipeline_mode=pl.Buffered(1))
    return pl.pallas_call(
        functools.partial(_ffn_kernel, final=final),
        grid=(t // tm,),
        in_specs=[
            pl.BlockSpec((tm, d), lambda i: (i, 0)),
            pl.BlockSpec((None, None, 6, d), lambda i: (layer, (i * tm) // seq, 0, 0)),
            pl.BlockSpec((None, 1, d), lambda i: (layer, 0, 0)),
            pl.BlockSpec((1, d), lambda i: (0, 0)),
            resident((d, 2 * dff)),
            resident((dff, d)),
        ],
        out_specs=pl.BlockSpec((tm, d), lambda i: (i, 0)),
        out_shape=jax.ShapeDtypeStruct((t, d), F32),
        compiler_params=_cparams("parallel"),
        name="ffn",
    )(x2d, mod, g, final_g, w13, w2)


def _prep_w_in(w_in):
    scale = np.ones((IN_COLS,), np.float32)
    scale[0 * BRANCH_WIDTH:1 * BRANCH_WIDTH] = HEAD_DIM ** -0.5 * LOG2E
    scale[3 * BRANCH_WIDTH:4 * BRANCH_WIDTH] = HEAD_DIM ** -0.5 * LOG2E
    scale[6 * BRANCH_WIDTH:7 * BRANCH_WIDTH] = C_HEAD_DIM ** -0.5 * LOG2E
    w = w_in * jnp.asarray(scale)
    return jnp.concatenate([w[..., QKV_COLS:], w[..., :QKV_COLS]], axis=-1).astype(BF16)


def kernel(x, c, w_ada, b_ada, norm1_g, w_in, b_gate, lam_q1, lam_k1, lam_q2, lam_k2, subln_g,
           t5_table, rel_bias_b, w_branch, w_out, norm2_g, w13, w2, final_g):
    bsz, seq, d = x.shape
    depth = w_in.shape[0]
    t = bsz * seq
    assert d == D_MODEL and w_in.shape[2] == IN_COLS and w2.shape[1] == D_FF
    assert seq % (2 * B_PAD) == 0, "sequence tiles assume a multiple of 1024 frames"

    mod = _ada_mod(c, w_ada, b_ada).reshape(depth, bsz, 6, d)
    w_in_b = _prep_w_in(w_in)
    w_branch_b = w_branch.astype(BF16)
    w_out_b = w_out.astype(BF16)
    w13_b = w13.astype(BF16)
    w2_b = w2.astype(BF16)
    tq_a = min(TQ_A, seq // 2)
    a_bias, a_bmax = _a_bias_tiles(t5_table, tq_a, min(TK_A, tq_a))
    b_bias, b_bmax = _b_bias_tiles(rel_bias_b, TQ_B)
    fg = final_g.reshape(1, d)
    row = lambda p: p.reshape(depth, 1, -1)
    g1, g2, bg = row(norm1_g), row(norm2_g), row(b_gate)
    lams = [row(p) for p in (lam_q1, lam_k1, lam_q2, lam_k2)]
    subg = subln_g.reshape(depth, -1, 1)

    x2d = x.reshape(t, d)
    for l in range(depth):
        lam_init = 0.8 - 0.6 * math.exp(-0.3 * l)
        proj = _inproj(x2d, mod, g1, w_in_b, l, seq)
        ya = _attn_a(proj, a_bias, a_bmax, *lams, subg, l, bsz, seq, lam_init)
        yb = _attn_b(proj, b_bias, b_bmax, l, bsz, seq)
        yc = _attn_c(proj, bsz, seq)
        x2d = _merge(x2d, mod, proj, bg, ya, yb, yc, w_branch_b, w_out_b, l, seq)
        x2d = _ffn(x2d, mod, g2, fg, w13_b, w2_b, l, seq, final=(l == depth - 1))
    return x2d.reshape(bsz, seq, d)
```

```python
import functools
import math

import numpy as np
import jax
import jax.numpy as jnp
from jax import lax
from jax.experimental import pallas as pl
from jax.experimental.pallas import tpu as pltpu

F32 = jnp.float32
BF16 = jnp.bfloat16

D_MODEL = 1024
DEPTH = 4
CHUNK = 64
HEAD_DIM = 64
BRANCH_WIDTH = 512
N_BRANCH = 3
A_HEADS = 4
B_HEADS = 8
C_HEADS = 4
C_HEAD_DIM = 128
B_LEFT_CHUNKS = 8
B_PAD = B_LEFT_CHUNKS * CHUNK
B_MAX_REL = 128
T5_BUCKETS = 32
T5_MAX_DIST = 128
D_FF = 2816
QKV_COLS = 3 * N_BRANCH * BRANCH_WIDTH
GATE_COLS = N_BRANCH * D_MODEL
IN_COLS = QKV_COLS + GATE_COLS
NEG_INF = -1e30
EPS = 1e-6
LOG2E = math.log2(math.e)

LANES = 128
N_COLBLK = IN_COLS // LANES
GATE_BLKS = GATE_COLS // LANES
AQ0, AK0, AV0 = GATE_BLKS, GATE_BLKS + 4, GATE_BLKS + 8
BQ0, BK0, BV0 = GATE_BLKS + 12, GATE_BLKS + 16, GATE_BLKS + 20
CQ0, CK0, CV0 = GATE_BLKS + 24, GATE_BLKS + 28, GATE_BLKS + 32

TM_PROJ = 512
TN_PROJ = 1536
TM_MERGE = 512
TM_FFN = 512
FFN_CHUNKS = ((0, 1536), (1536, 2816))
TQ_A = 1024
TK_A = 512
TP_A = 512
UNROLL_A = 2
NORM_SLACK_A = 1.02
L_MIN_A = 2.0 ** -100
TQ_B = 512
UNROLL_B = 3
TQ_C = 256
GROUP_C = 8
C_SKIP_LOG2 = -150.0
VMEM_LIMIT = 56 * 1024 * 1024


def _cparams(*sem, flags=None):
    return pltpu.CompilerParams(dimension_semantics=sem, vmem_limit_bytes=VMEM_LIMIT, flags=flags)


def _ada_kernel(c_ref, w_ref, b_ref, o_ref):
    c = c_ref[...]
    cs = c * jax.nn.sigmoid(c)
    o_ref[...] = jnp.dot(cs, w_ref[...], preferred_element_type=F32,
                         precision=lax.Precision.HIGHEST) + b_ref[...]


def _ada_mod(c, w_ada, b_ada):
    depth, d, e = w_ada.shape
    bsz = c.shape[0]
    nblk = e // d
    return pl.pallas_call(
        _ada_kernel,
        grid=(depth, nblk),
        in_specs=[
            pl.BlockSpec((bsz, d), lambda l, j: (0, 0)),
            pl.BlockSpec((None, d, d), lambda l, j: (l, 0, j)),
            pl.BlockSpec((None, 1, d), lambda l, j: (l, 0, j)),
        ],
        out_specs=pl.BlockSpec((None, bsz, d), lambda l, j: (l, 0, j)),
        out_shape=jax.ShapeDtypeStruct((depth, bsz, e), F32),
        compiler_params=_cparams("arbitrary", "arbitrary"),
        name="ada_mod",
    )(c, w_ada, b_ada.reshape(depth, 1, e))


def _norm_mod(x, g, shift, scale):
    ms = jnp.mean(x * x, axis=-1, keepdims=True)
    y = x * lax.rsqrt(ms + EPS) * g
    return y * (1.0 + scale) + shift


def _inproj_kernel(x_ref, mod_ref, g_ref, w_ref, o_ref):
    h = _norm_mod(x_ref[...], g_ref[...], mod_ref[0:1, :], mod_ref[1:2, :]).astype(BF16)
    n = w_ref.shape[1]
    for c0 in range(0, n, TN_PROJ):
        res = jnp.dot(h, w_ref[:, c0:c0 + TN_PROJ], preferred_element_type=F32)
        for cb in range(TN_PROJ // LANES):
            o_ref[c0 // LANES + cb] = res[:, cb * LANES:(cb + 1) * LANES].astype(o_ref.dtype)


def _inproj(x2d, mod, g, w, layer, seq):
    t, d = x2d.shape
    n = w.shape[2]
    assert n % TN_PROJ == 0
    tm = min(TM_PROJ, seq)
    return pl.pallas_call(
        _inproj_kernel,
        grid=(t // tm,),
        in_specs=[
            pl.BlockSpec((tm, d), lambda i: (i, 0)),
            pl.BlockSpec((None, None, 6, d), lambda i: (layer, (i * tm) // seq, 0, 0)),
            pl.BlockSpec((None, 1, d), lambda i: (layer, 0, 0)),
            pl.BlockSpec((None, d, n), lambda i: (layer, 0, 0), pipeline_mode=pl.Buffered(1)),
        ],
        out_specs=pl.BlockSpec((n // LANES, tm, LANES), lambda i: (0, i, 0)),
        out_shape=jax.ShapeDtypeStruct((n // LANES, t, LANES), BF16),
        compiler_params=_cparams("parallel"),
        name="in_proj",
    )(x2d, mod, g, w)


def _t5_bucket_np(rel):
    nb = T5_BUCKETS // 2
    max_exact = nb // 2
    ret = np.where(rel > 0, nb, 0)
    n = np.abs(rel)
    nf = np.maximum(n, 1).astype(np.float32)
    scaled = (np.log(nf / np.float32(max_exact)) / np.float32(math.log(T5_MAX_DIST / max_exact))
              * np.float32(nb - max_exact))
    large = max_exact + scaled.astype(np.int32)
    large = np.minimum(large, nb - 1)
    return (ret + np.where(n < max_exact, n, large)).astype(np.int32)


def _toeplitz(vec, rows, cols):
    length = rows + cols - 1
    assert vec.shape[-1] == length
    lead = vec.shape[:-1]
    ext = jnp.concatenate([vec, jnp.zeros(lead + (1,), vec.dtype)], axis=-1)
    flat = jnp.broadcast_to(ext[..., None, :], lead + (rows, length + 1)).reshape(lead + (rows * (length + 1),))
    return flat[..., :rows * length].reshape(lead + (rows, length))[..., rows - 1:rows - 1 + cols]


def _a_bias_tiles(t5_table, tq, tk):
    r = np.arange(tq)[None, :]
    c = np.arange(tk)[:, None]
    far_bucket = _t5_bucket_np((c - 2 * tk) - r)
    assert (far_bucket == far_bucket[0, 0]).all() and far_bucket[0, 0] == _t5_bucket_np(np.array(-10 * tq))
    far = t5_table[int(far_bucket[0, 0])].astype(F32)
    tiles = [jnp.zeros((A_HEADS, tk, tq), F32)]
    bmax = jnp.zeros((A_HEADS,), F32)
    u = np.arange(tk + tq - 1)
    for koff in range(-tk, tq, tk):
        vec = t5_table[_t5_bucket_np(koff + tk - 1 - u)].astype(F32).T
        vec = (vec - far[:, None]) * LOG2E
        bmax = jnp.maximum(bmax, jnp.max(vec, axis=1))
        bias = _toeplitz(vec, tk, tq)
        allowed = ((c + koff) // CHUNK) <= (r // CHUNK)
        tiles.append(jnp.where(jnp.asarray(allowed)[None], bias, NEG_INF))
    tiles.append(jnp.full((A_HEADS, tk, tq), NEG_INF, F32))
    return jnp.stack(tiles, axis=1), jnp.broadcast_to(bmax[:, None, None], (A_HEADS, 1, LANES))


def _attn_a_kernel(lq1_ref, lk1_ref, lq2_ref, lk2_ref, subg_ref, bmax_ref, bias_ref, q_ref, k_ref, v_ref,
                   o_ref, vt_ref, acc_ref, *s_refs, tq, tk, lam_init):
    seq = q_ref.shape[0]
    nq = seq // tq
    nkb = seq // tk
    per_q = tq // tk
    unroll = len(s_refs)
    tp = TP_A
    lam = (jnp.exp(jnp.sum(lq1_ref[...] * lk1_ref[...], axis=-1, keepdims=True))
           - jnp.exp(jnp.sum(lq2_ref[...] * lk2_ref[...], axis=-1, keepdims=True)) + lam_init)
    lane = lax.broadcasted_iota(jnp.int32, (1, LANES), 1)
    mask1 = (lane < HEAD_DIM).astype(BF16)
    mask2 = (lane >= HEAD_DIM).astype(BF16)
    subg = subg_ref[...] * (1.0 - lam_init)
    sel = (lax.broadcasted_iota(jnp.int32, (8, LANES), 0)
           == lax.broadcasted_iota(jnp.int32, (8, LANES), 1) // HEAD_DIM).astype(BF16)

    def prepare(b, kn2):
        c0 = pl.multiple_of(b * tk, tk)
        vt_ref[:, pl.ds(c0, tk)] = jnp.transpose(v_ref[pl.ds(c0, tk), :])
        kb = k_ref[pl.ds(c0, tk), :]
        blk = lax.dot_general(sel, kb * kb, (((1,), (1,)), ((), ())), preferred_element_type=F32)
        return jnp.maximum(kn2, blk)

    kn2 = lax.fori_loop(0, nkb, prepare, jnp.zeros((8, tk), F32))
    kmax2 = jnp.max(kn2, axis=1, keepdims=True) * NORM_SLACK_A
    kmax = jnp.sqrt(jnp.concatenate([jnp.broadcast_to(kmax2[0:1], (1, tq)),
                                     jnp.broadcast_to(kmax2[1:2], (1, tq))], axis=1))

    def finish(i, acc, l):
        o = acc * (1.0 / l)
        d = o[:, :tq] - lam * o[:, tq:]
        y = d * lax.rsqrt(jnp.mean(d * d, axis=0, keepdims=True) + EPS) * subg
        o_ref[pl.ds(pl.multiple_of(i * tq, tq), tq), :] = jnp.transpose(y).astype(o_ref.dtype)

    def load_q(i):
        qb = q_ref[pl.ds(pl.multiple_of(i * tq, tq), tq), :]
        return jnp.concatenate([qb * mask1, qb * mask2], axis=0)

    def steps(i, step, st):
        n_plain = jnp.maximum(per_q * i - 1, 0) // unroll
        n_steps = (per_q * (i + 1) + unroll - 1) // unroll
        st = lax.fori_loop(0, n_plain, lambda it, s_: step(it, s_, False), st)
        return lax.fori_loop(n_plain, n_steps, lambda it, s_: step(it, s_, True), st)

    def bounded_pass(i):
        qq = load_q(i)
        qn2 = lax.dot_general(jnp.ones((8, LANES), BF16), qq * qq, (((1,), (1,)), ((), ())),
                              preferred_element_type=F32)[0:1]
        shift = jnp.sqrt(qn2 * NORM_SLACK_A) * kmax + (bmax_ref[:, 0:1] + 1.0)
        acc_ref[...] = jnp.zeros(acc_ref.shape, F32)

        def block(js, l, bias, late_queries_only=False):
            c0 = pl.multiple_of(js * tk, tk)
            kb = k_ref[pl.ds(c0, tk), :]
            vt = vt_ref[:, pl.ds(c0, tk)]
            sums = []
            for m0 in (0, tq):
                cols = slice(m0 + tq // 2, m0 + tq) if late_queries_only else slice(m0, m0 + tq)
                s = lax.dot_general(kb, qq[cols], (((1,), (1,)), ((), ())), preferred_element_type=F32)
                if bias is not None:
                    s = s + jnp.concatenate([bias, bias], axis=1)[:, cols]
                p = jnp.exp2(s - shift[:, cols])
                ps = jnp.sum(p, axis=0, keepdims=True)
                if late_queries_only:
                    ps = jnp.concatenate([jnp.zeros((1, tq // 2), F32), ps], axis=1)
                sums.append(ps)
                acc_ref[:, cols] += jnp.dot(vt, p.astype(BF16), preferred_element_type=F32)
            return l + jnp.concatenate(sums, axis=1)

        def plain_step(it, l):
            for u in range(unroll):
                l = block(it * unroll + u, l, None)
            return l

        l = lax.fori_loop(0, jnp.maximum(i - 1, 0), plain_step, jnp.zeros((1, 2 * tq), F32))

        def previous_blocks(l):
            return block(2 * i - 1, block(2 * i - 2, l, None), bias_ref[1])

        l = lax.cond(i > 0, previous_blocks, lambda l: l, l)
        l = block(2 * i, l, bias_ref[2])
        l = block(2 * i + 1, l, bias_ref[3], late_queries_only=True)
        finish(i, acc_ref[...], l)
        return jnp.min(l)

    def running_max_pass(i):
        qq = load_q(i)
        acc_ref[...] = jnp.zeros(acc_ref.shape, F32)

        def block(js, st, with_bias, s_ref):
            m, l = st
            c0 = pl.multiple_of(jnp.minimum(js, nkb - 1) * tk, tk)
            kb = k_ref[pl.ds(c0, tk), :]
            s = lax.dot_general(kb, qq, (((1,), (1,)), ((), ())), preferred_element_type=F32)
            if with_bias:
                bias = bias_ref[jnp.clip(js - per_q * i + 2, 0, per_q + 2)]
                s = s + jnp.concatenate([bias, bias], axis=1)
            s_ref[...] = s
            m_new = jnp.maximum(m, jnp.max(s, axis=0, keepdims=True))
            alpha = jnp.exp2(m - m_new)
            sums = []
            for ct in range(2 * tq // tp):
                cols = slice(ct * tp, (ct + 1) * tp)
                p = jnp.exp2(s_ref[:, cols] - m_new[:, cols])
                sums.append(jnp.sum(p, axis=0, keepdims=True))
                vt = vt_ref[:, pl.ds(c0, tk)]
                acc_ref[:, cols] = (alpha[:, cols] * acc_ref[:, cols]
                                    + jnp.dot(vt, p.astype(BF16), preferred_element_type=F32))
            return m_new, alpha * l + jnp.concatenate(sums, axis=1)

        def step(it, st, with_bias):
            for u in range(unroll):
                st = block(it * unroll + u, st, with_bias, s_refs[u])
            return st

        _, l = steps(i, step, (jnp.full((1, 2 * tq), NEG_INF, F32), jnp.zeros((1, 2 * tq), F32)))
        finish(i, acc_ref[...], l)

    def qtile(i, carry):
        l_min = bounded_pass(i)

        @pl.when(jnp.logical_not(l_min > L_MIN_A))
        def _():
            running_max_pass(i)

        return carry

    lax.fori_loop(0, nq, qtile, 0)


def _attn_a(proj, bias, bmax, lq1, lk1, lq2, lk2, subg, layer, bsz, seq, lam_init):
    tq = min(TQ_A, seq // 2)
    tk = min(TK_A, tq)
    assert tk == TP_A and tq == 2 * tk and UNROLL_A == 2
    vec = lambda n: pl.BlockSpec((None, 1, n), lambda b, h: (layer, 0, 0))
    col = lambda base: pl.BlockSpec((None, seq, LANES), lambda b, h: (base + h, b, 0))
    return pl.pallas_call(
        functools.partial(_attn_a_kernel, tq=tq, tk=tk, lam_init=lam_init),
        grid=(bsz, A_HEADS),
        in_specs=[vec(HEAD_DIM), vec(HEAD_DIM), vec(HEAD_DIM), vec(HEAD_DIM),
                  pl.BlockSpec((None, 2 * HEAD_DIM, 1), lambda b, h: (layer, 0, 0)),
                  pl.BlockSpec((None, 1, LANES), lambda b, h: (h, 0, 0)),
                  pl.BlockSpec((None, tq // tk + 3, tk, tq), lambda b, h: (h, 0, 0, 0)),
                  col(AQ0), col(AK0), col(AV0)],
        out_specs=pl.BlockSpec((None, seq, LANES), lambda b, h: (h, b, 0)),
        out_shape=jax.ShapeDtypeStruct((A_HEADS, bsz * seq, LANES), BF16),
        scratch_shapes=[pltpu.VMEM((LANES, seq), BF16), pltpu.VMEM((LANES, 2 * tq), F32)]
        + [pltpu.VMEM((tk, 2 * tq), F32) for _ in range(UNROLL_A)],
        compiler_params=_cparams("parallel", "parallel"),
        name="attn_a",
    )(lq1, lk1, lq2, lk2, subg, bmax, bias, proj, proj, proj)


def _b_bias_tiles(rel_bias, tq):
    depth = rel_bias.shape[0]
    assert B_PAD % tq == 0
    r = np.arange(tq)[None, :]
    c = np.arange(tq)[:, None]
    lo = CHUNK * (r // CHUNK)
    u = np.arange(2 * tq - 1)
    tiles = []
    for koff in range(-B_PAD, tq, tq):
        in_band = (c + koff >= lo - B_PAD) & (c + koff < lo + CHUNK)
        idx = np.clip(u - (tq - 1) - koff, -B_MAX_REL, B_MAX_REL) + B_MAX_REL
        bias = _toeplitz(rel_bias[..., idx].astype(F32) * LOG2E, tq, tq)
        bias = jnp.where(jnp.asarray(in_band), bias, NEG_INF)
        tiles.append(bias.reshape(depth, B_HEADS // 2, 2, tq, tq).transpose(0, 1, 3, 2, 4)
                     .reshape(depth, B_HEADS // 2, tq, 2 * tq))
    bmax = jnp.max((rel_bias.astype(F32) * LOG2E).reshape(depth, B_HEADS // 2, -1), axis=2)
    return (jnp.stack(tiles, axis=2),
            jnp.broadcast_to(bmax[:, :, None, None], (depth, B_HEADS // 2, 1, LANES)))


def _attn_b_kernel(bmax_ref, bias_ref, q_ref, k_ref, v_ref, o_ref, vt_ref, *s_refs, tq):
    seq = q_ref.shape[0]
    nq = seq // tq
    max_prev = B_PAD // tq
    lane = lax.broadcasted_iota(jnp.int32, (1, LANES), 1)
    mask1 = (lane < HEAD_DIM).astype(BF16)
    mask2 = (lane >= HEAD_DIM).astype(BF16)
    feat = lax.broadcasted_iota(jnp.int32, (LANES, tq), 0)
    sel = (lax.broadcasted_iota(jnp.int32, (8, LANES), 0)
           == lax.broadcasted_iota(jnp.int32, (8, LANES), 1) // HEAD_DIM).astype(BF16)

    def prepare(b, kn2):
        c0 = pl.multiple_of(b * tq, tq)
        vt_ref[:, pl.ds(c0, tq)] = jnp.transpose(v_ref[pl.ds(c0, tq), :])
        kb = k_ref[pl.ds(c0, tq), :]
        blk = lax.dot_general(sel, kb * kb, (((1,), (1,)), ((), ())), preferred_element_type=F32)
        return jnp.maximum(kn2, blk)

    kn2 = lax.fori_loop(0, nq, prepare, jnp.zeros((8, tq), F32))
    kmax2 = jnp.max(kn2, axis=1, keepdims=True) * NORM_SLACK_A
    kmax = jnp.sqrt(jnp.concatenate([jnp.broadcast_to(kmax2[0:1], (1, tq)),
                                     jnp.broadcast_to(kmax2[1:2], (1, tq))], axis=1))

    def tile_blocks(i, n_prev):
        static = isinstance(i, int)
        r0 = i * tq if static else pl.multiple_of(i * tq, tq)
        qb = q_ref[pl.ds(r0, tq), :]
        qq = jnp.concatenate([qb * mask1, qb * mask2], axis=0)
        starts = [r0 - d * tq for d in range(n_prev, 0, -1)] + [r0]
        blocks = tuple((c0 if static else pl.multiple_of(c0, tq), max_prev - n_prev + j)
                       for j, c0 in enumerate(starts))
        return r0, qq, blocks

    def finish(r0, acc, den):
        o = acc * (1.0 / den)
        o = jnp.where(feat < HEAD_DIM, o[:, :tq], o[:, tq:])
        o_ref[pl.ds(r0, tq), :] = jnp.transpose(o).astype(o_ref.dtype)

    def bounded_tile(i, n_prev):
        r0, qq, blocks = tile_blocks(i, n_prev)
        qn2 = lax.dot_general(jnp.ones((8, LANES), BF16), qq * qq, (((1,), (1,)), ((), ())),
                              preferred_element_type=F32)[0:1]
        shift = jnp.sqrt(qn2 * NORM_SLACK_A) * kmax + (bmax_ref[:, 0:1] + 1.0)
        acc = None
        den = None
        for c0, t in blocks:
            s = lax.dot_general(k_ref[pl.ds(c0, tq), :], qq, (((1,), (1,)), ((), ())),
                                preferred_element_type=F32) + bias_ref[t]
            p = jnp.exp2(s - shift)
            ps = jnp.sum(p, axis=0, keepdims=True)
            pv = jnp.dot(vt_ref[:, pl.ds(c0, tq)], p.astype(BF16), preferred_element_type=F32)
            den = ps if den is None else den + ps
            acc = pv if acc is None else acc + pv
        finish(r0, acc, den)
        return jnp.min(den)

    def exact_tile(i, refs, n_prev):
        r0, qq, blocks = tile_blocks(i, n_prev)
        m = None
        for (c0, t), s_ref in zip(blocks, refs):
            s = lax.dot_general(k_ref[pl.ds(c0, tq), :], qq, (((1,), (1,)), ((), ())),
                                preferred_element_type=F32) + bias_ref[t]
            s_ref[...] = s
            bm = jnp.max(s, axis=0, keepdims=True)
            m = bm if m is None else jnp.maximum(m, bm)
        acc = None
        den = None
        for (c0, t), s_ref in zip(blocks, refs):
            p = jnp.exp2(s_ref[...] - m)
            ps = jnp.sum(p, axis=0, keepdims=True)
            pv = jnp.dot(vt_ref[:, pl.ds(c0, tq)], p.astype(BF16), preferred_element_type=F32)
            den = ps if den is None else den + ps
            acc = pv if acc is None else acc + pv
        finish(r0, acc, den)

    def tiles(idx, n_prev):
        mins = [bounded_tile(i, n_prev) for i in idx]
        for i, l_min in zip(idx, mins):
            @pl.when(jnp.logical_not(l_min > L_MIN_A))
            def _():
                exact_tile(i, s_refs, n_prev)

    for i in range(min(max_prev, nq)):
        tiles([i], i)

    def step(it, carry):
        tiles([max_prev + UNROLL_B * it + u for u in range(UNROLL_B)], max_prev)
        return carry

    n_rest = max(nq - max_prev, 0)
    lax.fori_loop(0, n_rest // UNROLL_B, step, 0)
    for i in range(nq - n_rest % UNROLL_B, nq):
        tiles([i], max_prev)


def _attn_b(proj, bias, bmax, layer, bsz, seq):
    tq = TQ_B
    col = lambda base: pl.BlockSpec((None, seq, LANES), lambda b, h: (base + h, b, 0))
    return pl.pallas_call(
        functools.partial(_attn_b_kernel, tq=tq),
        grid=(bsz, B_HEADS // 2),
        in_specs=[pl.BlockSpec((None, None, 1, LANES), lambda b, h: (layer, h, 0, 0)),
                  pl.BlockSpec((None, None, B_PAD // tq + 1, tq, 2 * tq), lambda b, h: (layer, h, 0, 0, 0)),
                  col(BQ0), col(BK0), col(BV0)],
        out_specs=pl.BlockSpec((None, seq, LANES), lambda b, h: (h, b, 0)),
        out_shape=jax.ShapeDtypeStruct((B_HEADS // 2, bsz * seq, LANES), BF16),
        scratch_shapes=[pltpu.VMEM((LANES, seq), BF16)]
        + [pltpu.VMEM((tq, 2 * tq), F32) for _ in range(B_PAD // tq + 1)],
        compiler_params=_cparams("parallel", "parallel"),
        name="attn_b",
    )(bmax, bias, proj, proj, proj)


def _attn_c_kernel(q_ref, k_ref, v_ref, o_ref, *, tq, group):
    seq = q_ref.shape[0]
    nq = seq // tq
    row = lax.broadcasted_iota(jnp.int32, (tq, tq), 0)
    colm = lax.broadcasted_iota(jnp.int32, (tq, tq), 1)
    tri_strict = (row > colm).astype(BF16)
    before = colm < row

    def block(i, j, carry, acc, diag):
        qb = q_ref[pl.ds(pl.multiple_of(i * tq, tq), tq), :]
        c0 = pl.multiple_of(j * tq, tq)
        kb = k_ref[pl.ds(c0, tq), :]
        vb = v_ref[pl.ds(c0, tq), :]
        z = lax.dot_general(qb, kb, (((1,), (1,)), ((), ())), preferred_element_type=F32)
        lm = -(jnp.maximum(z, 0.0) + jnp.log2(1.0 + jnp.exp2(-jnp.abs(z))))
        if diag:
            lm = jnp.where(before, lm, 0.0)
        lm_b = lm.astype(BF16)
        excl = jnp.dot(lm_b, tri_strict, preferred_element_type=F32)
        logw = (z + lm) + excl + carry
        wgt = jnp.exp2(logw)
        if diag:
            wgt = jnp.where(before, wgt, 0.0)
        acc = acc + jnp.dot(wgt.astype(BF16), vb, preferred_element_type=F32)
        carry = carry + excl[:, 0:1] + lm_b[:, 0:1].astype(F32)
        return carry, acc

    def qgroup(gi, c):
        i0 = gi * group
        sts = [block(i0 + g, i0 + g, jnp.zeros((tq, 1), F32), jnp.zeros((tq, LANES), F32), True)
               for g in range(group)]
        carries = tuple(st[0] for st in sts)
        accs = tuple(st[1] for st in sts)

        def live(carries):
            return functools.reduce(jnp.maximum, [jnp.max(cr) for cr in carries])

        def cond(st):
            t, cmax, _, _ = st
            return jnp.logical_and(t <= i0 + group - 1, cmax > C_SKIP_LOG2)

        def body(st):
            t, _, carries, accs = st
            new = []
            for g in range(group):
                j = i0 + g - t
                cin = jnp.where(j >= 0, carries[g], NEG_INF)
                new.append(block(i0 + g, jnp.maximum(j, 0), cin, accs[g], False))
            carries = tuple(st_[0] for st_ in new)
            return t + 1, live(carries), carries, tuple(st_[1] for st_ in new)

        _, _, _, accs = lax.while_loop(cond, body, (jnp.int32(1), live(carries), carries, accs))
        for g in range(group):
            o_ref[pl.ds(pl.multiple_of((i0 + g) * tq, tq), tq), :] = accs[g].astype(o_ref.dtype)
        return c

    lax.fori_loop(0, nq // group, qgroup, 0)


def _attn_c(proj, bsz, seq):
    tq = min(TQ_C, seq)
    group = min(GROUP_C, seq // tq)
    assert (seq // tq) % group == 0
    col = lambda base: pl.BlockSpec((None, seq, LANES), lambda b, h: (base + h, b, 0))
    return pl.pallas_call(
        functools.partial(_attn_c_kernel, tq=tq, group=group),
        grid=(bsz, C_HEADS),
        in_specs=[col(CQ0), col(CK0), col(CV0)],
        out_specs=pl.BlockSpec((None, seq, LANES), lambda b, h: (h, b, 0)),
        out_shape=jax.ShapeDtypeStruct((C_HEADS, bsz * seq, LANES), BF16),
        compiler_params=_cparams("parallel", "parallel"),
        name="attn_c",
    )(proj, proj, proj)


def _cat_lanes(ref, start, count):
    return jnp.concatenate([ref[start + c] for c in range(count)], axis=1)


def _merge_kernel(x_ref, mod_ref, gate_ref, bg_ref, ya_ref, yb_ref, yc_ref, wb_ref, wo_ref, o_ref):
    blks = D_MODEL // LANES
    merged = None
    for r, y_ref in enumerate((ya_ref, yb_ref, yc_ref)):
        y = _cat_lanes(y_ref, 0, BRANCH_WIDTH // LANES)
        br = jnp.dot(y, wb_ref[r], preferred_element_type=F32)
        pre = _cat_lanes(gate_ref, r * blks, blks).astype(F32) + bg_ref[:, r * D_MODEL:(r + 1) * D_MODEL]
        term = jax.nn.sigmoid(pre) * br
        merged = term if merged is None else merged + term
    out = jnp.dot(merged.astype(BF16), wo_ref[...], preferred_element_type=F32)
    o_ref[...] = x_ref[...] + mod_ref[2:3, :] * out


def _merge(x2d, mod, proj, b_gate, ya, yb, yc, w_branch, w_out, layer, seq):
    t, d = x2d.shape
    tm = min(TM_MERGE, seq)
    ybs = lambda: pl.BlockSpec((BRANCH_WIDTH // LANES, tm, LANES), lambda i: (0, i, 0))
    return pl.pallas_call(
        _merge_kernel,
        grid=(t // tm,),
        in_specs=[
            pl.BlockSpec((tm, d), lambda i: (i, 0)),
            pl.BlockSpec((None, None, 6, d), lambda i: (layer, (i * tm) // seq, 0, 0)),
            pl.BlockSpec((GATE_BLKS, tm, LANES), lambda i: (0, i, 0)),
            pl.BlockSpec((None, 1, GATE_COLS), lambda i: (layer, 0, 0)),
            ybs(), ybs(), ybs(),
            pl.BlockSpec((None, N_BRANCH, BRANCH_WIDTH, d), lambda i: (layer, 0, 0, 0)),
            pl.BlockSpec((None, d, d), lambda i: (layer, 0, 0)),
        ],
        out_specs=pl.BlockSpec((tm, d), lambda i: (i, 0)),
        out_shape=jax.ShapeDtypeStruct((t, d), F32),
        compiler_params=_cparams("parallel"),
        name="merge",
    )(x2d, mod, proj, b_gate, ya, yb, yc, w_branch, w_out)


def _ffn_kernel(x_ref, mod_ref, g_ref, fg_ref, w13_ref, w2_ref, o_ref, *, final):
    dff = w2_ref.shape[0]
    x = x_ref[...]
    h = _norm_mod(x, g_ref[...], mod_ref[3:4, :], mod_ref[4:5, :]).astype(BF16)
    acc = None
    for f0, f1 in FFN_CHUNKS:
        u_gate = jnp.dot(h, w13_ref[:, f0:f1], preferred_element_type=F32)
        u_up = jnp.dot(h, w13_ref[:, dff + f0:dff + f1], preferred_element_type=F32)
        act = (u_gate * jax.nn.sigmoid(u_gate) * u_up).astype(BF16)
        part = jnp.dot(act, w2_ref[f0:f1, :], preferred_element_type=F32)
        acc = part if acc is None else acc + part
    y = x + mod_ref[5:6, :] * acc
    if final:
        y = y * lax.rsqrt(jnp.mean(y * y, axis=-1, keepdims=True) + EPS) * fg_ref[...]
    o_ref[...] = y


def _ffn(x2d, mod, g, final_g, w13, w2, layer, seq, final):
    t, d = x2d.shape
    dff = w2.shape[1]
    assert FFN_CHUNKS[0][0] == 0 and FFN_CHUNKS[-1][1] == dff
    tm = min(TM_FFN, seq)
    resident = lambda shape: pl.BlockSpec((None,) + shape, lambda i: (layer, 0, 0), pipeline_mode=pl.Buffered(1))
    return pl.pallas_call(
        functools.partial(_ffn_kernel, final=final),
        grid=(t // tm,),
        in_specs=[
            pl.BlockSpec((tm, d), lambda i: (i, 0)),
            pl.BlockSpec((None, None, 6, d), lambda i: (layer, (i * tm) // seq, 0, 0)),
            pl.BlockSpec((None, 1, d), lambda i: (layer, 0, 0)),
            pl.BlockSpec((1, d), lambda i: (0, 0)),
            resident((d, 2 * dff)),
            resident((dff, d)),
        ],
        out_specs=pl.BlockSpec((tm, d), lambda i: (i, 0)),
        out_shape=jax.ShapeDtypeStruct((t, d), F32),
        compiler_params=_cparams("parallel"),
        name="ffn",
    )(x2d, mod, g, final_g, w13, w2)


def _prep_w_in(w_in):
    scale = np.ones((IN_COLS,), np.float32)
    scale[0 * BRANCH_WIDTH:1 * BRANCH_WIDTH] = HEAD_DIM ** -0.5 * LOG2E
    scale[3 * BRANCH_WIDTH:4 * BRANCH_WIDTH] = HEAD_DIM ** -0.5 * LOG2E
    scale[6 * BRANCH_WIDTH:7 * BRANCH_WIDTH] = C_HEAD_DIM ** -0.5 * LOG2E
    w = w_in * jnp.asarray(scale)
    return jnp.concatenate([w[..., QKV_COLS:], w[..., :QKV_COLS]], axis=-1).astype(BF16)


def kernel(x, c, w_ada, b_ada, norm1_g, w_in, b_gate, lam_q1, lam_k1, lam_q2, lam_k2, subln_g,
           t5_table, rel_bias_b, w_branch, w_out, norm2_g, w13, w2, final_g):
    bsz, seq, d = x.shape
    depth = w_in.shape[0]
    t = bsz * seq
    assert d == D_MODEL and w_in.shape[2] == IN_COLS and w2.shape[1] == D_FF
    assert seq % (2 * B_PAD) == 0, "sequence tiles assume a multiple of 1024 frames"

    mod = _ada_mod(c, w_ada, b_ada).reshape(depth, bsz, 6, d)
    w_in_b = _prep_w_in(w_in)
    w_branch_b = w_branch.astype(BF16)
    w_out_b = w_out.astype(BF16)
    w13_b = w13.astype(BF16)
    w2_b = w2.astype(BF16)
    tq_a = min(TQ_A, seq // 2)
    a_bias, a_bmax = _a_bias_tiles(t5_table, tq_a, min(TK_A, tq_a))
    b_bias, b_bmax = _b_bias_tiles(rel_bias_b, TQ_B)
    fg = final_g.reshape(1, d)
    row = lambda p: p.reshape(depth, 1, -1)
    g1, g2, bg = row(norm1_g), row(norm2_g), row(b_gate)
    lams = [row(p) for p in (lam_q1, lam_k1, lam_q2, lam_k2)]
    subg = subln_g.reshape(depth, -1, 1)

    x2d = x.reshape(t, d)
    for l in range(depth):
        lam_init = 0.8 - 0.6 * math.exp(-0.3 * l)
        proj = _inproj(x2d, mod, g1, w_in_b, l, seq)
        ya = _attn_a(proj, a_bias, a_bmax, *lams, subg, l, bsz, seq, lam_init)
        yb = _attn_b(proj, b_bias, b_bmax, l, bsz, seq)
        yc = _attn_c(proj, bsz, seq)
        x2d = _merge(x2d, mod, proj, bg, ya, yb, yc, w_branch_b, w_out_b, l, seq)
        x2d = _ffn(x2d, mod, g2, fg, w13_b, w2_b, l, seq, final=(l == depth - 1))
    return x2d.reshape(bsz, seq, d)
```

```python
import functools
import math

import numpy as np
import jax
import jax.numpy as jnp
from jax import lax
from jax.experimental import pallas as pl
from jax.experimental.pallas import tpu as pltpu

F32 = jnp.float32
BF16 = jnp.bfloat16

D_MODEL = 1024
CHUNK = 64
HEAD_DIM = 64
BRANCH_WIDTH = 512
N_BRANCH = 3
A_HEADS = 4
B_HEADS = 8
C_HEADS = 4
C_HEAD_DIM = 128
B_LEFT_CHUNKS = 8
B_PAD = B_LEFT_CHUNKS * CHUNK
B_MAX_REL = 128
T5_BUCKETS = 32
T5_MAX_DIST = 128
D_FF = 2816
QKV_COLS = 3 * N_BRANCH * BRANCH_WIDTH
GATE_COLS = N_BRANCH * D_MODEL
IN_COLS = QKV_COLS + GATE_COLS
NEG_INF = -1e30
EPS = 1e-6
LOG2E = math.log2(math.e)

LANES = 128
GATE_BLKS = GATE_COLS // LANES
AQ0, AK0, AV0 = GATE_BLKS, GATE_BLKS + 4, GATE_BLKS + 8
BQ0, BK0, BV0 = GATE_BLKS + 12, GATE_BLKS + 16, GATE_BLKS + 20
CQ0, CK0, CV0 = GATE_BLKS + 24, GATE_BLKS + 28, GATE_BLKS + 32

TM_PROJ = 512
TN_PROJ = 1536
TM_MERGE = 512
TM_FFN = 512
FFN_CHUNKS = ((0, 1536), (1536, 2816))
TQ_A = 1024
TK_A = 512
TP_A = 512
UNROLL_A = 2
NORM_SLACK_A = 1.02
L_MIN_A = 2.0 ** -100
TQ_B = 512
UNROLL_B = 3
TQ_C = 256
GROUP_C = 8
C_SKIP_LOG2 = -150.0
VMEM_LIMIT = 56 * 1024 * 1024


def _cparams(*sem):
    return pltpu.CompilerParams(dimension_semantics=sem, vmem_limit_bytes=VMEM_LIMIT)


def _ada_kernel(c_ref, w_ref, b_ref, o_ref):
    c = c_ref[...]
    cs = c * jax.nn.sigmoid(c)
    o_ref[...] = jnp.dot(cs, w_ref[...], preferred_element_type=F32,
                         precision=lax.Precision.HIGHEST) + b_ref[...]


def _ada_mod(c, w_ada, b_ada):
    depth, d, e = w_ada.shape
    bsz = c.shape[0]
    nblk = e // d
    return pl.pallas_call(
        _ada_kernel,
        grid=(depth, nblk),
        in_specs=[
            pl.BlockSpec((bsz, d), lambda l, j: (0, 0)),
            pl.BlockSpec((None, d, d), lambda l, j: (l, 0, j)),
            pl.BlockSpec((None, 1, d), lambda l, j: (l, 0, j)),
        ],
        out_specs=pl.BlockSpec((None, bsz, d), lambda l, j: (l, 0, j)),
        out_shape=jax.ShapeDtypeStruct((depth, bsz, e), F32),
        compiler_params=_cparams("arbitrary", "arbitrary"),
        name="ada_mod",
    )(c, w_ada, b_ada.reshape(depth, 1, e))


def _norm_mod(x, g, shift, scale):
    ms = jnp.mean(x * x, axis=-1, keepdims=True)
    y = x * lax.rsqrt(ms + EPS) * g
    return y * (1.0 + scale) + shift


def _inproj_kernel(x_ref, mod_ref, g_ref, w_ref, o_ref):
    h = _norm_mod(x_ref[...], g_ref[...], mod_ref[0:1, :], mod_ref[1:2, :]).astype(BF16)
    n = w_ref.shape[1]
    for c0 in range(0, n, TN_PROJ):
        res = jnp.dot(h, w_ref[:, c0:c0 + TN_PROJ], preferred_element_type=F32)
        for cb in range(TN_PROJ // LANES):
            o_ref[c0 // LANES + cb] = res[:, cb * LANES:(cb + 1) * LANES].astype(o_ref.dtype)


def _inproj(x2d, mod, g, w, layer, seq):
    t, d = x2d.shape
    n = w.shape[2]
    assert n % TN_PROJ == 0
    tm = min(TM_PROJ, seq)
    return pl.pallas_call(
        _inproj_kernel,
        grid=(t // tm,),
        in_specs=[
            pl.BlockSpec((tm, d), lambda i: (i, 0)),
            pl.BlockSpec((None, None, 6, d), lambda i: (layer, (i * tm) // seq, 0, 0)),
            pl.BlockSpec((None, 1, d), lambda i: (layer, 0, 0)),
            pl.BlockSpec((None, d, n), lambda i: (layer, 0, 0), pipeline_mode=pl.Buffered(1)),
        ],
        out_specs=pl.BlockSpec((n // LANES, tm, LANES), lambda i: (0, i, 0)),
        out_shape=jax.ShapeDtypeStruct((n // LANES, t, LANES), BF16),
        compiler_params=_cparams("parallel"),
        name="in_proj",
    )(x2d, mod, g, w)


def _t5_bucket_np(rel):
    nb = T5_BUCKETS // 2
    max_exact = nb // 2
    ret = np.where(rel > 0, nb, 0)
    n = np.abs(rel)
    nf = np.maximum(n, 1).astype(np.float32)
    scaled = (np.log(nf / np.float32(max_exact)) / np.float32(math.log(T5_MAX_DIST / max_exact))
              * np.float32(nb - max_exact))
    large = max_exact + scaled.astype(np.int32)
    large = np.minimum(large, nb - 1)
    return (ret + np.where(n < max_exact, n, large)).astype(np.int32)


def _toeplitz(vec, rows, cols):
    length = rows + cols - 1
    assert vec.shape[-1] == length
    lead = vec.shape[:-1]
    ext = jnp.concatenate([vec, jnp.zeros(lead + (1,), vec.dtype)], axis=-1)
    flat = jnp.broadcast_to(ext[..., None, :], lead + (rows, length + 1)).reshape(lead + (rows * (length + 1),))
    return flat[..., :rows * length].reshape(lead + (rows, length))[..., rows - 1:rows - 1 + cols]


def _a_bias_tiles(t5_table, tq, tk):
    r = np.arange(tq)[None, :]
    c = np.arange(tk)[:, None]
    far_bucket = _t5_bucket_np((c - 2 * tk) - r)
    assert (far_bucket == far_bucket[0, 0]).all() and far_bucket[0, 0] == _t5_bucket_np(np.array(-10 * tq))
    far = t5_table[int(far_bucket[0, 0])].astype(F32)
    tiles = [jnp.zeros((A_HEADS, tk, tq), F32)]
    bmax = jnp.zeros((A_HEADS,), F32)
    u = np.arange(tk + tq - 1)
    for koff in range(-tk, tq, tk):
        vec = t5_table[_t5_bucket_np(koff + tk - 1 - u)].astype(F32).T
        vec = (vec - far[:, None]) * LOG2E
        bmax = jnp.maximum(bmax, jnp.max(vec, axis=1))
        bias = _toeplitz(vec, tk, tq)
        allowed = ((c + koff) // CHUNK) <= (r // CHUNK)
        tiles.append(jnp.where(jnp.asarray(allowed)[None], bias, NEG_INF))
    tiles.append(jnp.full((A_HEADS, tk, tq), NEG_INF, F32))
    return jnp.stack(tiles, axis=1), jnp.broadcast_to(bmax[:, None, None], (A_HEADS, 1, LANES))


def _attn_a_kernel(lq1_ref, lk1_ref, lq2_ref, lk2_ref, subg_ref, bmax_ref, bias_ref, q_ref, k_ref, v_ref,
                   o_ref, vt_ref, acc_ref, *s_refs, tq, tk, lam_init):
    seq = q_ref.shape[0]
    nq = seq // tq
    nkb = seq // tk
    per_q = tq // tk
    unroll = len(s_refs)
    tp = TP_A
    lam = (jnp.exp(jnp.sum(lq1_ref[...] * lk1_ref[...], axis=-1, keepdims=True))
           - jnp.exp(jnp.sum(lq2_ref[...] * lk2_ref[...], axis=-1, keepdims=True)) + lam_init)
    lane = lax.broadcasted_iota(jnp.int32, (1, LANES), 1)
    mask1 = (lane < HEAD_DIM).astype(BF16)
    mask2 = (lane >= HEAD_DIM).astype(BF16)
    subg = subg_ref[...] * (1.0 - lam_init)
    sel = (lax.broadcasted_iota(jnp.int32, (8, LANES), 0)
           == lax.broadcasted_iota(jnp.int32, (8, LANES), 1) // HEAD_DIM).astype(BF16)

    def prepare(b, kn2):
        c0 = pl.multiple_of(b * tk, tk)
        vt_ref[:, pl.ds(c0, tk)] = jnp.transpose(v_ref[pl.ds(c0, tk), :])
        kb = k_ref[pl.ds(c0, tk), :]
        blk = lax.dot_general(sel, kb * kb, (((1,), (1,)), ((), ())), preferred_element_type=F32)
        return jnp.maximum(kn2, blk)

    kn2 = lax.fori_loop(0, nkb, prepare, jnp.zeros((8, tk), F32))
    kmax2 = jnp.max(kn2, axis=1, keepdims=True) * NORM_SLACK_A
    kmax = jnp.sqrt(jnp.concatenate([jnp.broadcast_to(kmax2[0:1], (1, tq)),
                                     jnp.broadcast_to(kmax2[1:2], (1, tq))], axis=1))

    def finish(i, acc, l):
        o = acc * (1.0 / l)
        d = o[:, :tq] - lam * o[:, tq:]
        y = d * lax.rsqrt(jnp.mean(d * d, axis=0, keepdims=True) + EPS) * subg
        o_ref[pl.ds(pl.multiple_of(i * tq, tq), tq), :] = jnp.transpose(y).astype(o_ref.dtype)

    def load_q(i):
        qb = q_ref[pl.ds(pl.multiple_of(i * tq, tq), tq), :]
        return jnp.concatenate([qb * mask1, qb * mask2], axis=0)

    def steps(i, step, st):
        n_plain = jnp.maximum(per_q * i - 1, 0) // unroll
        n_steps = (per_q * (i + 1) + unroll - 1) // unroll
        st = lax.fori_loop(0, n_plain, lambda it, s_: step(it, s_, False), st)
        return lax.fori_loop(n_plain, n_steps, lambda it, s_: step(it, s_, True), st)

    def bounded_pass(i):
        qq = load_q(i)
        qn2 = lax.dot_general(jnp.ones((8, LANES), BF16), qq * qq, (((1,), (1,)), ((), ())),
                              preferred_element_type=F32)[0:1]
        shift = jnp.sqrt(qn2 * NORM_SLACK_A) * kmax + (bmax_ref[:, 0:1] + 1.0)
        acc_ref[...] = jnp.zeros(acc_ref.shape, F32)

        def block(js, l, bias, late_queries_only=False):
            c0 = pl.multiple_of(js * tk, tk)
            kb = k_ref[pl.ds(c0, tk), :]
            vt = vt_ref[:, pl.ds(c0, tk)]
            sums = []
            for m0 in (0, tq):
                cols = slice(m0 + tq // 2, m0 + tq) if late_queries_only else slice(m0, m0 + tq)
                s = lax.dot_general(kb, qq[cols], (((1,), (1,)), ((), ())), preferred_element_type=F32)
                if bias is not None:
                    s = s + jnp.concatenate([bias, bias], axis=1)[:, cols]
                p = jnp.exp2(s - shift[:, cols])
                ps = jnp.sum(p, axis=0, keepdims=True)
                if late_queries_only:
                    ps = jnp.concatenate([jnp.zeros((1, tq // 2), F32), ps], axis=1)
                sums.append(ps)
                acc_ref[:, cols] += jnp.dot(vt, p.astype(BF16), preferred_element_type=F32)
            return l + jnp.concatenate(sums, axis=1)

        def plain_step(it, l):
            for u in range(unroll):
                l = block(it * unroll + u, l, None)
            return l

        l = lax.fori_loop(0, jnp.maximum(i - 1, 0), plain_step, jnp.zeros((1, 2 * tq), F32))

        def previous_blocks(l):
            return block(2 * i - 1, block(2 * i - 2, l, None), bias_ref[1])

        l = lax.cond(i > 0, previous_blocks, lambda l: l, l)
        l = block(2 * i, l, bias_ref[2])
        l = block(2 * i + 1, l, bias_ref[3], late_queries_only=True)
        finish(i, acc_ref[...], l)
        return jnp.min(l)

    def running_max_pass(i):
        qq = load_q(i)
        acc_ref[...] = jnp.zeros(acc_ref.shape, F32)

        def block(js, st, with_bias, s_ref):
            m, l = st
            c0 = pl.multiple_of(jnp.minimum(js, nkb - 1) * tk, tk)
            kb = k_ref[pl.ds(c0, tk), :]
            s = lax.dot_general(kb, qq, (((1,), (1,)), ((), ())), preferred_element_type=F32)
            if with_bias:
                bias = bias_ref[jnp.clip(js - per_q * i + 2, 0, per_q + 2)]
                s = s + jnp.concatenate([bias, bias], axis=1)
            s_ref[...] = s
            m_new = jnp.maximum(m, jnp.max(s, axis=0, keepdims=True))
            alpha = jnp.exp2(m - m_new)
            sums = []
            for ct in range(2 * tq // tp):
                cols = slice(ct * tp, (ct + 1) * tp)
                p = jnp.exp2(s_ref[:, cols] - m_new[:, cols])
                sums.append(jnp.sum(p, axis=0, keepdims=True))
                vt = vt_ref[:, pl.ds(c0, tk)]
                acc_ref[:, cols] = (alpha[:, cols] * acc_ref[:, cols]
                                    + jnp.dot(vt, p.astype(BF16), preferred_element_type=F32))
            return m_new, alpha * l + jnp.concatenate(sums, axis=1)

        def step(it, st, with_bias):
            for u in range(unroll):
                st = block(it * unroll + u, st, with_bias, s_refs[u])
            return st

        _, l = steps(i, step, (jnp.full((1, 2 * tq), NEG_INF, F32), jnp.zeros((1, 2 * tq), F32)))
        finish(i, acc_ref[...], l)

    def qtile(i, carry):
        l_min = bounded_pass(i)

        @pl.when(jnp.logical_not(l_min > L_MIN_A))
        def _():
            running_max_pass(i)

        return carry

    lax.fori_loop(0, nq, qtile, 0)


def _attn_a(proj, bias, bmax, lq1, lk1, lq2, lk2, subg, layer, bsz, seq, lam_init):
    tq = min(TQ_A, seq // 2)
    tk = min(TK_A, tq)
    assert tk == TP_A and tq == 2 * tk and UNROLL_A == 2
    vec = lambda n: pl.BlockSpec((None, 1, n), lambda b, h: (layer, 0, 0))
    col = lambda base: pl.BlockSpec((None, seq, LANES), lambda b, h: (base + h, b, 0))
    return pl.pallas_call(
        functools.partial(_attn_a_kernel, tq=tq, tk=tk, lam_init=lam_init),
        grid=(bsz, A_HEADS),
        in_specs=[vec(HEAD_DIM), vec(HEAD_DIM), vec(HEAD_DIM), vec(HEAD_DIM),
                  pl.BlockSpec((None, 2 * HEAD_DIM, 1), lambda b, h: (layer, 0, 0)),
                  pl.BlockSpec((None, 1, LANES), lambda b, h: (h, 0, 0)),
                  pl.BlockSpec((None, tq // tk + 3, tk, tq), lambda b, h: (h, 0, 0, 0)),
                  col(AQ0), col(AK0), col(AV0)],
        out_specs=pl.BlockSpec((None, seq, LANES), lambda b, h: (h, b, 0)),
        out_shape=jax.ShapeDtypeStruct((A_HEADS, bsz * seq, LANES), BF16),
        scratch_shapes=[pltpu.VMEM((LANES, seq), BF16), pltpu.VMEM((LANES, 2 * tq), F32)]
        + [pltpu.VMEM((tk, 2 * tq), F32) for _ in range(UNROLL_A)],
        compiler_params=_cparams("parallel", "parallel"),
        name="attn_a",
    )(lq1, lk1, lq2, lk2, subg, bmax, bias, proj, proj, proj)


def _b_bias_tiles(rel_bias, tq):
    depth = rel_bias.shape[0]
    assert B_PAD % tq == 0
    r = np.arange(tq)[None, :]
    c = np.arange(tq)[:, None]
    lo = CHUNK * (r // CHUNK)
    u = np.arange(2 * tq - 1)
    tiles = []
    for koff in range(-B_PAD, tq, tq):
        in_band = (c + koff >= lo - B_PAD) & (c + koff < lo + CHUNK)
        idx = np.clip(u - (tq - 1) - koff, -B_MAX_REL, B_MAX_REL) + B_MAX_REL
        bias = _toeplitz(rel_bias[..., idx].astype(F32) * LOG2E, tq, tq)
        bias = jnp.where(jnp.asarray(in_band), bias, NEG_INF)
        tiles.append(bias.reshape(depth, B_HEADS // 2, 2, tq, tq).transpose(0, 1, 3, 2, 4)
                     .reshape(depth, B_HEADS // 2, tq, 2 * tq))
    bmax = jnp.max((rel_bias.astype(F32) * LOG2E).reshape(depth, B_HEADS // 2, -1), axis=2)
    return (jnp.stack(tiles, axis=2),
            jnp.broadcast_to(bmax[:, :, None, None], (depth, B_HEADS // 2, 1, LANES)))


def _attn_b_kernel(bmax_ref, bias_ref, q_ref, k_ref, v_ref, o_ref, vt_ref, *s_refs, tq):
    seq = q_ref.shape[0]
    nq = seq // tq
    max_prev = B_PAD // tq
    lane = lax.broadcasted_iota(jnp.int32, (1, LANES), 1)
    mask1 = (lane < HEAD_DIM).astype(BF16)
    mask2 = (lane >= HEAD_DIM).astype(BF16)
    feat = lax.broadcasted_iota(jnp.int32, (LANES, tq), 0)
    sel = (lax.broadcasted_iota(jnp.int32, (8, LANES), 0)
           == lax.broadcasted_iota(jnp.int32, (8, LANES), 1) // HEAD_DIM).astype(BF16)

    def prepare(b, kn2):
        c0 = pl.multiple_of(b * tq, tq)
        vt_ref[:, pl.ds(c0, tq)] = jnp.transpose(v_ref[pl.ds(c0, tq), :])
        kb = k_ref[pl.ds(c0, tq), :]
        blk = lax.dot_general(sel, kb * kb, (((1,), (1,)), ((), ())), preferred_element_type=F32)
        return jnp.maximum(kn2, blk)

    kn2 = lax.fori_loop(0, nq, prepare, jnp.zeros((8, tq), F32))
    kmax2 = jnp.max(kn2, axis=1, keepdims=True) * NORM_SLACK_A
    kmax = jnp.sqrt(jnp.concatenate([jnp.broadcast_to(kmax2[0:1], (1, tq)),
                                     jnp.broadcast_to(kmax2[1:2], (1, tq))], axis=1))

    def tile_blocks(i, n_prev):
        static = isinstance(i, int)
        r0 = i * tq if static else pl.multiple_of(i * tq, tq)
        qb = q_ref[pl.ds(r0, tq), :]
        qq = jnp.concatenate([qb * mask1, qb * mask2], axis=0)
        starts = [r0 - d * tq for d in range(n_prev, 0, -1)] + [r0]
        blocks = tuple((c0 if static else pl.multiple_of(c0, tq), max_prev - n_prev + j)
                       for j, c0 in enumerate(starts))
        return r0, qq, blocks

    def finish(r0, acc, den):
        o = acc * (1.0 / den)
        o = jnp.where(feat < HEAD_DIM, o[:, :tq], o[:, tq:])
        o_ref[pl.ds(r0, tq), :] = jnp.transpose(o).astype(o_ref.dtype)

    def bounded_tile(i, n_prev):
        r0, qq, blocks = tile_blocks(i, n_prev)
        qn2 = lax.dot_general(jnp.ones((8, LANES), BF16), qq * qq, (((1,), (1,)), ((), ())),
                              preferred_element_type=F32)[0:1]
        shift = jnp.sqrt(qn2 * NORM_SLACK_A) * kmax + (bmax_ref[:, 0:1] + 1.0)
        acc = None
        den = None
        for c0, t in blocks:
            s = lax.dot_general(k_ref[pl.ds(c0, tq), :], qq, (((1,), (1,)), ((), ())),
                                preferred_element_type=F32) + bias_ref[t]
            p = jnp.exp2(s - shift)
            ps = jnp.sum(p, axis=0, keepdims=True)
            pv = jnp.dot(vt_ref[:, pl.ds(c0, tq)], p.astype(BF16), preferred_element_type=F32)
            den = ps if den is None else den + ps
            acc = pv if acc is None else acc + pv
        finish(r0, acc, den)
        return jnp.min(den)

    def exact_tile(i, refs, n_prev):
        r0, qq, blocks = tile_blocks(i, n_prev)
        m = None
        for (c0, t), s_ref in zip(blocks, refs):
            s = lax.dot_general(k_ref[pl.ds(c0, tq), :], qq, (((1,), (1,)), ((), ())),
                                preferred_element_type=F32) + bias_ref[t]
            s_ref[...] = s
            bm = jnp.max(s, axis=0, keepdims=True)
            m = bm if m is None else jnp.maximum(m, bm)
        acc = None
        den = None
        for (c0, t), s_ref in zip(blocks, refs):
            p = jnp.exp2(s_ref[...] - m)
            ps = jnp.sum(p, axis=0, keepdims=True)
            pv = jnp.dot(vt_ref[:, pl.ds(c0, tq)], p.astype(BF16), preferred_element_type=F32)
            den = ps if den is None else den + ps
            acc = pv if acc is None else acc + pv
        finish(r0, acc, den)

    def tiles(idx, n_prev):
        mins = [bounded_tile(i, n_prev) for i in idx]
        for i, l_min in zip(idx, mins):
            @pl.when(jnp.logical_not(l_min > L_MIN_A))
            def _():
                exact_tile(i, s_refs, n_prev)

    for i in range(min(max_prev, nq)):
        tiles([i], i)

    def step(it, carry):
        tiles([max_prev + UNROLL_B * it + u for u in range(UNROLL_B)], max_prev)
        return carry

    n_rest = max(nq - max_prev, 0)
    lax.fori_loop(0, n_rest // UNROLL_B, step, 0)
    for i in range(nq - n_rest % UNROLL_B, nq):
        tiles([i], max_prev)


def _attn_b(proj, bias, bmax, layer, bsz, seq):
    tq = TQ_B
    col = lambda base: pl.BlockSpec((None, seq, LANES), lambda b, h: (base + h, b, 0))
    return pl.pallas_call(
        functools.partial(_attn_b_kernel, tq=tq),
        grid=(bsz, B_HEADS // 2),
        in_specs=[pl.BlockSpec((None, None, 1, LANES), lambda b, h: (layer, h, 0, 0)),
                  pl.BlockSpec((None, None, B_PAD // tq + 1, tq, 2 * tq), lambda b, h: (layer, h, 0, 0, 0)),
                  col(BQ0), col(BK0), col(BV0)],
        out_specs=pl.BlockSpec((None, seq, LANES), lambda b, h: (h, b, 0)),
        out_shape=jax.ShapeDtypeStruct((B_HEADS // 2, bsz * seq, LANES), BF16),
        scratch_shapes=[pltpu.VMEM((LANES, seq), BF16)]
        + [pltpu.VMEM((tq, 2 * tq), F32) for _ in range(B_PAD // tq + 1)],
        compiler_params=_cparams("parallel", "parallel"),
        name="attn_b",
    )(bmax, bias, proj, proj, proj)


def _attn_c_kernel(q_ref, k_ref, v_ref, o_ref, *, tq, group):
    seq = q_ref.shape[0]
    nq = seq // tq
    row = lax.broadcasted_iota(jnp.int32, (tq, tq), 0)
    colm = lax.broadcasted_iota(jnp.int32, (tq, tq), 1)
    tri_strict = (row > colm).astype(BF16)
    before = colm < row

    def block(i, j, carry, acc, diag):
        qb = q_ref[pl.ds(pl.multiple_of(i * tq, tq), tq), :]
        c0 = pl.multiple_of(j * tq, tq)
        kb = k_ref[pl.ds(c0, tq), :]
        vb = v_ref[pl.ds(c0, tq), :]
        z = lax.dot_general(qb, kb, (((1,), (1,)), ((), ())), preferred_element_type=F32)
        lm = -(jnp.maximum(z, 0.0) + jnp.log2(1.0 + jnp.exp2(-jnp.abs(z))))
        if diag:
            lm = jnp.where(before, lm, 0.0)
        lm_b = lm.astype(BF16)
        excl = jnp.dot(lm_b, tri_strict, preferred_element_type=F32)
        logw = (z + lm) + excl + carry
        wgt = jnp.exp2(logw)
        if diag:
            wgt = jnp.where(before, wgt, 0.0)
        acc = acc + jnp.dot(wgt.astype(BF16), vb, preferred_element_type=F32)
        carry = carry + excl[:, 0:1] + lm_b[:, 0:1].astype(F32)
        return carry, acc

    def qgroup(gi, c):
        i0 = gi * group
        sts = [block(i0 + g, i0 + g, jnp.zeros((tq, 1), F32), jnp.zeros((tq, LANES), F32), True)
               for g in range(group)]
        carries = tuple(st[0] for st in sts)
        accs = tuple(st[1] for st in sts)

        def live(carries):
            return functools.reduce(jnp.maximum, [jnp.max(cr) for cr in carries])

        def cond(st):
            t, cmax, _, _ = st
            return jnp.logical_and(t <= i0 + group - 1, cmax > C_SKIP_LOG2)

        def body(st):
            t, _, carries, accs = st
            new = []
            for g in range(group):
                j = i0 + g - t
                cin = jnp.where(j >= 0, carries[g], NEG_INF)
                new.append(block(i0 + g, jnp.maximum(j, 0), cin, accs[g], False))
            carries = tuple(st_[0] for st_ in new)
            return t + 1, live(carries), carries, tuple(st_[1] for st_ in new)

        _, _, _, accs = lax.while_loop(cond, body, (jnp.int32(1), live(carries), carries, accs))
        for g in range(group):
            o_ref[pl.ds(pl.multiple_of((i0 + g) * tq, tq), tq), :] = accs[g].astype(o_ref.dtype)
        return c

    lax.fori_loop(0, nq // group, qgroup, 0)


def _attn_c(proj, bsz, seq):
    tq = min(TQ_C, seq)
    group = min(GROUP_C, seq // tq)
    assert (seq // tq) % group == 0
    col = lambda base: pl.BlockSpec((None, seq, LANES), lambda b, h: (base + h, b, 0))
    return pl.pallas_call(
        functools.partial(_attn_c_kernel, tq=tq, group=group),
        grid=(bsz, C_HEADS),
        in_specs=[col(CQ0), col(CK0), col(CV0)],
        out_specs=pl.BlockSpec((None, seq, LANES), lambda b, h: (h, b, 0)),
        out_shape=jax.ShapeDtypeStruct((C_HEADS, bsz * seq, LANES), BF16),
        compiler_params=_cparams("parallel", "parallel"),
        name="attn_c",
    )(proj, proj, proj)


def _cat_lanes(ref, start, count):
    return jnp.concatenate([ref[start + c] for c in range(count)], axis=1)


def _merge_kernel(x_ref, mod_ref, gate_ref, bg_ref, ya_ref, yb_ref, yc_ref, wb_ref, wo_ref, o_ref):
    blks = D_MODEL // LANES
    merged = None
    for r, y_ref in enumerate((ya_ref, yb_ref, yc_ref)):
        y = _cat_lanes(y_ref, 0, BRANCH_WIDTH // LANES)
        br = jnp.dot(y, wb_ref[r], preferred_element_type=F32)
        pre = _cat_lanes(gate_ref, r * blks, blks).astype(F32) + bg_ref[:, r * D_MODEL:(r + 1) * D_MODEL]
        term = jax.nn.sigmoid(pre) * br
        merged = term if merged is None else merged + term
    out = jnp.dot(merged.astype(BF16), wo_ref[...], preferred_element_type=F32)
    o_ref[...] = x_ref[...] + mod_ref[2:3, :] * out


def _merge(x2d, mod, proj, b_gate, ya, yb, yc, w_branch, w_out, layer, seq):
    t, d = x2d.shape
    tm = min(TM_MERGE, seq)
    ybs = lambda: pl.BlockSpec((BRANCH_WIDTH // LANES, tm, LANES), lambda i: (0, i, 0))
    return pl.pallas_call(
        _merge_kernel,
        grid=(t // tm,),
        in_specs=[
            pl.BlockSpec((tm, d), lambda i: (i, 0)),
            pl.BlockSpec((None, None, 6, d), lambda i: (layer, (i * tm) // seq, 0, 0)),
            pl.BlockSpec((GATE_BLKS, tm, LANES), lambda i: (0, i, 0)),
            pl.BlockSpec((None, 1, GATE_COLS), lambda i: (layer, 0, 0)),
            ybs(), ybs(), ybs(),
            pl.BlockSpec((None, N_BRANCH, BRANCH_WIDTH, d), lambda i: (layer, 0, 0, 0)),
            pl.BlockSpec((None, d, d), lambda i: (layer, 0, 0)),
        ],
        out_specs=pl.BlockSpec((tm, d), lambda i: (i, 0)),
        out_shape=jax.ShapeDtypeStruct((t, d), F32),
        compiler_params=_cparams("parallel"),
        name="merge",
    )(x2d, mod, proj, b_gate, ya, yb, yc, w_branch, w_out)


def _ffn_kernel(x_ref, mod_ref, g_ref, fg_ref, w13_ref, w2_ref, o_ref, *, final):
    dff = w2_ref.shape[0]
    x = x_ref[...]
    h = _norm_mod(x, g_ref[...], mod_ref[3:4, :], mod_ref[4:5, :]).astype(BF16)
    acc = None
    for f0, f1 in FFN_CHUNKS:
        u_gate = jnp.dot(h, w13_ref[:, f0:f1], preferred_element_type=F32)
        u_up = jnp.dot(h, w13_ref[:, dff + f0:dff + f1], preferred_element_type=F32)
        act = (u_gate * jax.nn.sigmoid(u_gate) * u_up).astype(BF16)
        part = jnp.dot(act, w2_ref[f0:f1, :], preferred_element_type=F32)
        acc = part if acc is None else acc + part
    y = x + mod_ref[5:6, :] * acc
    if final:
        y = y * lax.rsqrt(jnp.mean(y * y, axis=-1, keepdims=True) + EPS) * fg_ref[...]
    o_ref[...] = y


def _ffn(x2d, mod, g, final_g, w13, w2, layer, seq, final):
    t, d = x2d.shape
    dff = w2.shape[1]
    assert FFN_CHUNKS[0][0] == 0 and FFN_CHUNKS[-1][1] == dff
    tm = min(TM_FFN, seq)
    resident = lambda shape: pl.BlockSpec((None,) + shape, lambda i: (layer, 0, 0), pipeline_mode=pl.Buffered(1))
    return pl.pallas_call(
        functools.partial(_ffn_kernel, final=final),
        grid=(t // tm,),
        in_specs=[
            pl.BlockSpec((tm, d), lambda i: (i, 0)),
            pl.BlockSpec((None, None, 6, d), lambda i: (layer, (i * tm) // seq, 0, 0)),
            pl.BlockSpec((None, 1, d), lambda i: (layer, 0, 0)),
            pl.BlockSpec((1, d), lambda i: (0, 0)),
            resident((d, 2 * dff)),
            resident((dff, d)),
        ],
        out_specs=pl.BlockSpec((tm, d), lambda i: (i, 0)),
        out_shape=jax.ShapeDtypeStruct((t, d), F32),
        compiler_params=_cparams("parallel"),
        name="ffn",
    )(x2d, mod, g, final_g, w13, w2)


def _prep_w_in(w_in):
    scale = np.ones((IN_COLS,), np.float32)
    scale[0 * BRANCH_WIDTH:1 * BRANCH_WIDTH] = HEAD_DIM ** -0.5 * LOG2E
    scale[3 * BRANCH_WIDTH:4 * BRANCH_WIDTH] = HEAD_DIM ** -0.5 * LOG2E
    scale[6 * BRANCH_WIDTH:7 * BRANCH_WIDTH] = C_HEAD_DIM ** -0.5 * LOG2E
    w = w_in * jnp.asarray(scale)
    return jnp.concatenate([w[..., QKV_COLS:], w[..., :QKV_COLS]], axis=-1).astype(BF16)


def kernel(x, c, w_ada, b_ada, norm1_g, w_in, b_gate, lam_q1, lam_k1, lam_q2, lam_k2, subln_g,
           t5_table, rel_bias_b, w_branch, w_out, norm2_g, w13, w2, final_g):
    bsz, seq, d = x.shape
    depth = w_in.shape[0]
    t = bsz * seq
    assert d == D_MODEL and w_in.shape[2] == IN_COLS and w2.shape[1] == D_FF
    assert seq % (2 * B_PAD) == 0, "sequence tiles assume a multiple of 1024 frames"

    mod = _ada_mod(c, w_ada, b_ada).reshape(depth, bsz, 6, d)
    w_in_b = _prep_w_in(w_in)
    w_branch_b = w_branch.astype(BF16)
    w_out_b = w_out.astype(BF16)
    w13_b = w13.astype(BF16)
    w2_b = w2.astype(BF16)
    tq_a = min(TQ_A, seq // 2)
    a_bias, a_bmax = _a_bias_tiles(t5_table, tq_a, min(TK_A, tq_a))
    b_bias, b_bmax = _b_bias_tiles(rel_bias_b, TQ_B)
    fg = final_g.reshape(1, d)
    row = lambda p: p.reshape(depth, 1, -1)
    g1, g2, bg = row(norm1_g), row(norm2_g), row(b_gate)
    lams = [row(p) for p in (lam_q1, lam_k1, lam_q2, lam_k2)]
    subg = subln_g.reshape(depth, -1, 1)

    x2d = x.reshape(t, d)
    for l in range(depth):
        lam_init = 0.8 - 0.6 * math.exp(-0.3 * l)
        proj = _inproj(x2d, mod, g1, w_in_b, l, seq)
        ya = _attn_a(proj, a_bias, a_bmax, *lams, subg, l, bsz, seq, lam_init)
        yb = _attn_b(proj, b_bias, b_bmax, l, bsz, seq)
        yc = _attn_c(proj, bsz, seq)
        x2d = _merge(x2d, mod, proj, bg, ya, yb, yc, w_branch_b, w_out_b, l, seq)
        x2d = _ffn(x2d, mod, g2, fg, w13_b, w2_b, l, seq, final=(l == depth - 1))
    return x2d.reshape(bsz, seq, d)
```

```python
import functools
import math

import numpy as np
import jax
import jax.numpy as jnp
from jax import lax
from jax.experimental import pallas as pl
from jax.experimental.pallas import tpu as pltpu

F32 = jnp.float32
BF16 = jnp.bfloat16

D_MODEL = 1024
CHUNK = 64
HEAD_DIM = 64
BRANCH_WIDTH = 512
N_BRANCH = 3
A_HEADS = 4
B_HEADS = 8
C_HEADS = 4
C_HEAD_DIM = 128
B_LEFT_CHUNKS = 8
B_PAD = B_LEFT_CHUNKS * CHUNK
B_MAX_REL = 128
T5_BUCKETS = 32
T5_MAX_DIST = 128
D_FF = 2816
QKV_COLS = 3 * N_BRANCH * BRANCH_WIDTH
GATE_COLS = N_BRANCH * D_MODEL
IN_COLS = QKV_COLS + GATE_COLS
NEG_INF = -1e30
EPS = 1e-6
LOG2E = math.log2(math.e)

LANES = 128
GATE_BLKS = GATE_COLS // LANES
AQ0, AK0, AV0 = GATE_BLKS, GATE_BLKS + 4, GATE_BLKS + 8
BQ0, BK0, BV0 = GATE_BLKS + 12, GATE_BLKS + 16, GATE_BLKS + 20
CQ0, CK0, CV0 = GATE_BLKS + 24, GATE_BLKS + 28, GATE_BLKS + 32

TM_PROJ = 512
TN_PROJ = 1536
TM_MERGE = 512
TM_FFN = 512
FFN_CHUNKS = ((0, 1536), (1536, 2816))
TQ_A = 1024
TK_A = 512
TP_A = 512
UNROLL_A = 2
NORM_SLACK_A = 1.02
L_MIN_A = 2.0 ** -100
TQ_B = 512
UNROLL_B = 3
TQ_C = 256
GROUP_C = 8
C_SKIP_LOG2 = -150.0
VMEM_LIMIT = 56 * 1024 * 1024


def _cparams(*sem):
    return pltpu.CompilerParams(dimension_semantics=sem, vmem_limit_bytes=VMEM_LIMIT)


def _ada_kernel(c_ref, w_ref, b_ref, o_ref):
    c = c_ref[...]
    cs = c * jax.nn.sigmoid(c)
    o_ref[...] = jnp.dot(cs, w_ref[...], preferred_element_type=F32,
                         precision=lax.Precision.HIGHEST) + b_ref[...]


def _ada_mod(c, w_ada, b_ada):
    depth, d, e = w_ada.shape
    bsz = c.shape[0]
    nblk = e // d
    return pl.pallas_call(
        _ada_kernel,
        grid=(depth, nblk),
        in_specs=[
            pl.BlockSpec((bsz, d), lambda l, j: (0, 0)),
            pl.BlockSpec((None, d, d), lambda l, j: (l, 0, j)),
            pl.BlockSpec((None, 1, d), lambda l, j: (l, 0, j)),
        ],
        out_specs=pl.BlockSpec((None, bsz, d), lambda l, j: (l, 0, j)),
        out_shape=jax.ShapeDtypeStruct((depth, bsz, e), F32),
        compiler_params=_cparams("arbitrary", "arbitrary"),
        name="ada_mod",
    )(c, w_ada, b_ada.reshape(depth, 1, e))


def _norm_mod(x, g, shift, scale):
    ms = jnp.mean(x * x, axis=-1, keepdims=True)
    y = x * lax.rsqrt(ms + EPS) * g
    return y * (1.0 + scale) + shift


def _inproj_kernel(x_ref, mod_ref, g_ref, w_ref, o_ref):
    h = _norm_mod(x_ref[...], g_ref[...], mod_ref[0:1, :], mod_ref[1:2, :]).astype(BF16)
    n = w_ref.shape[1]
    for c0 in range(0, n, TN_PROJ):
        res = jnp.dot(h, w_ref[:, c0:c0 + TN_PROJ], preferred_element_type=F32)
        for cb in range(TN_PROJ // LANES):
            o_ref[c0 // LANES + cb] = res[:, cb * LANES:(cb + 1) * LANES].astype(o_ref.dtype)


def _inproj(x2d, mod, g, w, layer, seq):
    t, d = x2d.shape
    n = w.shape[2]
    assert n % TN_PROJ == 0
    tm = min(TM_PROJ, seq)
    return pl.pallas_call(
        _inproj_kernel,
        grid=(t // tm,),
        in_specs=[
            pl.BlockSpec((tm, d), lambda i: (i, 0)),
            pl.BlockSpec((None, None, 6, d), lambda i: (layer, (i * tm) // seq, 0, 0)),
            pl.BlockSpec((None, 1, d), lambda i: (layer, 0, 0)),
            pl.BlockSpec((None, d, n), lambda i: (layer, 0, 0), pipeline_mode=pl.Buffered(1)),
        ],
        out_specs=pl.BlockSpec((n // LANES, tm, LANES), lambda i: (0, i, 0)),
        out_shape=jax.ShapeDtypeStruct((n // LANES, t, LANES), BF16),
        compiler_params=_cparams("parallel"),
        name="in_proj",
    )(x2d, mod, g, w)


def _t5_bucket_np(rel):
    nb = T5_BUCKETS // 2
    max_exact = nb // 2
    ret = np.where(rel > 0, nb, 0)
    n = np.abs(rel)
    nf = np.maximum(n, 1).astype(np.float32)
    scaled = (np.log(nf / np.float32(max_exact)) / np.float32(math.log(T5_MAX_DIST / max_exact))
              * np.float32(nb - max_exact))
    large = max_exact + scaled.astype(np.int32)
    large = np.minimum(large, nb - 1)
    return (ret + np.where(n < max_exact, n, large)).astype(np.int32)


def _toeplitz(vec, rows, cols):
    length = rows + cols - 1
    assert vec.shape[-1] == length
    lead = vec.shape[:-1]
    ext = jnp.concatenate([vec, jnp.zeros(lead + (1,), vec.dtype)], axis=-1)
    flat = jnp.broadcast_to(ext[..., None, :], lead + (rows, length + 1)).reshape(lead + (rows * (length + 1),))
    return flat[..., :rows * length].reshape(lead + (rows, length))[..., rows - 1:rows - 1 + cols]


def _a_bias_tiles(t5_table, tq, tk):
    r = np.arange(tq)[None, :]
    c = np.arange(tk)[:, None]
    far_bucket = _t5_bucket_np((c - 2 * tk) - r)
    assert (far_bucket == far_bucket[0, 0]).all() and far_bucket[0, 0] == _t5_bucket_np(np.array(-10 * tq))
    far = t5_table[int(far_bucket[0, 0])].astype(F32)
    tiles = [jnp.zeros((A_HEADS, tk, tq), F32)]
    bmax = jnp.zeros((A_HEADS,), F32)
    u = np.arange(tk + tq - 1)
    for koff in range(-tk, tq, tk):
        vec = t5_table[_t5_bucket_np(koff + tk - 1 - u)].astype(F32).T
        vec = (vec - far[:, None]) * LOG2E
        bmax = jnp.maximum(bmax, jnp.max(vec, axis=1))
        bias = _toeplitz(vec, tk, tq)
        allowed = ((c + koff) // CHUNK) <= (r // CHUNK)
        tiles.append(jnp.where(jnp.asarray(allowed)[None], bias, NEG_INF))
    tiles.append(jnp.full((A_HEADS, tk, tq), NEG_INF, F32))
    return jnp.stack(tiles, axis=1), jnp.broadcast_to(bmax[:, None, None], (A_HEADS, 1, LANES))


def _attn_a_kernel(lq1_ref, lk1_ref, lq2_ref, lk2_ref, subg_ref, bmax_ref, bias_ref, q_ref, k_ref, v_ref,
                   o_ref, vt_ref, acc_ref, *s_refs, tq, tk, lam_init):
    seq = q_ref.shape[0]
    nq = seq // tq
    nkb = seq // tk
    per_q = tq // tk
    unroll = len(s_refs)
    tp = TP_A
    lam = (jnp.exp(jnp.sum(lq1_ref[...] * lk1_ref[...], axis=-1, keepdims=True))
           - jnp.exp(jnp.sum(lq2_ref[...] * lk2_ref[...], axis=-1, keepdims=True)) + lam_init)
    lane = lax.broadcasted_iota(jnp.int32, (1, LANES), 1)
    mask1 = (lane < HEAD_DIM).astype(BF16)
    mask2 = (lane >= HEAD_DIM).astype(BF16)
    subg = subg_ref[...] * (1.0 - lam_init)
    sel = (lax.broadcasted_iota(jnp.int32, (8, LANES), 0)
           == lax.broadcasted_iota(jnp.int32, (8, LANES), 1) // HEAD_DIM).astype(BF16)

    def prepare(b, kn2):
        c0 = pl.multiple_of(b * tk, tk)
        vt_ref[:, pl.ds(c0, tk)] = jnp.transpose(v_ref[pl.ds(c0, tk), :])
        kb = k_ref[pl.ds(c0, tk), :]
        blk = lax.dot_general(sel, kb * kb, (((1,), (1,)), ((), ())), preferred_element_type=F32)
        return jnp.maximum(kn2, blk)

    kn2 = lax.fori_loop(0, nkb, prepare, jnp.zeros((8, tk), F32))
    kmax2 = jnp.max(kn2, axis=1, keepdims=True) * NORM_SLACK_A
    kmax = jnp.sqrt(jnp.concatenate([jnp.broadcast_to(kmax2[0:1], (1, tq)),
                                     jnp.broadcast_to(kmax2[1:2], (1, tq))], axis=1))

    def finish(i, acc, l):
        o = acc * (1.0 / l)
        d = o[:, :tq] - lam * o[:, tq:]
        y = d * lax.rsqrt(jnp.mean(d * d, axis=0, keepdims=True) + EPS) * subg
        o_ref[pl.ds(pl.multiple_of(i * tq, tq), tq), :] = jnp.transpose(y).astype(o_ref.dtype)

    def load_q(i):
        qb = q_ref[pl.ds(pl.multiple_of(i * tq, tq), tq), :]
        return jnp.concatenate([qb * mask1, qb * mask2], axis=0)

    def steps(i, step, st):
        n_plain = jnp.maximum(per_q * i - 1, 0) // unroll
        n_steps = (per_q * (i + 1) + unroll - 1) // unroll
        st = lax.fori_loop(0, n_plain, lambda it, s_: step(it, s_, False), st)
        return lax.fori_loop(n_plain, n_steps, lambda it, s_: step(it, s_, True), st)

    def bounded_pass(i):
        qq = load_q(i)
        qn2 = lax.dot_general(jnp.ones((8, LANES), BF16), qq * qq, (((1,), (1,)), ((), ())),
                              preferred_element_type=F32)[0:1]
        shift = jnp.sqrt(qn2 * NORM_SLACK_A) * kmax + (bmax_ref[:, 0:1] + 1.0)
        acc_ref[...] = jnp.zeros(acc_ref.shape, F32)

        def block(js, l, bias, late_queries_only=False):
            c0 = pl.multiple_of(js * tk, tk)
            kb = k_ref[pl.ds(c0, tk), :]
            vt = vt_ref[:, pl.ds(c0, tk)]
            sums = []
            for m0 in (0, tq):
                cols = slice(m0 + tq // 2, m0 + tq) if late_queries_only else slice(m0, m0 + tq)
                s = lax.dot_general(kb, qq[cols], (((1,), (1,)), ((), ())), preferred_element_type=F32)
                if bias is not None:
                    s = s + jnp.concatenate([bias, bias], axis=1)[:, cols]
                p = jnp.exp2(s - shift[:, cols])
                ps = jnp.sum(p, axis=0, keepdims=True)
                if late_queries_only:
                    ps = jnp.concatenate([jnp.zeros((1, tq // 2), F32), ps], axis=1)
                sums.append(ps)
                acc_ref[:, cols] += jnp.dot(vt, p.astype(BF16), preferred_element_type=F32)
            return l + jnp.concatenate(sums, axis=1)

        def plain_blocks(first, count, l):
            for u in range(count):
                l = block(first + u, l, None)
            return l

        n_pairs = jnp.maximum(i - 1, 0)
        l = lax.fori_loop(0, n_pairs // 2, lambda it, l: plain_blocks(4 * it, 4, l),
                          jnp.zeros((1, 2 * tq), F32))
        l = lax.cond(n_pairs % 2 == 1, lambda l: plain_blocks(2 * n_pairs - 2, 2, l), lambda l: l, l)

        def previous_blocks(l):
            return block(2 * i - 1, block(2 * i - 2, l, None), bias_ref[1])

        l = lax.cond(i > 0, previous_blocks, lambda l: l, l)
        l = block(2 * i, l, bias_ref[2])
        l = block(2 * i + 1, l, bias_ref[3], late_queries_only=True)
        finish(i, acc_ref[...], l)
        return jnp.min(l)

    def running_max_pass(i):
        qq = load_q(i)
        acc_ref[...] = jnp.zeros(acc_ref.shape, F32)

        def block(js, st, with_bias, s_ref):
            m, l = st
            c0 = pl.multiple_of(jnp.minimum(js, nkb - 1) * tk, tk)
            kb = k_ref[pl.ds(c0, tk), :]
            s = lax.dot_general(kb, qq, (((1,), (1,)), ((), ())), preferred_element_type=F32)
            if with_bias:
                bias = bias_ref[jnp.clip(js - per_q * i + 2, 0, per_q + 2)]
                s = s + jnp.concatenate([bias, bias], axis=1)
            s_ref[...] = s
            m_new = jnp.maximum(m, jnp.max(s, axis=0, keepdims=True))
            alpha = jnp.exp2(m - m_new)
            sums = []
            for ct in range(2 * tq // tp):
                cols = slice(ct * tp, (ct + 1) * tp)
                p = jnp.exp2(s_ref[:, cols] - m_new[:, cols])
                sums.append(jnp.sum(p, axis=0, keepdims=True))
                vt = vt_ref[:, pl.ds(c0, tk)]
                acc_ref[:, cols] = (alpha[:, cols] * acc_ref[:, cols]
                                    + jnp.dot(vt, p.astype(BF16), preferred_element_type=F32))
            return m_new, alpha * l + jnp.concatenate(sums, axis=1)

        def step(it, st, with_bias):
            for u in range(unroll):
                st = block(it * unroll + u, st, with_bias, s_refs[u])
            return st

        _, l = steps(i, step, (jnp.full((1, 2 * tq), NEG_INF, F32), jnp.zeros((1, 2 * tq), F32)))
        finish(i, acc_ref[...], l)

    def qtile(i, carry):
        l_min = bounded_pass(i)

        @pl.when(jnp.logical_not(l_min > L_MIN_A))
        def _():
            running_max_pass(i)

        return carry

    lax.fori_loop(0, nq, qtile, 0)


def _attn_a(proj, bias, bmax, lq1, lk1, lq2, lk2, subg, layer, bsz, seq, lam_init):
    tq = min(TQ_A, seq // 2)
    tk = min(TK_A, tq)
    assert tk == TP_A and tq == 2 * tk and UNROLL_A == 2
    vec = lambda n: pl.BlockSpec((None, 1, n), lambda b, h: (layer, 0, 0))
    col = lambda base: pl.BlockSpec((None, seq, LANES), lambda b, h: (base + h, b, 0))
    return pl.pallas_call(
        functools.partial(_attn_a_kernel, tq=tq, tk=tk, lam_init=lam_init),
        grid=(bsz, A_HEADS),
        in_specs=[vec(HEAD_DIM), vec(HEAD_DIM), vec(HEAD_DIM), vec(HEAD_DIM),
                  pl.BlockSpec((None, 2 * HEAD_DIM, 1), lambda b, h: (layer, 0, 0)),
                  pl.BlockSpec((None, 1, LANES), lambda b, h: (h, 0, 0)),
                  pl.BlockSpec((None, tq // tk + 3, tk, tq), lambda b, h: (h, 0, 0, 0)),
                  col(AQ0), col(AK0), col(AV0)],
        out_specs=pl.BlockSpec((None, seq, LANES), lambda b, h: (h, b, 0)),
        out_shape=jax.ShapeDtypeStruct((A_HEADS, bsz * seq, LANES), BF16),
        scratch_shapes=[pltpu.VMEM((LANES, seq), BF16), pltpu.VMEM((LANES, 2 * tq), F32)]
        + [pltpu.VMEM((tk, 2 * tq), F32) for _ in range(UNROLL_A)],
        compiler_params=_cparams("parallel", "parallel"),
        name="attn_a",
    )(lq1, lk1, lq2, lk2, subg, bmax, bias, proj, proj, proj)


def _b_bias_tiles(rel_bias, tq):
    depth = rel_bias.shape[0]
    assert B_PAD % tq == 0
    r = np.arange(tq)[None, :]
    c = np.arange(tq)[:, None]
    lo = CHUNK * (r // CHUNK)
    u = np.arange(2 * tq - 1)
    tiles = []
    for koff in range(-B_PAD, tq, tq):
        in_band = (c + koff >= lo - B_PAD) & (c + koff < lo + CHUNK)
        idx = np.clip(u - (tq - 1) - koff, -B_MAX_REL, B_MAX_REL) + B_MAX_REL
        bias = _toeplitz(rel_bias[..., idx].astype(F32) * LOG2E, tq, tq)
        bias = jnp.where(jnp.asarray(in_band), bias, NEG_INF)
        tiles.append(bias.reshape(depth, B_HEADS // 2, 2, tq, tq).transpose(0, 1, 3, 2, 4)
                     .reshape(depth, B_HEADS // 2, tq, 2 * tq))
    bmax = jnp.max((rel_bias.astype(F32) * LOG2E).reshape(depth, B_HEADS // 2, -1), axis=2)
    return (jnp.stack(tiles, axis=2),
            jnp.broadcast_to(bmax[:, :, None, None], (depth, B_HEADS // 2, 1, LANES)))


def _attn_b_kernel(bmax_ref, bias_ref, q_ref, k_ref, v_ref, o_ref, vt_ref, *s_refs, tq):
    seq = q_ref.shape[0]
    nq = seq // tq
    max_prev = B_PAD // tq
    lane = lax.broadcasted_iota(jnp.int32, (1, LANES), 1)
    mask1 = (lane < HEAD_DIM).astype(BF16)
    mask2 = (lane >= HEAD_DIM).astype(BF16)
    feat = lax.broadcasted_iota(jnp.int32, (LANES, tq), 0)
    sel = (lax.broadcasted_iota(jnp.int32, (8, LANES), 0)
           == lax.broadcasted_iota(jnp.int32, (8, LANES), 1) // HEAD_DIM).astype(BF16)

    def prepare(b, kn2):
        c0 = pl.multiple_of(b * tq, tq)
        vt_ref[:, pl.ds(c0, tq)] = jnp.transpose(v_ref[pl.ds(c0, tq), :])
        kb = k_ref[pl.ds(c0, tq), :]
        blk = lax.dot_general(sel, kb * kb, (((1,), (1,)), ((), ())), preferred_element_type=F32)
        return jnp.maximum(kn2, blk)

    kn2 = lax.fori_loop(0, nq, prepare, jnp.zeros((8, tq), F32))
    kmax2 = jnp.max(kn2, axis=1, keepdims=True) * NORM_SLACK_A
    kmax = jnp.sqrt(jnp.concatenate([jnp.broadcast_to(kmax2[0:1], (1, tq)),
                                     jnp.broadcast_to(kmax2[1:2], (1, tq))], axis=1))

    def tile_blocks(i, n_prev):
        static = isinstance(i, int)
        r0 = i * tq if static else pl.multiple_of(i * tq, tq)
        qb = q_ref[pl.ds(r0, tq), :]
        qq = jnp.concatenate([qb * mask1, qb * mask2], axis=0)
        starts = [r0 - d * tq for d in range(n_prev, 0, -1)] + [r0]
        blocks = tuple((c0 if static else pl.multiple_of(c0, tq), max_prev - n_prev + j)
                       for j, c0 in enumerate(starts))
        return r0, qq, blocks

    def finish(r0, acc, den):
        o = acc * (1.0 / den)
        o = jnp.where(feat < HEAD_DIM, o[:, :tq], o[:, tq:])
        o_ref[pl.ds(r0, tq), :] = jnp.transpose(o).astype(o_ref.dtype)

    def bounded_tile(i, n_prev):
        r0, qq, blocks = tile_blocks(i, n_prev)
        qn2 = lax.dot_general(jnp.ones((8, LANES), BF16), qq * qq, (((1,), (1,)), ((), ())),
                              preferred_element_type=F32)[0:1]
        shift = jnp.sqrt(qn2 * NORM_SLACK_A) * kmax + (bmax_ref[:, 0:1] + 1.0)
        acc = None
        den = None
        for c0, t in blocks:
            s = lax.dot_general(k_ref[pl.ds(c0, tq), :], qq, (((1,), (1,)), ((), ())),
                                preferred_element_type=F32) + bias_ref[t]
            p = jnp.exp2(s - shift)
            ps = jnp.sum(p, axis=0, keepdims=True)
            pv = jnp.dot(vt_ref[:, pl.ds(c0, tq)], p.astype(BF16), preferred_element_type=F32)
            den = ps if den is None else den + ps
            acc = pv if acc is None else acc + pv
        finish(r0, acc, den)
        return jnp.min(den)

    def exact_tile(i, refs, n_prev):
        r0, qq, blocks = tile_blocks(i, n_prev)
        m = None
        for (c0, t), s_ref in zip(blocks, refs):
            s = lax.dot_general(k_ref[pl.ds(c0, tq), :], qq, (((1,), (1,)), ((), ())),
                                preferred_element_type=F32) + bias_ref[t]
            s_ref[...] = s
            bm = jnp.max(s, axis=0, keepdims=True)
            m = bm if m is None else jnp.maximum(m, bm)
        acc = None
        den = None
        for (c0, t), s_ref in zip(blocks, refs):
            p = jnp.exp2(s_ref[...] - m)
            ps = jnp.sum(p, axis=0, keepdims=True)
            pv = jnp.dot(vt_ref[:, pl.ds(c0, tq)], p.astype(BF16), preferred_element_type=F32)
            den = ps if den is None else den + ps
            acc = pv if acc is None else acc + pv
        finish(r0, acc, den)

    def tiles(idx, n_prev):
        mins = [bounded_tile(i, n_prev) for i in idx]
        for i, l_min in zip(idx, mins):
            @pl.when(jnp.logical_not(l_min > L_MIN_A))
            def _():
                exact_tile(i, s_refs, n_prev)

    for i in range(min(max_prev, nq)):
        tiles([i], i)

    def step(it, carry):
        tiles([max_prev + UNROLL_B * it + u for u in range(UNROLL_B)], max_prev)
        return carry

    n_rest = max(nq - max_prev, 0)
    lax.fori_loop(0, n_rest // UNROLL_B, step, 0)
    for i in range(nq - n_rest % UNROLL_B, nq):
        tiles([i], max_prev)


def _attn_b(proj, bias, bmax, layer, bsz, seq):
    tq = TQ_B
    col = lambda base: pl.BlockSpec((None, seq, LANES), lambda b, h: (base + h, b, 0))
    return pl.pallas_call(
        functools.partial(_attn_b_kernel, tq=tq),
        grid=(bsz, B_HEADS // 2),
        in_specs=[pl.BlockSpec((None, None, 1, LANES), lambda b, h: (layer, h, 0, 0)),
                  pl.BlockSpec((None, None, B_PAD // tq + 1, tq, 2 * tq), lambda b, h: (layer, h, 0, 0, 0)),
                  col(BQ0), col(BK0), col(BV0)],
        out_specs=pl.BlockSpec((None, seq, LANES), lambda b, h: (h, b, 0)),
        out_shape=jax.ShapeDtypeStruct((B_HEADS // 2, bsz * seq, LANES), BF16),
        scratch_shapes=[pltpu.VMEM((LANES, seq), BF16)]
        + [pltpu.VMEM((tq, 2 * tq), F32) for _ in range(B_PAD // tq + 1)],
        compiler_params=_cparams("parallel", "parallel"),
        name="attn_b",
    )(bmax, bias, proj, proj, proj)


def _attn_c_kernel(q_ref, k_ref, v_ref, o_ref, *, tq, group):
    seq = q_ref.shape[0]
    nq = seq // tq
    row = lax.broadcasted_iota(jnp.int32, (tq, tq), 0)
    colm = lax.broadcasted_iota(jnp.int32, (tq, tq), 1)
    tri_strict = (row > colm).astype(BF16)
    before = colm < row

    def block(i, j, carry, acc, diag):
        qb = q_ref[pl.ds(pl.multiple_of(i * tq, tq), tq), :]
        c0 = pl.multiple_of(j * tq, tq)
        kb = k_ref[pl.ds(c0, tq), :]
        vb = v_ref[pl.ds(c0, tq), :]
        z = lax.dot_general(qb, kb, (((1,), (1,)), ((), ())), preferred_element_type=F32)
        lm = -(jnp.maximum(z, 0.0) + jnp.log2(1.0 + jnp.exp2(-jnp.abs(z))))
        if diag:
            lm = jnp.where(before, lm, 0.0)
        lm_b = lm.astype(BF16)
        excl = jnp.dot(lm_b, tri_strict, preferred_element_type=F32)
        logw = (z + lm) + excl + carry
        wgt = jnp.exp2(logw)
        if diag:
            wgt = jnp.where(before, wgt, 0.0)
        acc = acc + jnp.dot(wgt.astype(BF16), vb, preferred_element_type=F32)
        carry = carry + excl[:, 0:1] + lm_b[:, 0:1].astype(F32)
        return carry, acc

    def qgroup(gi, c):
        i0 = gi * group
        sts = [block(i0 + g, i0 + g, jnp.zeros((tq, 1), F32), jnp.zeros((tq, LANES), F32), True)
               for g in range(group)]
        carries = tuple(st[0] for st in sts)
        accs = tuple(st[1] for st in sts)

        def live(carries):
            return functools.reduce(jnp.maximum, [jnp.max(cr) for cr in carries])

        def cond(st):
            t, cmax, _, _ = st
            return jnp.logical_and(t <= i0 + group - 1, cmax > C_SKIP_LOG2)

        def body(st):
            t, _, carries, accs = st
            new = []
            for g in range(group):
                j = i0 + g - t
                cin = jnp.where(j >= 0, carries[g], NEG_INF)
                new.append(block(i0 + g, jnp.maximum(j, 0), cin, accs[g], False))
            carries = tuple(st_[0] for st_ in new)
            return t + 1, live(carries), carries, tuple(st_[1] for st_ in new)

        _, _, _, accs = lax.while_loop(cond, body, (jnp.int32(1), live(carries), carries, accs))
        for g in range(group):
            o_ref[pl.ds(pl.multiple_of((i0 + g) * tq, tq), tq), :] = accs[g].astype(o_ref.dtype)
        return c

    lax.fori_loop(0, nq // group, qgroup, 0)


def _attn_c(proj, bsz, seq):
    tq = min(TQ_C, seq)
    group = min(GROUP_C, seq // tq)
    assert (seq // tq) % group == 0
    col = lambda base: pl.BlockSpec((None, seq, LANES), lambda b, h: (base + h, b, 0))
    return pl.pallas_call(
        functools.partial(_attn_c_kernel, tq=tq, group=group),
        grid=(bsz, C_HEADS),
        in_specs=[col(CQ0), col(CK0), col(CV0)],
        out_specs=pl.BlockSpec((None, seq, LANES), lambda b, h: (h, b, 0)),
        out_shape=jax.ShapeDtypeStruct((C_HEADS, bsz * seq, LANES), BF16),
        compiler_params=_cparams("parallel", "parallel"),
        name="attn_c",
    )(proj, proj, proj)


def _cat_lanes(ref, start, count):
    return jnp.concatenate([ref[start + c] for c in range(count)], axis=1)


def _merge_kernel(x_ref, mod_ref, gate_ref, bg_ref, ya_ref, yb_ref, yc_ref, wb_ref, wo_ref, o_ref):
    blks = D_MODEL // LANES
    merged = None
    for r, y_ref in enumerate((ya_ref, yb_ref, yc_ref)):
        y = _cat_lanes(y_ref, 0, BRANCH_WIDTH // LANES)
        br = jnp.dot(y, wb_ref[r], preferred_element_type=F32)
        pre = _cat_lanes(gate_ref, r * blks, blks).astype(F32) + bg_ref[:, r * D_MODEL:(r + 1) * D_MODEL]
        term = jax.nn.sigmoid(pre) * br
        merged = term if merged is None else merged + term
    out = jnp.dot(merged.astype(BF16), wo_ref[...], preferred_element_type=F32)
    o_ref[...] = x_ref[...] + mod_ref[2:3, :] * out


def _merge(x2d, mod, proj, b_gate, ya, yb, yc, w_branch, w_out, layer, seq):
    t, d = x2d.shape
    tm = min(TM_MERGE, seq)
    ybs = lambda: pl.BlockSpec((BRANCH_WIDTH // LANES, tm, LANES), lambda i: (0, i, 0))
    return pl.pallas_call(
        _merge_kernel,
        grid=(t // tm,),
        in_specs=[
            pl.BlockSpec((tm, d), lambda i: (i, 0)),
            pl.BlockSpec((None, None, 6, d), lambda i: (layer, (i * tm) // seq, 0, 0)),
            pl.BlockSpec((GATE_BLKS, tm, LANES), lambda i: (0, i, 0)),
            pl.BlockSpec((None, 1, GATE_COLS), lambda i: (layer, 0, 0)),
            ybs(), ybs(), ybs(),
            pl.BlockSpec((None, N_BRANCH, BRANCH_WIDTH, d), lambda i: (layer, 0, 0, 0)),
            pl.BlockSpec((None, d, d), lambda i: (layer, 0, 0)),
        ],
        out_specs=pl.BlockSpec((tm, d), lambda i: (i, 0)),
        out_shape=jax.ShapeDtypeStruct((t, d), F32),
        compiler_params=_cparams("parallel"),
        name="merge",
    )(x2d, mod, proj, b_gate, ya, yb, yc, w_branch, w_out)


def _ffn_kernel(x_ref, mod_ref, g_ref, fg_ref, w13_ref, w2_ref, o_ref, *, final):
    dff = w2_ref.shape[0]
    x = x_ref[...]
    h = _norm_mod(x, g_ref[...], mod_ref[3:4, :], mod_ref[4:5, :]).astype(BF16)
    acc = None
    for f0, f1 in FFN_CHUNKS:
        u_gate = jnp.dot(h, w13_ref[:, f0:f1], preferred_element_type=F32)
        u_up = jnp.dot(h, w13_ref[:, dff + f0:dff + f1], preferred_element_type=F32)
        act = (u_gate * jax.nn.sigmoid(u_gate) * u_up).astype(BF16)
        part = jnp.dot(act, w2_ref[f0:f1, :], preferred_element_type=F32)
        acc = part if acc is None else acc + part
    y = x + mod_ref[5:6, :] * acc
    if final:
        y = y * lax.rsqrt(jnp.mean(y * y, axis=-1, keepdims=True) + EPS) * fg_ref[...]
    o_ref[...] = y


def _ffn(x2d, mod, g, final_g, w13, w2, layer, seq, final):
    t, d = x2d.shape
    dff = w2.shape[1]
    assert FFN_CHUNKS[0][0] == 0 and FFN_CHUNKS[-1][1] == dff
    tm = min(TM_FFN, seq)
    resident = lambda shape: pl.BlockSpec((None,) + shape, lambda i: (layer, 0, 0), pipeline_mode=pl.Buffered(1))
    return pl.pallas_call(
        functools.partial(_ffn_kernel, final=final),
        grid=(t // tm,),
        in_specs=[
            pl.BlockSpec((tm, d), lambda i: (i, 0)),
            pl.BlockSpec((None, None, 6, d), lambda i: (layer, (i * tm) // seq, 0, 0)),
            pl.BlockSpec((None, 1, d), lambda i: (layer, 0, 0)),
            pl.BlockSpec((1, d), lambda i: (0, 0)),
            resident((d, 2 * dff)),
            resident((dff, d)),
        ],
        out_specs=pl.BlockSpec((tm, d), lambda i: (i, 0)),
        out_shape=jax.ShapeDtypeStruct((t, d), F32),
        compiler_params=_cparams("parallel"),
        name="ffn",
    )(x2d, mod, g, final_g, w13, w2)


def _prep_w_in(w_in):
    scale = np.ones((IN_COLS,), np.float32)
    scale[0 * BRANCH_WIDTH:1 * BRANCH_WIDTH] = HEAD_DIM ** -0.5 * LOG2E
    scale[3 * BRANCH_WIDTH:4 * BRANCH_WIDTH] = HEAD_DIM ** -0.5 * LOG2E
    scale[6 * BRANCH_WIDTH:7 * BRANCH_WIDTH] = C_HEAD_DIM ** -0.5 * LOG2E
    w = w_in * jnp.asarray(scale)
    return jnp.concatenate([w[..., QKV_COLS:], w[..., :QKV_COLS]], axis=-1).astype(BF16)


def kernel(x, c, w_ada, b_ada, norm1_g, w_in, b_gate, lam_q1, lam_k1, lam_q2, lam_k2, subln_g,
           t5_table, rel_bias_b, w_branch, w_out, norm2_g, w13, w2, final_g):
    bsz, seq, d = x.shape
    depth = w_in.shape[0]
    t = bsz * seq
    assert d == D_MODEL and w_in.shape[2] == IN_COLS and w2.shape[1] == D_FF
    assert seq % (2 * B_PAD) == 0, "sequence tiles assume a multiple of 1024 frames"

    mod = _ada_mod(c, w_ada, b_ada).reshape(depth, bsz, 6, d)
    w_in_b = _prep_w_in(w_in)
    w_branch_b = w_branch.astype(BF16)
    w_out_b = w_out.astype(BF16)
    w13_b = w13.astype(BF16)
    w2_b = w2.astype(BF16)
    tq_a = min(TQ_A, seq // 2)
    a_bias, a_bmax = _a_bias_tiles(t5_table, tq_a, min(TK_A, tq_a))
    b_bias, b_bmax = _b_bias_tiles(rel_bias_b, TQ_B)
    fg = final_g.reshape(1, d)
    row = lambda p: p.reshape(depth, 1, -1)
    g1, g2, bg = row(norm1_g), row(norm2_g), row(b_gate)
    lams = [row(p) for p in (lam_q1, lam_k1, lam_q2, lam_k2)]
    subg = subln_g.reshape(depth, -1, 1)

    x2d = x.reshape(t, d)
    for l in range(depth):
        lam_init = 0.8 - 0.6 * math.exp(-0.3 * l)
        proj = _inproj(x2d, mod, g1, w_in_b, l, seq)
        ya = _attn_a(proj, a_bias, a_bmax, *lams, subg, l, bsz, seq, lam_init)
        yb = _attn_b(proj, b_bias, b_bmax, l, bsz, seq)
        yc = _attn_c(proj, bsz, seq)
        x2d = _merge(x2d, mod, proj, bg, ya, yb, yc, w_branch_b, w_out_b, l, seq)
        x2d = _ffn(x2d, mod, g2, fg, w13_b, w2_b, l, seq, final=(l == depth - 1))
    return x2d.reshape(bsz, seq, d)
```

```python
import functools
import math

import numpy as np
import jax
import jax.numpy as jnp
from jax import lax
from jax.experimental import pallas as pl
from jax.experimental.pallas import tpu as pltpu

F32 = jnp.float32
BF16 = jnp.bfloat16

D_MODEL = 1024
CHUNK = 64
HEAD_DIM = 64
BRANCH_WIDTH = 512
N_BRANCH = 3
A_HEADS = 4
B_HEADS = 8
C_HEADS = 4
C_HEAD_DIM = 128
B_LEFT_CHUNKS = 8
B_PAD = B_LEFT_CHUNKS * CHUNK
B_MAX_REL = 128
T5_BUCKETS = 32
T5_MAX_DIST = 128
D_FF = 2816
QKV_COLS = 3 * N_BRANCH * BRANCH_WIDTH
GATE_COLS = N_BRANCH * D_MODEL
IN_COLS = QKV_COLS + GATE_COLS
NEG_INF = -1e30
EPS = 1e-6
LOG2E = math.log2(math.e)

LANES = 128
GATE_BLKS = GATE_COLS // LANES
AQ0, AK0, AV0 = GATE_BLKS, GATE_BLKS + 4, GATE_BLKS + 8
BQ0, BK0, BV0 = GATE_BLKS + 12, GATE_BLKS + 16, GATE_BLKS + 20
CQ0, CK0, CV0 = GATE_BLKS + 24, GATE_BLKS + 28, GATE_BLKS + 32

TM_PROJ = 512
TN_PROJ = 1536
TM_MERGE = 512
TM_FFN = 512
FFN_CHUNKS = ((0, 1536), (1536, 2816))
TQ_A = 1024
TK_A = 512
TP_A = 512
UNROLL_A = 2
NORM_SLACK_A = 1.02
L_MIN_A = 2.0 ** -100
TQ_B = 512
UNROLL_B = 3
TQ_C = 256
GROUP_C = 16
C_SKIP_LOG2 = -150.0
VMEM_LIMIT = 56 * 1024 * 1024


def _cparams(*sem):
    return pltpu.CompilerParams(dimension_semantics=sem, vmem_limit_bytes=VMEM_LIMIT)


def _ada_kernel(c_ref, w_ref, b_ref, o_ref):
    c = c_ref[...]
    cs = c * jax.nn.sigmoid(c)
    o_ref[...] = jnp.dot(cs, w_ref[...], preferred_element_type=F32,
                         precision=lax.Precision.HIGHEST) + b_ref[...]


def _ada_mod(c, w_ada, b_ada):
    depth, d, e = w_ada.shape
    bsz = c.shape[0]
    nblk = e // d
    return pl.pallas_call(
        _ada_kernel,
        grid=(depth, nblk),
        in_specs=[
            pl.BlockSpec((bsz, d), lambda l, j: (0, 0)),
            pl.BlockSpec((None, d, d), lambda l, j: (l, 0, j)),
            pl.BlockSpec((None, 1, d), lambda l, j: (l, 0, j)),
        ],
        out_specs=pl.BlockSpec((None, bsz, d), lambda l, j: (l, 0, j)),
        out_shape=jax.ShapeDtypeStruct((depth, bsz, e), F32),
        compiler_params=_cparams("arbitrary", "arbitrary"),
        name="ada_mod",
    )(c, w_ada, b_ada.reshape(depth, 1, e))


def _norm_mod(x, g, shift, scale):
    ms = jnp.mean(x * x, axis=-1, keepdims=True)
    y = x * lax.rsqrt(ms + EPS) * g
    return y * (1.0 + scale) + shift


def _inproj_kernel(x_ref, mod_ref, g_ref, w_ref, o_ref):
    h = _norm_mod(x_ref[...], g_ref[...], mod_ref[0:1, :], mod_ref[1:2, :]).astype(BF16)
    n = w_ref.shape[1]
    for c0 in range(0, n, TN_PROJ):
        res = jnp.dot(h, w_ref[:, c0:c0 + TN_PROJ], preferred_element_type=F32)
        for cb in range(TN_PROJ // LANES):
            o_ref[c0 // LANES + cb] = res[:, cb * LANES:(cb + 1) * LANES].astype(o_ref.dtype)


def _inproj(x2d, mod, g, w, layer, seq):
    t, d = x2d.shape
    n = w.shape[2]
    assert n % TN_PROJ == 0
    tm = min(TM_PROJ, seq)
    return pl.pallas_call(
        _inproj_kernel,
        grid=(t // tm,),
        in_specs=[
            pl.BlockSpec((tm, d), lambda i: (i, 0)),
            pl.BlockSpec((None, None, 6, d), lambda i: (layer, (i * tm) // seq, 0, 0)),
            pl.BlockSpec((None, 1, d), lambda i: (layer, 0, 0)),
            pl.BlockSpec((None, d, n), lambda i: (layer, 0, 0), pipeline_mode=pl.Buffered(1)),
        ],
        out_specs=pl.BlockSpec((n // LANES, tm, LANES), lambda i: (0, i, 0)),
        out_shape=jax.ShapeDtypeStruct((n // LANES, t, LANES), BF16),
        compiler_params=_cparams("parallel"),
        name="in_proj",
    )(x2d, mod, g, w)


def _t5_bucket_np(rel):
    nb = T5_BUCKETS // 2
    max_exact = nb // 2
    ret = np.where(rel > 0, nb, 0)
    n = np.abs(rel)
    nf = np.maximum(n, 1).astype(np.float32)
    scaled = (np.log(nf / np.float32(max_exact)) / np.float32(math.log(T5_MAX_DIST / max_exact))
              * np.float32(nb - max_exact))
    large = max_exact + scaled.astype(np.int32)
    large = np.minimum(large, nb - 1)
    return (ret + np.where(n < max_exact, n, large)).astype(np.int32)


def _toeplitz(vec, rows, cols):
    length = rows + cols - 1
    assert vec.shape[-1] == length
    lead = vec.shape[:-1]
    ext = jnp.concatenate([vec, jnp.zeros(lead + (1,), vec.dtype)], axis=-1)
    flat = jnp.broadcast_to(ext[..., None, :], lead + (rows, length + 1)).reshape(lead + (rows * (length + 1),))
    return flat[..., :rows * length].reshape(lead + (rows, length))[..., rows - 1:rows - 1 + cols]


def _a_bias_tiles(t5_table, tq, tk):
    r = np.arange(tq)[None, :]
    c = np.arange(tk)[:, None]
    far_bucket = _t5_bucket_np((c - 2 * tk) - r)
    assert (far_bucket == far_bucket[0, 0]).all() and far_bucket[0, 0] == _t5_bucket_np(np.array(-10 * tq))
    far = t5_table[int(far_bucket[0, 0])].astype(F32)
    tiles = [jnp.zeros((A_HEADS, tk, tq), F32)]
    bmax = jnp.zeros((A_HEADS,), F32)
    u = np.arange(tk + tq - 1)
    for koff in range(-tk, tq, tk):
        vec = t5_table[_t5_bucket_np(koff + tk - 1 - u)].astype(F32).T
        vec = (vec - far[:, None]) * LOG2E
        bmax = jnp.maximum(bmax, jnp.max(vec, axis=1))
        bias = _toeplitz(vec, tk, tq)
        allowed = ((c + koff) // CHUNK) <= (r // CHUNK)
        tiles.append(jnp.where(jnp.asarray(allowed)[None], bias, NEG_INF))
    tiles.append(jnp.full((A_HEADS, tk, tq), NEG_INF, F32))
    return jnp.stack(tiles, axis=1), jnp.broadcast_to(bmax[:, None, None], (A_HEADS, 1, LANES))


def _attn_a_kernel(lq1_ref, lk1_ref, lq2_ref, lk2_ref, subg_ref, bmax_ref, bias_ref, q_ref, k_ref, v_ref,
                   o_ref, vt_ref, acc_ref, *s_refs, tq, tk, lam_init):
    seq = q_ref.shape[0]
    nq = seq // tq
    nkb = seq // tk
    per_q = tq // tk
    unroll = len(s_refs)
    tp = TP_A
    lam = (jnp.exp(jnp.sum(lq1_ref[...] * lk1_ref[...], axis=-1, keepdims=True))
           - jnp.exp(jnp.sum(lq2_ref[...] * lk2_ref[...], axis=-1, keepdims=True)) + lam_init)
    lane = lax.broadcasted_iota(jnp.int32, (1, LANES), 1)
    mask1 = (lane < HEAD_DIM).astype(BF16)
    mask2 = (lane >= HEAD_DIM).astype(BF16)
    subg = subg_ref[...] * (1.0 - lam_init)
    sel = (lax.broadcasted_iota(jnp.int32, (8, LANES), 0)
           == lax.broadcasted_iota(jnp.int32, (8, LANES), 1) // HEAD_DIM).astype(BF16)

    def prepare(b, kn2):
        c0 = pl.multiple_of(b * tk, tk)
        vt_ref[:, pl.ds(c0, tk)] = jnp.transpose(v_ref[pl.ds(c0, tk), :])
        kb = k_ref[pl.ds(c0, tk), :]
        blk = lax.dot_general(sel, kb * kb, (((1,), (1,)), ((), ())), preferred_element_type=F32)
        return jnp.maximum(kn2, blk)

    kn2 = lax.fori_loop(0, nkb, prepare, jnp.zeros((8, tk), F32))
    kmax2 = jnp.max(kn2, axis=1, keepdims=True) * NORM_SLACK_A
    kmax = jnp.sqrt(jnp.concatenate([jnp.broadcast_to(kmax2[0:1], (1, tq)),
                                     jnp.broadcast_to(kmax2[1:2], (1, tq))], axis=1))

    def finish(i, acc, l):
        o = acc * (1.0 / l)
        d = o[:, :tq] - lam * o[:, tq:]
        y = d * lax.rsqrt(jnp.mean(d * d, axis=0, keepdims=True) + EPS) * subg
        o_ref[pl.ds(pl.multiple_of(i * tq, tq), tq), :] = jnp.transpose(y).astype(o_ref.dtype)

    def load_q(i):
        qb = q_ref[pl.ds(pl.multiple_of(i * tq, tq), tq), :]
        return jnp.concatenate([qb * mask1, qb * mask2], axis=0)

    def steps(i, step, st):
        n_plain = jnp.maximum(per_q * i - 1, 0) // unroll
        n_steps = (per_q * (i + 1) + unroll - 1) // unroll
        st = lax.fori_loop(0, n_plain, lambda it, s_: step(it, s_, False), st)
        return lax.fori_loop(n_plain, n_steps, lambda it, s_: step(it, s_, True), st)

    def bounded_pass(i):
        qq = load_q(i)
        qn2 = lax.dot_general(jnp.ones((8, LANES), BF16), qq * qq, (((1,), (1,)), ((), ())),
                              preferred_element_type=F32)[0:1]
        shift = jnp.sqrt(qn2 * NORM_SLACK_A) * kmax + (bmax_ref[:, 0:1] + 1.0)
        acc_ref[...] = jnp.zeros(acc_ref.shape, F32)

        def block(js, l, bias, late_queries_only=False):
            c0 = pl.multiple_of(js * tk, tk)
            kb = k_ref[pl.ds(c0, tk), :]
            vt = vt_ref[:, pl.ds(c0, tk)]
            sums = []
            for m0 in (0, tq):
                cols = slice(m0 + tq // 2, m0 + tq) if late_queries_only else slice(m0, m0 + tq)
                s = lax.dot_general(kb, qq[cols], (((1,), (1,)), ((), ())), preferred_element_type=F32)
                if bias is not None:
                    s = s + jnp.concatenate([bias, bias], axis=1)[:, cols]
                p = jnp.exp2(s - shift[:, cols])
                ps = jnp.sum(p, axis=0, keepdims=True)
                if late_queries_only:
                    ps = jnp.concatenate([jnp.zeros((1, tq // 2), F32), ps], axis=1)
                sums.append(ps)
                acc_ref[:, cols] += jnp.dot(vt, p.astype(BF16), preferred_element_type=F32)
            return l + jnp.concatenate(sums, axis=1)

        def plain_blocks(first, count, l):
            for u in range(count):
                l = block(first + u, l, None)
            return l

        n_pairs = jnp.maximum(i - 1, 0)
        l = lax.fori_loop(0, n_pairs // 2, lambda it, l: plain_blocks(4 * it, 4, l),
                          jnp.zeros((1, 2 * tq), F32))
        l = lax.cond(n_pairs % 2 == 1, lambda l: plain_blocks(2 * n_pairs - 2, 2, l), lambda l: l, l)

        def own_blocks(l):
            return block(2 * i + 1, block(2 * i, l, bias_ref[2]), bias_ref[3], late_queries_only=True)

        def previous_and_own_blocks(l):
            return own_blocks(block(2 * i - 1, block(2 * i - 2, l, None), bias_ref[1]))

        l = lax.cond(i > 0, previous_and_own_blocks, own_blocks, l)
        finish(i, acc_ref[...], l)
        return jnp.min(l)

    def running_max_pass(i):
        qq = load_q(i)
        acc_ref[...] = jnp.zeros(acc_ref.shape, F32)

        def block(js, st, with_bias, s_ref):
            m, l = st
            c0 = pl.multiple_of(jnp.minimum(js, nkb - 1) * tk, tk)
            kb = k_ref[pl.ds(c0, tk), :]
            s = lax.dot_general(kb, qq, (((1,), (1,)), ((), ())), preferred_element_type=F32)
            if with_bias:
                bias = bias_ref[jnp.clip(js - per_q * i + 2, 0, per_q + 2)]
                s = s + jnp.concatenate([bias, bias], axis=1)
            s_ref[...] = s
            m_new = jnp.maximum(m, jnp.max(s, axis=0, keepdims=True))
            alpha = jnp.exp2(m - m_new)
            sums = []
            for ct in range(2 * tq // tp):
                cols = slice(ct * tp, (ct + 1) * tp)
                p = jnp.exp2(s_ref[:, cols] - m_new[:, cols])
                sums.append(jnp.sum(p, axis=0, keepdims=True))
                vt = vt_ref[:, pl.ds(c0, tk)]
                acc_ref[:, cols] = (alpha[:, cols] * acc_ref[:, cols]
                                    + jnp.dot(vt, p.astype(BF16), preferred_element_type=F32))
            return m_new, alpha * l + jnp.concatenate(sums, axis=1)

        def step(it, st, with_bias):
            for u in range(unroll):
                st = block(it * unroll + u, st, with_bias, s_refs[u])
            return st

        _, l = steps(i, step, (jnp.full((1, 2 * tq), NEG_INF, F32), jnp.zeros((1, 2 * tq), F32)))
        finish(i, acc_ref[...], l)

    def qtile(i, carry):
        l_min = bounded_pass(i)

        @pl.when(jnp.logical_not(l_min > L_MIN_A))
        def _():
            running_max_pass(i)

        return carry

    lax.fori_loop(0, nq, qtile, 0)


def _attn_a(proj, bias, bmax, lq1, lk1, lq2, lk2, subg, layer, bsz, seq, lam_init):
    tq = min(TQ_A, seq // 2)
    tk = min(TK_A, tq)
    assert tk == TP_A and tq == 2 * tk and UNROLL_A == 2
    vec = lambda n: pl.BlockSpec((None, 1, n), lambda b, h: (layer, 0, 0))
    col = lambda base: pl.BlockSpec((None, seq, LANES), lambda b, h: (base + h, b, 0))
    return pl.pallas_call(
        functools.partial(_attn_a_kernel, tq=tq, tk=tk, lam_init=lam_init),
        grid=(bsz, A_HEADS),
        in_specs=[vec(HEAD_DIM), vec(HEAD_DIM), vec(HEAD_DIM), vec(HEAD_DIM),
                  pl.BlockSpec((None, 2 * HEAD_DIM, 1), lambda b, h: (layer, 0, 0)),
                  pl.BlockSpec((None, 1, LANES), lambda b, h: (h, 0, 0)),
                  pl.BlockSpec((None, tq // tk + 3, tk, tq), lambda b, h: (h, 0, 0, 0)),
                  col(AQ0), col(AK0), col(AV0)],
        out_specs=pl.BlockSpec((None, seq, LANES), lambda b, h: (h, b, 0)),
        out_shape=jax.ShapeDtypeStruct((A_HEADS, bsz * seq, LANES), BF16),
        scratch_shapes=[pltpu.VMEM((LANES, seq), BF16), pltpu.VMEM((LANES, 2 * tq), F32)]
        + [pltpu.VMEM((tk, 2 * tq), F32) for _ in range(UNROLL_A)],
        compiler_params=_cparams("parallel", "parallel"),
        name="attn_a",
    )(lq1, lk1, lq2, lk2, subg, bmax, bias, proj, proj, proj)


def _b_bias_tiles(rel_bias, tq):
    depth = rel_bias.shape[0]
    assert B_PAD % tq == 0
    r = np.arange(tq)[None, :]
    c = np.arange(tq)[:, None]
    lo = CHUNK * (r // CHUNK)
    u = np.arange(2 * tq - 1)
    tiles = []
    for koff in range(-B_PAD, tq, tq):
        in_band = (c + koff >= lo - B_PAD) & (c + koff < lo + CHUNK)
        idx = np.clip(u - (tq - 1) - koff, -B_MAX_REL, B_MAX_REL) + B_MAX_REL
        bias = _toeplitz(rel_bias[..., idx].astype(F32) * LOG2E, tq, tq)
        bias = jnp.where(jnp.asarray(in_band), bias, NEG_INF)
        tiles.append(bias.reshape(depth, B_HEADS // 2, 2, tq, tq).transpose(0, 1, 3, 2, 4)
                     .reshape(depth, B_HEADS // 2, tq, 2 * tq))
    bmax = jnp.max((rel_bias.astype(F32) * LOG2E).reshape(depth, B_HEADS // 2, -1), axis=2)
    return (jnp.stack(tiles, axis=2),
            jnp.broadcast_to(bmax[:, :, None, None], (depth, B_HEADS // 2, 1, LANES)))


def _attn_b_kernel(bmax_ref, bias_ref, q_ref, k_ref, v_ref, o_ref, vt_ref, *s_refs, tq):
    seq = q_ref.shape[0]
    nq = seq // tq
    max_prev = B_PAD // tq
    lane = lax.broadcasted_iota(jnp.int32, (1, LANES), 1)
    mask1 = (lane < HEAD_DIM).astype(BF16)
    mask2 = (lane >= HEAD_DIM).astype(BF16)
    feat = lax.broadcasted_iota(jnp.int32, (LANES, tq), 0)
    sel = (lax.broadcasted_iota(jnp.int32, (8, LANES), 0)
           == lax.broadcasted_iota(jnp.int32, (8, LANES), 1) // HEAD_DIM).astype(BF16)

    def prepare(b, kn2):
        c0 = pl.multiple_of(b * tq, tq)
        vt_ref[:, pl.ds(c0, tq)] = jnp.transpose(v_ref[pl.ds(c0, tq), :])
        kb = k_ref[pl.ds(c0, tq), :]
        blk = lax.dot_general(sel, kb * kb, (((1,), (1,)), ((), ())), preferred_element_type=F32)
        return jnp.maximum(kn2, blk)

    kn2 = lax.fori_loop(0, nq, prepare, jnp.zeros((8, tq), F32))
    kmax2 = jnp.max(kn2, axis=1, keepdims=True) * NORM_SLACK_A
    kmax = jnp.sqrt(jnp.concatenate([jnp.broadcast_to(kmax2[0:1], (1, tq)),
                                     jnp.broadcast_to(kmax2[1:2], (1, tq))], axis=1))

    def tile_blocks(i, n_prev):
        static = isinstance(i, int)
        r0 = i * tq if static else pl.multiple_of(i * tq, tq)
        qb = q_ref[pl.ds(r0, tq), :]
        qq = jnp.concatenate([qb * mask1, qb * mask2], axis=0)
        starts = [r0 - d * tq for d in range(n_prev, 0, -1)] + [r0]
        blocks = tuple((c0 if static else pl.multiple_of(c0, tq), max_prev - n_prev + j)
                       for j, c0 in enumerate(starts))
        return r0, qq, blocks

    def finish(r0, acc, den):
        o = acc * (1.0 / den)
        o = jnp.where(feat < HEAD_DIM, o[:, :tq], o[:, tq:])
        o_ref[pl.ds(r0, tq), :] = jnp.transpose(o).astype(o_ref.dtype)

    def bounded_tile(i, n_prev):
        r0, qq, blocks = tile_blocks(i, n_prev)
        qn2 = lax.dot_general(jnp.ones((8, LANES), BF16), qq * qq, (((1,), (1,)), ((), ())),
                              preferred_element_type=F32)[0:1]
        shift = jnp.sqrt(qn2 * NORM_SLACK_A) * kmax + (bmax_ref[:, 0:1] + 1.0)
        acc = None
        den = None
        for c0, t in blocks:
            s = lax.dot_general(k_ref[pl.ds(c0, tq), :], qq, (((1,), (1,)), ((), ())),
                                preferred_element_type=F32) + bias_ref[t]
            p = jnp.exp2(s - shift)
            ps = jnp.sum(p, axis=0, keepdims=True)
            pv = jnp.dot(vt_ref[:, pl.ds(c0, tq)], p.astype(BF16), preferred_element_type=F32)
            den = ps if den is None else den + ps
            acc = pv if acc is None else acc + pv
        finish(r0, acc, den)
        return jnp.min(den)

    def exact_tile(i, refs, n_prev):
        r0, qq, blocks = tile_blocks(i, n_prev)
        m = None
        for (c0, t), s_ref in zip(blocks, refs):
            s = lax.dot_general(k_ref[pl.ds(c0, tq), :], qq, (((1,), (1,)), ((), ())),
                                preferred_element_type=F32) + bias_ref[t]
            s_ref[...] = s
            bm = jnp.max(s, axis=0, keepdims=True)
            m = bm if m is None else jnp.maximum(m, bm)
        acc = None
        den = None
        for (c0, t), s_ref in zip(blocks, refs):
            p = jnp.exp2(s_ref[...] - m)
            ps = jnp.sum(p, axis=0, keepdims=True)
            pv = jnp.dot(vt_ref[:, pl.ds(c0, tq)], p.astype(BF16), preferred_element_type=F32)
            den = ps if den is None else den + ps
            acc = pv if acc is None else acc + pv
        finish(r0, acc, den)

    def tiles(idx, n_prev):
        mins = [bounded_tile(i, n_prev) for i in idx]
        for i, l_min in zip(idx, mins):
            @pl.when(jnp.logical_not(l_min > L_MIN_A))
            def _():
                exact_tile(i, s_refs, n_prev)

    for i in range(min(max_prev, nq)):
        tiles([i], i)

    def step(it, carry):
        tiles([max_prev + UNROLL_B * it + u for u in range(UNROLL_B)], max_prev)
        return carry

    n_rest = max(nq - max_prev, 0)
    lax.fori_loop(0, n_rest // UNROLL_B, step, 0)
    for i in range(nq - n_rest % UNROLL_B, nq):
        tiles([i], max_prev)


def _attn_b(proj, bias, bmax, layer, bsz, seq):
    tq = TQ_B
    col = lambda base: pl.BlockSpec((None, seq, LANES), lambda b, h: (base + h, b, 0))
    return pl.pallas_call(
        functools.partial(_attn_b_kernel, tq=tq),
        grid=(bsz, B_HEADS // 2),
        in_specs=[pl.BlockSpec((None, None, 1, LANES), lambda b, h: (layer, h, 0, 0)),
                  pl.BlockSpec((None, None, B_PAD // tq + 1, tq, 2 * tq), lambda b, h: (layer, h, 0, 0, 0)),
                  col(BQ0), col(BK0), col(BV0)],
        out_specs=pl.BlockSpec((None, seq, LANES), lambda b, h: (h, b, 0)),
        out_shape=jax.ShapeDtypeStruct((B_HEADS // 2, bsz * seq, LANES), BF16),
        scratch_shapes=[pltpu.VMEM((LANES, seq), BF16)]
        + [pltpu.VMEM((tq, 2 * tq), F32) for _ in range(B_PAD // tq + 1)],
        compiler_params=_cparams("parallel", "parallel"),
        name="attn_b",
    )(bmax, bias, proj, proj, proj)


def _attn_c_kernel(q_ref, k_ref, v_ref, o_ref, *, tq, group):
    seq = q_ref.shape[0]
    nq = seq // tq
    row = lax.broadcasted_iota(jnp.int32, (tq, tq), 0)
    colm = lax.broadcasted_iota(jnp.int32, (tq, tq), 1)
    tri_strict = (row > colm).astype(BF16)
    before = colm < row

    def block(i, j, carry, acc, diag):
        qb = q_ref[pl.ds(pl.multiple_of(i * tq, tq), tq), :]
        c0 = pl.multiple_of(j * tq, tq)
        kb = k_ref[pl.ds(c0, tq), :]
        vb = v_ref[pl.ds(c0, tq), :]
        z = lax.dot_general(qb, kb, (((1,), (1,)), ((), ())), preferred_element_type=F32)
        lm = -(jnp.maximum(z, 0.0) + jnp.log2(1.0 + jnp.exp2(-jnp.abs(z))))
        if diag:
            lm = jnp.where(before, lm, 0.0)
        lm_b = lm.astype(BF16)
        excl = jnp.dot(lm_b, tri_strict, preferred_element_type=F32)
        logw = (z + lm) + excl + carry
        wgt = jnp.exp2(logw)
        if diag:
            wgt = jnp.where(before, wgt, 0.0)
        acc = acc + jnp.dot(wgt.astype(BF16), vb, preferred_element_type=F32)
        carry = carry + excl[:, 0:1] + lm_b[:, 0:1].astype(F32)
        return carry, acc

    def qgroup(gi, c):
        i0 = gi * group
        sts = [block(i0 + g, i0 + g, jnp.zeros((tq, 1), F32), jnp.zeros((tq, LANES), F32), True)
               for g in range(group)]
        carries = tuple(st[0] for st in sts)
        accs = tuple(st[1] for st in sts)

        def live(carries):
            return functools.reduce(jnp.maximum, [jnp.max(cr) for cr in carries])

        def cond(st):
            t, cmax, _, _ = st
            return jnp.logical_and(t <= i0 + group - 1, cmax > C_SKIP_LOG2)

        def body(st):
            t, _, carries, accs = st
            new = []
            for g in range(group):
                j = i0 + g - t
                cin = jnp.where(j >= 0, carries[g], NEG_INF)
                new.append(block(i0 + g, jnp.maximum(j, 0), cin, accs[g], False))
            carries = tuple(st_[0] for st_ in new)
            return t + 1, live(carries), carries, tuple(st_[1] for st_ in new)

        _, _, _, accs = lax.while_loop(cond, body, (jnp.int32(1), live(carries), carries, accs))
        for g in range(group):
            o_ref[pl.ds(pl.multiple_of((i0 + g) * tq, tq), tq), :] = accs[g].astype(o_ref.dtype)
        return c

    lax.fori_loop(0, nq // group, qgroup, 0)


def _attn_c(proj, bsz, seq):
    tq = min(TQ_C, seq)
    group = min(GROUP_C, seq // tq)
    assert (seq // tq) % group == 0
    col = lambda base: pl.BlockSpec((None, seq, LANES), lambda b, h: (base + h, b, 0))
    return pl.pallas_call(
        functools.partial(_attn_c_kernel, tq=tq, group=group),
        grid=(bsz, C_HEADS),
        in_specs=[col(CQ0), col(CK0), col(CV0)],
        out_specs=pl.BlockSpec((None, seq, LANES), lambda b, h: (h, b, 0)),
        out_shape=jax.ShapeDtypeStruct((C_HEADS, bsz * seq, LANES), BF16),
        compiler_params=_cparams("parallel", "parallel"),
        name="attn_c",
    )(proj, proj, proj)


def _cat_lanes(ref, start, count):
    return jnp.concatenate([ref[start + c] for c in range(count)], axis=1)


def _merge_kernel(x_ref, mod_ref, gate_ref, bg_ref, ya_ref, yb_ref, yc_ref, wb_ref, wo_ref, o_ref):
    blks = D_MODEL // LANES
    merged = None
    for r, y_ref in enumerate((ya_ref, yb_ref, yc_ref)):
        y = _cat_lanes(y_ref, 0, BRANCH_WIDTH // LANES)
        br = jnp.dot(y, wb_ref[r], preferred_element_type=F32)
        pre = _cat_lanes(gate_ref, r * blks, blks).astype(F32) + bg_ref[:, r * D_MODEL:(r + 1) * D_MODEL]
        term = jax.nn.sigmoid(pre) * br
        merged = term if merged is None else merged + term
    out = jnp.dot(merged.astype(BF16), wo_ref[...], preferred_element_type=F32)
    o_ref[...] = x_ref[...] + mod_ref[2:3, :] * out


def _merge(x2d, mod, proj, b_gate, ya, yb, yc, w_branch, w_out, layer, seq):
    t, d = x2d.shape
    tm = min(TM_MERGE, seq)
    ybs = lambda: pl.BlockSpec((BRANCH_WIDTH // LANES, tm, LANES), lambda i: (0, i, 0))
    return pl.pallas_call(
        _merge_kernel,
        grid=(t // tm,),
        in_specs=[
            pl.BlockSpec((tm, d), lambda i: (i, 0)),
            pl.BlockSpec((None, None, 6, d), lambda i: (layer, (i * tm) // seq, 0, 0)),
            pl.BlockSpec((GATE_BLKS, tm, LANES), lambda i: (0, i, 0)),
            pl.BlockSpec((None, 1, GATE_COLS), lambda i: (layer, 0, 0)),
            ybs(), ybs(), ybs(),
            pl.BlockSpec((None, N_BRANCH, BRANCH_WIDTH, d), lambda i: (layer, 0, 0, 0)),
            pl.BlockSpec((None, d, d), lambda i: (layer, 0, 0)),
        ],
        out_specs=pl.BlockSpec((tm, d), lambda i: (i, 0)),
        out_shape=jax.ShapeDtypeStruct((t, d), F32),
        compiler_params=_cparams("parallel"),
        name="merge",
    )(x2d, mod, proj, b_gate, ya, yb, yc, w_branch, w_out)


def _ffn_kernel(x_ref, mod_ref, g_ref, fg_ref, w13_ref, w2_ref, o_ref, *, final):
    dff = w2_ref.shape[0]
    x = x_ref[...]
    h = _norm_mod(x, g_ref[...], mod_ref[3:4, :], mod_ref[4:5, :]).astype(BF16)
    acc = None
    for f0, f1 in FFN_CHUNKS:
        u_gate = jnp.dot(h, w13_ref[:, f0:f1], preferred_element_type=F32)
        u_up = jnp.dot(h, w13_ref[:, dff + f0:dff + f1], preferred_element_type=F32)
        act = (u_gate * jax.nn.sigmoid(u_gate) * u_up).astype(BF16)
        part = jnp.dot(act, w2_ref[f0:f1, :], preferred_element_type=F32)
        acc = part if acc is None else acc + part
    y = x + mod_ref[5:6, :] * acc
    if final:
        y = y * lax.rsqrt(jnp.mean(y * y, axis=-1, keepdims=True) + EPS) * fg_ref[...]
    o_ref[...] = y


def _ffn(x2d, mod, g, final_g, w13, w2, layer, seq, final):
    t, d = x2d.shape
    dff = w2.shape[1]
    assert FFN_CHUNKS[0][0] == 0 and FFN_CHUNKS[-1][1] == dff
    tm = min(TM_FFN, seq)
    resident = lambda shape: pl.BlockSpec((None,) + shape, lambda i: (layer, 0, 0), pipeline_mode=pl.Buffered(1))
    return pl.pallas_call(
        functools.partial(_ffn_kernel, final=final),
        grid=(t // tm,),
        in_specs=[
            pl.BlockSpec((tm, d), lambda i: (i, 0)),
            pl.BlockSpec((None, None, 6, d), lambda i: (layer, (i * tm) // seq, 0, 0)),
            pl.BlockSpec((None, 1, d), lambda i: (layer, 0, 0)),
            pl.BlockSpec((1, d), lambda i: (0, 0)),
            resident((d, 2 * dff)),
            resident((dff, d)),
        ],
        out_specs=pl.BlockSpec((tm, d), lambda i: (i, 0)),
        out_shape=jax.ShapeDtypeStruct((t, d), F32),
        compiler_params=_cparams("parallel"),
        name="ffn",
    )(x2d, mod, g, final_g, w13, w2)


def _prep_w_in(w_in):
    scale = np.ones((IN_COLS,), np.float32)
    scale[0 * BRANCH_WIDTH:1 * BRANCH_WIDTH] = HEAD_DIM ** -0.5 * LOG2E
    scale[3 * BRANCH_WIDTH:4 * BRANCH_WIDTH] = HEAD_DIM ** -0.5 * LOG2E
    scale[6 * BRANCH_WIDTH:7 * BRANCH_WIDTH] = C_HEAD_DIM ** -0.5 * LOG2E
    w = w_in * jnp.asarray(scale)
    return jnp.concatenate([w[..., QKV_COLS:], w[..., :QKV_COLS]], axis=-1).astype(BF16)


def kernel(x, c, w_ada, b_ada, norm1_g, w_in, b_gate, lam_q1, lam_k1, lam_q2, lam_k2, subln_g,
           t5_table, rel_bias_b, w_branch, w_out, norm2_g, w13, w2, final_g):
    bsz, seq, d = x.shape
    depth = w_in.shape[0]
    t = bsz * seq
    assert d == D_MODEL and w_in.shape[2] == IN_COLS and w2.shape[1] == D_FF
    assert seq % (2 * B_PAD) == 0, "sequence tiles assume a multiple of 1024 frames"

    mod = _ada_mod(c, w_ada, b_ada).reshape(depth, bsz, 6, d)
    w_in_b = _prep_w_in(w_in)
    w_branch_b = w_branch.astype(BF16)
    w_out_b = w_out.astype(BF16)
    w13_b = w13.astype(BF16)
    w2_b = w2.astype(BF16)
    tq_a = min(TQ_A, seq // 2)
    a_bias, a_bmax = _a_bias_tiles(t5_table, tq_a, min(TK_A, tq_a))
    b_bias, b_bmax = _b_bias_tiles(rel_bias_b, TQ_B)
    fg = final_g.reshape(1, d)
    row = lambda p: p.reshape(depth, 1, -1)
    g1, g2, bg = row(norm1_g), row(norm2_g), row(b_gate)
    lams = [row(p) for p in (lam_q1, lam_k1, lam_q2, lam_k2)]
    subg = subln_g.reshape(depth, -1, 1)

    x2d = x.reshape(t, d)
    for l in range(depth):
        lam_init = 0.8 - 0.6 * math.exp(-0.3 * l)
        proj = _inproj(x2d, mod, g1, w_in_b, l, seq)
        ya = _attn_a(proj, a_bias, a_bmax, *lams, subg, l, bsz, seq, lam_init)
        yb = _attn_b(proj, b_bias, b_bmax, l, bsz, seq)
        yc = _attn_c(proj, bsz, seq)
        x2d = _merge(x2d, mod, proj, bg, ya, yb, yc, w_branch_b, w_out_b, l, seq)
        x2d = _ffn(x2d, mod, g2, fg, w13_b, w2_b, l, seq, final=(l == depth - 1))
    return x2d.reshape(bsz, seq, d)
```

```python
import functools
import math

import numpy as np
import jax
import jax.numpy as jnp
from jax import lax
from jax.experimental import pallas as pl
from jax.experimental.pallas import tpu as pltpu

F32 = jnp.float32
BF16 = jnp.bfloat16

D_MODEL = 1024
CHUNK = 64
HEAD_DIM = 64
BRANCH_WIDTH = 512
N_BRANCH = 3
A_HEADS = 4
B_HEADS = 8
C_HEADS = 4
C_HEAD_DIM = 128
B_LEFT_CHUNKS = 8
B_PAD = B_LEFT_CHUNKS * CHUNK
B_MAX_REL = 128
T5_BUCKETS = 32
T5_MAX_DIST = 128
D_FF = 2816
QKV_COLS = 3 * N_BRANCH * BRANCH_WIDTH
GATE_COLS = N_BRANCH * D_MODEL
IN_COLS = QKV_COLS + GATE_COLS
NEG_INF = -1e30
EPS = 1e-6
LOG2E = math.log2(math.e)

LANES = 128
GATE_BLKS = GATE_COLS // LANES
AQ0, AK0, AV0 = GATE_BLKS, GATE_BLKS + 4, GATE_BLKS + 8
BQ0, BK0, BV0 = GATE_BLKS + 12, GATE_BLKS + 16, GATE_BLKS + 20
CQ0, CK0, CV0 = GATE_BLKS + 24, GATE_BLKS + 28, GATE_BLKS + 32

TM_PROJ = 512
TN_PROJ = 1536
TM_MERGE = 512
TM_FFN = 512
FFN_CHUNKS = ((0, 1536), (1536, 2816))
TQ_A = 1024
TK_A = 512
TP_A = 512
UNROLL_A = 2
NORM_SLACK_A = 1.02
L_MIN_A = 2.0 ** -100
TQ_B = 512
UNROLL_B = 3
TQ_C = 256
GROUP_C = 16
C_SKIP_LOG2 = -150.0
VMEM_LIMIT = 56 * 1024 * 1024


def _cparams(*sem):
    return pltpu.CompilerParams(dimension_semantics=sem, vmem_limit_bytes=VMEM_LIMIT)


def _ada_kernel(c_ref, w_ref, b_ref, o_ref):
    c = c_ref[...]
    cs = c * jax.nn.sigmoid(c)
    o_ref[...] = jnp.dot(cs, w_ref[...], preferred_element_type=F32,
                         precision=lax.Precision.HIGHEST) + b_ref[...]


def _ada_mod(c, w_ada, b_ada):
    depth, d, e = w_ada.shape
    bsz = c.shape[0]
    nblk = e // d
    return pl.pallas_call(
        _ada_kernel,
        grid=(depth, nblk),
        in_specs=[
            pl.BlockSpec((bsz, d), lambda l, j: (0, 0)),
            pl.BlockSpec((None, d, d), lambda l, j: (l, 0, j)),
            pl.BlockSpec((None, 1, d), lambda l, j: (l, 0, j)),
        ],
        out_specs=pl.BlockSpec((None, bsz, d), lambda l, j: (l, 0, j)),
        out_shape=jax.ShapeDtypeStruct((depth, bsz, e), F32),
        compiler_params=_cparams("arbitrary", "arbitrary"),
        name="ada_mod",
    )(c, w_ada, b_ada.reshape(depth, 1, e))


def _norm_mod(x, g, shift, scale):
    ms = jnp.mean(x * x, axis=-1, keepdims=True)
    y = x * lax.rsqrt(ms + EPS) * g
    return y * (1.0 + scale) + shift


def _inproj_kernel(x_ref, mod_ref, g_ref, w_ref, o_ref):
    h = _norm_mod(x_ref[...], g_ref[...], mod_ref[0:1, :], mod_ref[1:2, :]).astype(BF16)
    n = w_ref.shape[1]
    for c0 in range(0, n, TN_PROJ):
        res = jnp.dot(h, w_ref[:, c0:c0 + TN_PROJ], preferred_element_type=F32)
        for cb in range(TN_PROJ // LANES):
            o_ref[c0 // LANES + cb] = res[:, cb * LANES:(cb + 1) * LANES].astype(o_ref.dtype)


def _inproj(x2d, mod, g, w, layer, seq):
    t, d = x2d.shape
    n = w.shape[2]
    assert n % TN_PROJ == 0
    tm = min(TM_PROJ, seq)
    return pl.pallas_call(
        _inproj_kernel,
        grid=(t // tm,),
        in_specs=[
            pl.BlockSpec((tm, d), lambda i: (i, 0)),
            pl.BlockSpec((None, None, 6, d), lambda i: (layer, (i * tm) // seq, 0, 0)),
            pl.BlockSpec((None, 1, d), lambda i: (layer, 0, 0)),
            pl.BlockSpec((None, d, n), lambda i: (layer, 0, 0), pipeline_mode=pl.Buffered(1)),
        ],
        out_specs=pl.BlockSpec((n // LANES, tm, LANES), lambda i: (0, i, 0)),
        out_shape=jax.ShapeDtypeStruct((n // LANES, t, LANES), BF16),
        compiler_params=_cparams("parallel"),
        name="in_proj",
    )(x2d, mod, g, w)


def _t5_bucket_np(rel):
    nb = T5_BUCKETS // 2
    max_exact = nb // 2
    ret = np.where(rel > 0, nb, 0)
    n = np.abs(rel)
    nf = np.maximum(n, 1).astype(np.float32)
    scaled = (np.log(nf / np.float32(max_exact)) / np.float32(math.log(T5_MAX_DIST / max_exact))
              * np.float32(nb - max_exact))
    large = max_exact + scaled.astype(np.int32)
    large = np.minimum(large, nb - 1)
    return (ret + np.where(n < max_exact, n, large)).astype(np.int32)


def _toeplitz(vec, rows, cols):
    length = rows + cols - 1
    assert vec.shape[-1] == length
    lead = vec.shape[:-1]
    ext = jnp.concatenate([vec, jnp.zeros(lead + (1,), vec.dtype)], axis=-1)
    flat = jnp.broadcast_to(ext[..., None, :], lead + (rows, length + 1)).reshape(lead + (rows * (length + 1),))
    return flat[..., :rows * length].reshape(lead + (rows, length))[..., rows - 1:rows - 1 + cols]


def _a_bias_tiles(t5_table, tq, tk):
    r = np.arange(tq)[None, :]
    c = np.arange(tk)[:, None]
    far_bucket = _t5_bucket_np((c - 2 * tk) - r)
    assert (far_bucket == far_bucket[0, 0]).all() and far_bucket[0, 0] == _t5_bucket_np(np.array(-10 * tq))
    far = t5_table[int(far_bucket[0, 0])].astype(F32)
    tiles = [jnp.zeros((A_HEADS, tk, tq), F32)]
    bmax = jnp.zeros((A_HEADS,), F32)
    u = np.arange(tk + tq - 1)
    for koff in range(-tk, tq, tk):
        vec = t5_table[_t5_bucket_np(koff + tk - 1 - u)].astype(F32).T
        vec = (vec - far[:, None]) * LOG2E
        bmax = jnp.maximum(bmax, jnp.max(vec, axis=1))
        bias = _toeplitz(vec, tk, tq)
        allowed = ((c + koff) // CHUNK) <= (r // CHUNK)
        tiles.append(jnp.where(jnp.asarray(allowed)[None], bias, NEG_INF))
    tiles.append(jnp.full((A_HEADS, tk, tq), NEG_INF, F32))
    return jnp.stack(tiles, axis=1), jnp.broadcast_to(bmax[:, None, None], (A_HEADS, 1, LANES))


def _attn_a_kernel(lq1_ref, lk1_ref, lq2_ref, lk2_ref, subg_ref, bmax_ref, bias_ref, q_ref, k_ref, v_ref,
                   o_ref, vt_ref, acc_ref, *s_refs, tq, tk, lam_init):
    seq = q_ref.shape[0]
    nq = seq // tq
    nkb = seq // tk
    per_q = tq // tk
    unroll = len(s_refs)
    tp = TP_A
    lam = (jnp.exp(jnp.sum(lq1_ref[...] * lk1_ref[...], axis=-1, keepdims=True))
           - jnp.exp(jnp.sum(lq2_ref[...] * lk2_ref[...], axis=-1, keepdims=True)) + lam_init)
    lane = lax.broadcasted_iota(jnp.int32, (1, LANES), 1)
    mask1 = (lane < HEAD_DIM).astype(BF16)
    mask2 = (lane >= HEAD_DIM).astype(BF16)
    subg = subg_ref[...] * (1.0 - lam_init)
    sel = (lax.broadcasted_iota(jnp.int32, (8, LANES), 0)
           == lax.broadcasted_iota(jnp.int32, (8, LANES), 1) // HEAD_DIM).astype(BF16)

    def prepare(b, kn2):
        c0 = pl.multiple_of(b * tk, tk)
        vt_ref[:, pl.ds(c0, tk)] = jnp.transpose(v_ref[pl.ds(c0, tk), :])
        kb = k_ref[pl.ds(c0, tk), :]
        blk = lax.dot_general(sel, kb * kb, (((1,), (1,)), ((), ())), preferred_element_type=F32)
        return jnp.maximum(kn2, blk)

    kn2 = lax.fori_loop(0, nkb, prepare, jnp.zeros((8, tk), F32))
    kmax2 = jnp.max(kn2, axis=1, keepdims=True) * NORM_SLACK_A
    kmax = jnp.sqrt(jnp.concatenate([jnp.broadcast_to(kmax2[0:1], (1, tq)),
                                     jnp.broadcast_to(kmax2[1:2], (1, tq))], axis=1))

    def finish(i, acc, l):
        o = acc * (1.0 / l)
        d = o[:, :tq] - lam * o[:, tq:]
        y = d * lax.rsqrt(jnp.mean(d * d, axis=0, keepdims=True) + EPS) * subg
        o_ref[pl.ds(pl.multiple_of(i * tq, tq), tq), :] = jnp.transpose(y).astype(o_ref.dtype)

    def load_q(i):
        qb = q_ref[pl.ds(pl.multiple_of(i * tq, tq), tq), :]
        return jnp.concatenate([qb * mask1, qb * mask2], axis=0)

    def steps(i, step, st):
        n_plain = jnp.maximum(per_q * i - 1, 0) // unroll
        n_steps = (per_q * (i + 1) + unroll - 1) // unroll
        st = lax.fori_loop(0, n_plain, lambda it, s_: step(it, s_, False), st)
        return lax.fori_loop(n_plain, n_steps, lambda it, s_: step(it, s_, True), st)

    def bounded_pass(i):
        qq = load_q(i)
        qn2 = lax.dot_general(jnp.ones((8, LANES), BF16), qq * qq, (((1,), (1,)), ((), ())),
                              preferred_element_type=F32)[0:1]
        shift = jnp.sqrt(qn2 * NORM_SLACK_A) * kmax + (bmax_ref[:, 0:1] + 1.0)
        acc_ref[...] = jnp.zeros(acc_ref.shape, F32)

        def block(js, l, bias, late_queries_only=False):
            c0 = pl.multiple_of(js * tk, tk)
            kb = k_ref[pl.ds(c0, tk), :]
            vt = vt_ref[:, pl.ds(c0, tk)]
            sums = []
            for m0 in (0, tq):
                cols = slice(m0 + tq // 2, m0 + tq) if late_queries_only else slice(m0, m0 + tq)
                s = lax.dot_general(kb, qq[cols], (((1,), (1,)), ((), ())), preferred_element_type=F32)
                if bias is not None:
                    s = s + jnp.concatenate([bias, bias], axis=1)[:, cols]
                p = jnp.exp2(s - shift[:, cols])
                ps = jnp.sum(p, axis=0, keepdims=True)
                if late_queries_only:
                    ps = jnp.concatenate([jnp.zeros((1, tq // 2), F32), ps], axis=1)
                sums.append(ps)
                acc_ref[:, cols] += jnp.dot(vt, p.astype(BF16), preferred_element_type=F32)
            return l + jnp.concatenate(sums, axis=1)

        def plain_blocks(first, count, l):
            for u in range(count):
                l = block(first + u, l, None)
            return l

        n_pairs = jnp.maximum(i - 1, 0)
        l = lax.fori_loop(0, n_pairs // 2, lambda it, l: plain_blocks(4 * it, 4, l),
                          jnp.zeros((1, 2 * tq), F32))
        l = lax.cond(n_pairs % 2 == 1, lambda l: plain_blocks(2 * n_pairs - 2, 2, l), lambda l: l, l)

        def own_blocks(l):
            return block(2 * i + 1, block(2 * i, l, bias_ref[2]), bias_ref[3], late_queries_only=True)

        def previous_and_own_blocks(l):
            return own_blocks(block(2 * i - 1, block(2 * i - 2, l, None), bias_ref[1]))

        l = lax.cond(i > 0, previous_and_own_blocks, own_blocks, l)
        finish(i, acc_ref[...], l)
        return jnp.min(l)

    def running_max_pass(i):
        qq = load_q(i)
        acc_ref[...] = jnp.zeros(acc_ref.shape, F32)

        def block(js, st, with_bias, s_ref):
            m, l = st
            c0 = pl.multiple_of(jnp.minimum(js, nkb - 1) * tk, tk)
            kb = k_ref[pl.ds(c0, tk), :]
            s = lax.dot_general(kb, qq, (((1,), (1,)), ((), ())), preferred_element_type=F32)
            if with_bias:
                bias = bias_ref[jnp.clip(js - per_q * i + 2, 0, per_q + 2)]
                s = s + jnp.concatenate([bias, bias], axis=1)
            s_ref[...] = s
            m_new = jnp.maximum(m, jnp.max(s, axis=0, keepdims=True))
            alpha = jnp.exp2(m - m_new)
            sums = []
            for ct in range(2 * tq // tp):
                cols = slice(ct * tp, (ct + 1) * tp)
                p = jnp.exp2(s_ref[:, cols] - m_new[:, cols])
                sums.append(jnp.sum(p, axis=0, keepdims=True))
                vt = vt_ref[:, pl.ds(c0, tk)]
                acc_ref[:, cols] = (alpha[:, cols] * acc_ref[:, cols]
                                    + jnp.dot(vt, p.astype(BF16), preferred_element_type=F32))
            return m_new, alpha * l + jnp.concatenate(sums, axis=1)

        def step(it, st, with_bias):
            for u in range(unroll):
                st = block(it * unroll + u, st, with_bias, s_refs[u])
            return st

        _, l = steps(i, step, (jnp.full((1, 2 * tq), NEG_INF, F32), jnp.zeros((1, 2 * tq), F32)))
        finish(i, acc_ref[...], l)

    def qtile(i, carry):
        l_min = bounded_pass(i)

        @pl.when(jnp.logical_not(l_min > L_MIN_A))
        def _():
            running_max_pass(i)

        return carry

    lax.fori_loop(0, nq, qtile, 0)


def _attn_a(proj, bias, bmax, lq1, lk1, lq2, lk2, subg, layer, bsz, seq, lam_init):
    tq = min(TQ_A, seq // 2)
    tk = min(TK_A, tq)
    assert tk == TP_A and tq == 2 * tk and UNROLL_A == 2
    vec = lambda n: pl.BlockSpec((None, 1, n), lambda b, h: (layer, 0, 0))
    col = lambda base: pl.BlockSpec((None, seq, LANES), lambda b, h: (base + h, b, 0))
    return pl.pallas_call(
        functools.partial(_attn_a_kernel, tq=tq, tk=tk, lam_init=lam_init),
        grid=(bsz, A_HEADS),
        in_specs=[vec(HEAD_DIM), vec(HEAD_DIM), vec(HEAD_DIM), vec(HEAD_DIM),
                  pl.BlockSpec((None, 2 * HEAD_DIM, 1), lambda b, h: (layer, 0, 0)),
                  pl.BlockSpec((None, 1, LANES), lambda b, h: (h, 0, 0)),
                  pl.BlockSpec((None, tq // tk + 3, tk, tq), lambda b, h: (h, 0, 0, 0)),
                  col(AQ0), col(AK0), col(AV0)],
        out_specs=pl.BlockSpec((None, seq, LANES), lambda b, h: (h, b, 0)),
        out_shape=jax.ShapeDtypeStruct((A_HEADS, bsz * seq, LANES), BF16),
        scratch_shapes=[pltpu.VMEM((LANES, seq), BF16), pltpu.VMEM((LANES, 2 * tq), F32)]
        + [pltpu.VMEM((tk, 2 * tq), F32) for _ in range(UNROLL_A)],
        compiler_params=_cparams("parallel", "parallel"),
        name="attn_a",
    )(lq1, lk1, lq2, lk2, subg, bmax, bias, proj, proj, proj)


def _b_bias_tiles(rel_bias, tq):
    depth = rel_bias.shape[0]
    assert B_PAD % tq == 0
    r = np.arange(tq)[None, :]
    c = np.arange(tq)[:, None]
    lo = CHUNK * (r // CHUNK)
    u = np.arange(2 * tq - 1)
    tiles = []
    for koff in range(-B_PAD, tq, tq):
        in_band = (c + koff >= lo - B_PAD) & (c + koff < lo + CHUNK)
        idx = np.clip(u - (tq - 1) - koff, -B_MAX_REL, B_MAX_REL) + B_MAX_REL
        bias = _toeplitz(rel_bias[..., idx].astype(F32) * LOG2E, tq, tq)
        bias = jnp.where(jnp.asarray(in_band), bias, NEG_INF)
        tiles.append(bias.reshape(depth, B_HEADS // 2, 2, tq, 2, tq // 2).transpose(0, 1, 3, 4, 2, 5)
                     .reshape(depth, B_HEADS // 2, tq, 2 * tq))
    bmax = jnp.max((rel_bias.astype(F32) * LOG2E).reshape(depth, B_HEADS // 2, -1), axis=2)
    return (jnp.stack(tiles, axis=2),
            jnp.broadcast_to(bmax[:, :, None, None], (depth, B_HEADS // 2, 1, LANES)))


def _attn_b_kernel(bmax_ref, bias_ref, q_ref, k_ref, v_ref, o_ref, vt_ref, *s_refs, tq):
    seq = q_ref.shape[0]
    nq = seq // tq
    max_prev = B_PAD // tq
    lane = lax.broadcasted_iota(jnp.int32, (1, LANES), 1)
    mask1 = (lane < HEAD_DIM).astype(BF16)
    mask2 = (lane >= HEAD_DIM).astype(BF16)
    hq = tq // 2
    feat = lax.broadcasted_iota(jnp.int32, (LANES, hq), 0)
    sel = (lax.broadcasted_iota(jnp.int32, (8, LANES), 0)
           == lax.broadcasted_iota(jnp.int32, (8, LANES), 1) // HEAD_DIM).astype(BF16)

    def prepare(b, kn2):
        c0 = pl.multiple_of(b * tq, tq)
        vt_ref[:, pl.ds(c0, tq)] = jnp.transpose(v_ref[pl.ds(c0, tq), :])
        kb = k_ref[pl.ds(c0, tq), :]
        blk = lax.dot_general(sel, kb * kb, (((1,), (1,)), ((), ())), preferred_element_type=F32)
        return jnp.maximum(kn2, blk)

    kn2 = lax.fori_loop(0, nq, prepare, jnp.zeros((8, tq), F32))
    kmax2 = jnp.max(kn2, axis=1, keepdims=True) * NORM_SLACK_A
    kmax = jnp.sqrt(jnp.concatenate([jnp.broadcast_to(kmax2[h:h + 1], (1, hq)) for h in (0, 1, 0, 1)],
                                    axis=1))

    def tile_blocks(i, n_prev):
        static = isinstance(i, int)
        r0 = i * tq if static else pl.multiple_of(i * tq, tq)
        qb = q_ref[pl.ds(r0, tq), :]
        qq = jnp.concatenate([qb[:hq] * mask1, qb[:hq] * mask2, qb[hq:] * mask1, qb[hq:] * mask2], axis=0)
        starts = [r0 - d * tq for d in range(n_prev, 0, -1)] + [r0]
        blocks = tuple((c0 if static else pl.multiple_of(c0, tq), max_prev - n_prev + j)
                       for j, c0 in enumerate(starts))
        return r0, qq, blocks

    def finish(r0, acc, den):
        o = acc * (1.0 / den)
        o = jnp.concatenate([jnp.where(feat < HEAD_DIM, o[:, h * tq:h * tq + hq], o[:, h * tq + hq:(h + 1) * tq])
                             for h in (0, 1)], axis=1)
        o_ref[pl.ds(r0, tq), :] = jnp.transpose(o).astype(o_ref.dtype)

    def bounded_tile(i, n_prev):
        r0, qq, blocks = tile_blocks(i, n_prev)
        qn2 = lax.dot_general(jnp.ones((8, LANES), BF16), qq * qq, (((1,), (1,)), ((), ())),
                              preferred_element_type=F32)[0:1]
        shift = jnp.sqrt(qn2 * NORM_SLACK_A) * kmax + (bmax_ref[:, 0:1] + 1.0)
        if n_prev == 1 and not isinstance(i, int):
            prev, own = bias_ref[0], bias_ref[1]
            chains = ((r0 - tq, slice(0, tq), jnp.concatenate([prev[:, :tq], own[:hq, :tq]], axis=0)),
                      (r0 - hq, slice(tq, 2 * tq), jnp.concatenate([prev[hq:, tq:], own[:, tq:]], axis=0)))
            accs, dens = [], []
            for start, cols, bias in chains:
                start = pl.multiple_of(start, hq)
                s = lax.dot_general(k_ref[pl.ds(start, tq + hq), :], qq[cols], (((1,), (1,)), ((), ())),
                                    preferred_element_type=F32) + bias
                p = jnp.exp2(s - shift[:, cols])
                dens.append(jnp.sum(p, axis=0, keepdims=True))
                accs.append(jnp.dot(vt_ref[:, pl.ds(start, tq + hq)], p.astype(BF16),
                                    preferred_element_type=F32))
            acc, den = jnp.concatenate(accs, axis=1), jnp.concatenate(dens, axis=1)
        else:
            acc = None
            den = None
            for c0, t in blocks:
                s = lax.dot_general(k_ref[pl.ds(c0, tq), :], qq, (((1,), (1,)), ((), ())),
                                    preferred_element_type=F32) + bias_ref[t]
                p = jnp.exp2(s - shift)
                ps = jnp.sum(p, axis=0, keepdims=True)
                pv = jnp.dot(vt_ref[:, pl.ds(c0, tq)], p.astype(BF16), preferred_element_type=F32)
                den = ps if den is None else den + ps
                acc = pv if acc is None else acc + pv
        finish(r0, acc, den)
        return jnp.min(den)

    def exact_tile(i, refs, n_prev):
        r0, qq, blocks = tile_blocks(i, n_prev)
        m = None
        for (c0, t), s_ref in zip(blocks, refs):
            s = lax.dot_general(k_ref[pl.ds(c0, tq), :], qq, (((1,), (1,)), ((), ())),
                                preferred_element_type=F32) + bias_ref[t]
            s_ref[...] = s
            bm = jnp.max(s, axis=0, keepdims=True)
            m = bm if m is None else jnp.maximum(m, bm)
        acc = None
        den = None
        for (c0, t), s_ref in zip(blocks, refs):
            p = jnp.exp2(s_ref[...] - m)
            ps = jnp.sum(p, axis=0, keepdims=True)
            pv = jnp.dot(vt_ref[:, pl.ds(c0, tq)], p.astype(BF16), preferred_element_type=F32)
            den = ps if den is None else den + ps
            acc = pv if acc is None else acc + pv
        finish(r0, acc, den)

    def tiles(idx, n_prev):
        mins = [bounded_tile(i, n_prev) for i in idx]
        for i, l_min in zip(idx, mins):
            @pl.when(jnp.logical_not(l_min > L_MIN_A))
            def _():
                exact_tile(i, s_refs, n_prev)

    for i in range(min(max_prev, nq)):
        tiles([i], i)

    def step(it, carry):
        tiles([max_prev + UNROLL_B * it + u for u in range(UNROLL_B)], max_prev)
        return carry

    n_rest = max(nq - max_prev, 0)
    lax.fori_loop(0, n_rest // UNROLL_B, step, 0)
    for i in range(nq - n_rest % UNROLL_B, nq):
        tiles([i], max_prev)


def _attn_b(proj, bias, bmax, layer, bsz, seq):
    tq = TQ_B
    col = lambda base: pl.BlockSpec((None, seq, LANES), lambda b, h: (base + h, b, 0))
    return pl.pallas_call(
        functools.partial(_attn_b_kernel, tq=tq),
        grid=(bsz, B_HEADS // 2),
        in_specs=[pl.BlockSpec((None, None, 1, LANES), lambda b, h: (layer, h, 0, 0)),
                  pl.BlockSpec((None, None, B_PAD // tq + 1, tq, 2 * tq), lambda b, h: (layer, h, 0, 0, 0)),
                  col(BQ0), col(BK0), col(BV0)],
        out_specs=pl.BlockSpec((None, seq, LANES), lambda b, h: (h, b, 0)),
        out_shape=jax.ShapeDtypeStruct((B_HEADS // 2, bsz * seq, LANES), BF16),
        scratch_shapes=[pltpu.VMEM((LANES, seq), BF16)]
        + [pltpu.VMEM((tq, 2 * tq), F32) for _ in range(B_PAD // tq + 1)],
        compiler_params=_cparams("parallel", "parallel"),
        name="attn_b",
    )(bmax, bias, proj, proj, proj)


def _attn_c_kernel(q_ref, k_ref, v_ref, o_ref, *, tq, group):
    seq = q_ref.shape[0]
    nq = seq // tq
    row = lax.broadcasted_iota(jnp.int32, (tq, tq), 0)
    colm = lax.broadcasted_iota(jnp.int32, (tq, tq), 1)
    tri_strict = (row > colm).astype(BF16)
    before = colm < row

    def block(i, j, carry, acc, diag):
        qb = q_ref[pl.ds(pl.multiple_of(i * tq, tq), tq), :]
        c0 = pl.multiple_of(j * tq, tq)
        kb = k_ref[pl.ds(c0, tq), :]
        vb = v_ref[pl.ds(c0, tq), :]
        z = lax.dot_general(qb, kb, (((1,), (1,)), ((), ())), preferred_element_type=F32)
        lm = -(jnp.maximum(z, 0.0) + jnp.log2(1.0 + jnp.exp2(-jnp.abs(z))))
        if diag:
            lm = jnp.where(before, lm, 0.0)
        lm_b = lm.astype(BF16)
        excl = jnp.dot(lm_b, tri_strict, preferred_element_type=F32)
        logw = (z + lm) + excl + carry
        wgt = jnp.exp2(logw)
        if diag:
            wgt = jnp.where(before, wgt, 0.0)
        acc = acc + jnp.dot(wgt.astype(BF16), vb, preferred_element_type=F32)
        carry = carry + excl[:, 0:1] + lm_b[:, 0:1].astype(F32)
        return carry, acc

    def qgroup(gi, c):
        i0 = gi * group
        sts = [block(i0 + g, i0 + g, jnp.zeros((tq, 1), F32), jnp.zeros((tq, LANES), F32), True)
               for g in range(group)]
        carries = tuple(st[0] for st in sts)
        accs = tuple(st[1] for st in sts)

        def live(carries):
            return functools.reduce(jnp.maximum, [jnp.max(cr) for cr in carries])

        def cond(st):
            t, cmax, _, _ = st
            return jnp.logical_and(t <= i0 + group - 1, cmax > C_SKIP_LOG2)

        def body(st):
            t, _, carries, accs = st
            new = []
            for g in range(group):
                j = i0 + g - t
                cin = jnp.where(j >= 0, carries[g], NEG_INF)
                new.append(block(i0 + g, jnp.maximum(j, 0), cin, accs[g], False))
            carries = tuple(st_[0] for st_ in new)
            return t + 1, live(carries), carries, tuple(st_[1] for st_ in new)

        _, _, _, accs = lax.while_loop(cond, body, (jnp.int32(1), live(carries), carries, accs))
        for g in range(group):
            o_ref[pl.ds(pl.multiple_of((i0 + g) * tq, tq), tq), :] = accs[g].astype(o_ref.dtype)
        return c

    lax.fori_loop(0, nq // group, qgroup, 0)


def _attn_c(proj, bsz, seq):
    tq = min(TQ_C, seq)
    group = min(GROUP_C, seq // tq)
    assert (seq // tq) % group == 0
    col = lambda base: pl.BlockSpec((None, seq, LANES), lambda b, h: (base + h, b, 0))
    return pl.pallas_call(
        functools.partial(_attn_c_kernel, tq=tq, group=group),
        grid=(bsz, C_HEADS),
        in_specs=[col(CQ0), col(CK0), col(CV0)],
        out_specs=pl.BlockSpec((None, seq, LANES), lambda b, h: (h, b, 0)),
        out_shape=jax.ShapeDtypeStruct((C_HEADS, bsz * seq, LANES), BF16),
        compiler_params=_cparams("parallel", "parallel"),
        name="attn_c",
    )(proj, proj, proj)


def _cat_lanes(ref, start, count):
    return jnp.concatenate([ref[start + c] for c in range(count)], axis=1)


def _merge_kernel(x_ref, mod_ref, gate_ref, bg_ref, ya_ref, yb_ref, yc_ref, wb_ref, wo_ref, o_ref):
    blks = D_MODEL // LANES
    merged = None
    for r, y_ref in enumerate((ya_ref, yb_ref, yc_ref)):
        y = _cat_lanes(y_ref, 0, BRANCH_WIDTH // LANES)
        br = jnp.dot(y, wb_ref[r], preferred_element_type=F32)
        pre = _cat_lanes(gate_ref, r * blks, blks).astype(F32) + bg_ref[:, r * D_MODEL:(r + 1) * D_MODEL]
        term = jax.nn.sigmoid(pre) * br
        merged = term if merged is None else merged + term
    out = jnp.dot(merged.astype(BF16), wo_ref[...], preferred_element_type=F32)
    o_ref[...] = x_ref[...] + mod_ref[2:3, :] * out


def _merge(x2d, mod, proj, b_gate, ya, yb, yc, w_branch, w_out, layer, seq):
    t, d = x2d.shape
    tm = min(TM_MERGE, seq)
    ybs = lambda: pl.BlockSpec((BRANCH_WIDTH // LANES, tm, LANES), lambda i: (0, i, 0))
    return pl.pallas_call(
        _merge_kernel,
        grid=(t // tm,),
        in_specs=[
            pl.BlockSpec((tm, d), lambda i: (i, 0)),
            pl.BlockSpec((None, None, 6, d), lambda i: (layer, (i * tm) // seq, 0, 0)),
            pl.BlockSpec((GATE_BLKS, tm, LANES), lambda i: (0, i, 0)),
            pl.BlockSpec((None, 1, GATE_COLS), lambda i: (layer, 0, 0)),
            ybs(), ybs(), ybs(),
            pl.BlockSpec((None, N_BRANCH, BRANCH_WIDTH, d), lambda i: (layer, 0, 0, 0)),
            pl.BlockSpec((None, d, d), lambda i: (layer, 0, 0)),
        ],
        out_specs=pl.BlockSpec((tm, d), lambda i: (i, 0)),
        out_shape=jax.ShapeDtypeStruct((t, d), F32),
        compiler_params=_cparams("parallel"),
        name="merge",
    )(x2d, mod, proj, b_gate, ya, yb, yc, w_branch, w_out)


def _ffn_kernel(x_ref, mod_ref, g_ref, fg_ref, w13_ref, w2_ref, o_ref, *, final):
    dff = w2_ref.shape[0]
    x = x_ref[...]
    h = _norm_mod(x, g_ref[...], mod_ref[3:4, :], mod_ref[4:5, :]).astype(BF16)
    acc = None
    for f0, f1 in FFN_CHUNKS:
        u_gate = jnp.dot(h, w13_ref[:, f0:f1], preferred_element_type=F32)
        u_up = jnp.dot(h, w13_ref[:, dff + f0:dff + f1], preferred_element_type=F32)
        act = (u_gate * jax.nn.sigmoid(u_gate) * u_up).astype(BF16)
        part = jnp.dot(act, w2_ref[f0:f1, :], preferred_element_type=F32)
        acc = part if acc is None else acc + part
    y = x + mod_ref[5:6, :] * acc
    if final:
        y = y * lax.rsqrt(jnp.mean(y * y, axis=-1, keepdims=True) + EPS) * fg_ref[...]
    o_ref[...] = y


def _ffn(x2d, mod, g, final_g, w13, w2, layer, seq, final):
    t, d = x2d.shape
    dff = w2.shape[1]
    assert FFN_CHUNKS[0][0] == 0 and FFN_CHUNKS[-1][1] == dff
    tm = min(TM_FFN, seq)
    resident = lambda shape: pl.BlockSpec((None,) + shape, lambda i: (layer, 0, 0), pipeline_mode=pl.Buffered(1))
    return pl.pallas_call(
        functools.partial(_ffn_kernel, final=final),
        grid=(t // tm,),
        in_specs=[
            pl.BlockSpec((tm, d), lambda i: (i, 0)),
            pl.BlockSpec((None, None, 6, d), lambda i: (layer, (i * tm) // seq, 0, 0)),
            pl.BlockSpec((None, 1, d), lambda i: (layer, 0, 0)),
            pl.BlockSpec((1, d), lambda i: (0, 0)),
            resident((d, 2 * dff)),
            resident((dff, d)),
        ],
        out_specs=pl.BlockSpec((tm, d), lambda i: (i, 0)),
        out_shape=jax.ShapeDtypeStruct((t, d), F32),
        compiler_params=_cparams("parallel"),
        name="ffn",
    )(x2d, mod, g, final_g, w13, w2)


def _prep_w_in(w_in):
    scale = np.ones((IN_COLS,), np.float32)
    scale[0 * BRANCH_WIDTH:1 * BRANCH_WIDTH] = HEAD_DIM ** -0.5 * LOG2E
    scale[3 * BRANCH_WIDTH:4 * BRANCH_WIDTH] = HEAD_DIM ** -0.5 * LOG2E
    scale[6 * BRANCH_WIDTH:7 * BRANCH_WIDTH] = C_HEAD_DIM ** -0.5 * LOG2E
    w = w_in * jnp.asarray(scale)
    return jnp.concatenate([w[..., QKV_COLS:], w[..., :QKV_COLS]], axis=-1).astype(BF16)


def kernel(x, c, w_ada, b_ada, norm1_g, w_in, b_gate, lam_q1, lam_k1, lam_q2, lam_k2, subln_g,
           t5_table, rel_bias_b, w_branch, w_out, norm2_g, w13, w2, final_g):
    bsz, seq, d = x.shape
    depth = w_in.shape[0]
    t = bsz * seq
    assert d == D_MODEL and w_in.shape[2] == IN_COLS and w2.shape[1] == D_FF
    assert seq % (2 * B_PAD) == 0, "sequence tiles assume a multiple of 1024 frames"

    mod = _ada_mod(c, w_ada, b_ada).reshape(depth, bsz, 6, d)
    w_in_b = _prep_w_in(w_in)
    w_branch_b = w_branch.astype(BF16)
    w_out_b = w_out.astype(BF16)
    w13_b = w13.astype(BF16)
    w2_b = w2.astype(BF16)
    tq_a = min(TQ_A, seq // 2)
    a_bias, a_bmax = _a_bias_tiles(t5_table, tq_a, min(TK_A, tq_a))
    b_bias, b_bmax = _b_bias_tiles(rel_bias_b, TQ_B)
    fg = final_g.reshape(1, d)
    row = lambda p: p.reshape(depth, 1, -1)
    g1, g2, bg = row(norm1_g), row(norm2_g), row(b_gate)
    lams = [row(p) for p in (lam_q1, lam_k1, lam_q2, lam_k2)]
    subg = subln_g.reshape(depth, -1, 1)

    x2d = x.reshape(t, d)
    for l in range(depth):
        lam_init = 0.8 - 0.6 * math.exp(-0.3 * l)
        proj = _inproj(x2d, mod, g1, w_in_b, l, seq)
        ya = _attn_a(proj, a_bias, a_bmax, *lams, subg, l, bsz, seq, lam_init)
        yb = _attn_b(proj, b_bias, b_bmax, l, bsz, seq)
        yc = _attn_c(proj, bsz, seq)
        x2d = _merge(x2d, mod, proj, bg, ya, yb, yc, w_branch_b, w_out_b, l, seq)
        x2d = _ffn(x2d, mod, g2, fg, w13_b, w2_b, l, seq, final=(l == depth - 1))
    return x2d.reshape(bsz, seq, d)
```

```python
import functools
import math

import numpy as np
import jax
import jax.numpy as jnp
from jax import lax
from jax.experimental import pallas as pl
from jax.experimental.pallas import tpu as pltpu

F32 = jnp.float32
BF16 = jnp.bfloat16

D_MODEL = 1024
CHUNK = 64
HEAD_DIM = 64
BRANCH_WIDTH = 512
N_BRANCH = 3
A_HEADS = 4
B_HEADS = 8
C_HEADS = 4
C_HEAD_DIM = 128
B_LEFT_CHUNKS = 8
B_PAD = B_LEFT_CHUNKS * CHUNK
B_MAX_REL = 128
T5_BUCKETS = 32
T5_MAX_DIST = 128
D_FF = 2816
QKV_COLS = 3 * N_BRANCH * BRANCH_WIDTH
GATE_COLS = N_BRANCH * D_MODEL
IN_COLS = QKV_COLS + GATE_COLS
NEG_INF = -1e30
EPS = 1e-6
LOG2E = math.log2(math.e)

LANES = 128
GATE_BLKS = GATE_COLS // LANES
AQ0, AK0, AV0 = GATE_BLKS, GATE_BLKS + 4, GATE_BLKS + 8
BQ0, BK0, BV0 = GATE_BLKS + 12, GATE_BLKS + 16, GATE_BLKS + 20
CQ0, CK0, CV0 = GATE_BLKS + 24, GATE_BLKS + 28, GATE_BLKS + 32

TM_PROJ = 512
TN_PROJ = 1536
TM_FFN = 512
FFN_CHUNKS = ((0, 1536), (1536, 2816))
TQ_A = 1024
TK_A = 512
TP_A = 512
UNROLL_A = 2
NORM_SLACK_A = 1.02
L_MIN_A = 2.0 ** -100
TQ_B = 512
UNROLL_B = 3
TQ_C = 256
GROUP_C = 16
C_SKIP_LOG2 = -150.0
VMEM_LIMIT = 56 * 1024 * 1024


def _cparams(*sem):
    return pltpu.CompilerParams(dimension_semantics=sem, vmem_limit_bytes=VMEM_LIMIT)


def _ada_kernel(c_ref, w_ref, b_ref, o_ref):
    c = c_ref[...]
    cs = c * jax.nn.sigmoid(c)
    o_ref[...] = jnp.dot(cs, w_ref[...], preferred_element_type=F32,
                         precision=lax.Precision.HIGHEST) + b_ref[...]


def _ada_mod(c, w_ada, b_ada):
    depth, d, e = w_ada.shape
    bsz = c.shape[0]
    nblk = e // d
    return pl.pallas_call(
        _ada_kernel,
        grid=(depth, nblk),
        in_specs=[
            pl.BlockSpec((bsz, d), lambda l, j: (0, 0)),
            pl.BlockSpec((None, d, d), lambda l, j: (l, 0, j)),
            pl.BlockSpec((None, 1, d), lambda l, j: (l, 0, j)),
        ],
        out_specs=pl.BlockSpec((None, bsz, d), lambda l, j: (l, 0, j)),
        out_shape=jax.ShapeDtypeStruct((depth, bsz, e), F32),
        compiler_params=_cparams("arbitrary", "arbitrary"),
        name="ada_mod",
    )(c, w_ada, b_ada.reshape(depth, 1, e))


def _norm_mod(x, g, shift, scale):
    ms = jnp.mean(x * x, axis=-1, keepdims=True)
    y = x * lax.rsqrt(ms + EPS) * g
    return y * (1.0 + scale) + shift


def _inproj_kernel(x_ref, mod_ref, g_ref, w_ref, o_ref):
    h = _norm_mod(x_ref[...], g_ref[...], mod_ref[0:1, :], mod_ref[1:2, :]).astype(BF16)
    n = w_ref.shape[1]
    for c0 in range(0, n, TN_PROJ):
        res = jnp.dot(h, w_ref[:, c0:c0 + TN_PROJ], preferred_element_type=F32)
        for cb in range(TN_PROJ // LANES):
            o_ref[c0 // LANES + cb] = res[:, cb * LANES:(cb + 1) * LANES].astype(o_ref.dtype)


def _inproj(x2d, mod, g, w, layer, seq):
    t, d = x2d.shape
    n = w.shape[2]
    assert n % TN_PROJ == 0
    tm = min(TM_PROJ, seq)
    return pl.pallas_call(
        _inproj_kernel,
        grid=(t // tm,),
        in_specs=[
            pl.BlockSpec((tm, d), lambda i: (i, 0)),
            pl.BlockSpec((None, None, 6, d), lambda i: (layer, (i * tm) // seq, 0, 0)),
            pl.BlockSpec((None, 1, d), lambda i: (layer, 0, 0)),
            pl.BlockSpec((None, d, n), lambda i: (layer, 0, 0), pipeline_mode=pl.Buffered(1)),
        ],
        out_specs=pl.BlockSpec((n // LANES, tm, LANES), lambda i: (0, i, 0)),
        out_shape=jax.ShapeDtypeStruct((n // LANES, t, LANES), BF16),
        compiler_params=_cparams("parallel"),
        name="in_proj",
    )(x2d, mod, g, w)


def _t5_bucket_np(rel):
    nb = T5_BUCKETS // 2
    max_exact = nb // 2
    ret = np.where(rel > 0, nb, 0)
    n = np.abs(rel)
    nf = np.maximum(n, 1).astype(np.float32)
    scaled = (np.log(nf / np.float32(max_exact)) / np.float32(math.log(T5_MAX_DIST / max_exact))
              * np.float32(nb - max_exact))
    large = max_exact + scaled.astype(np.int32)
    large = np.minimum(large, nb - 1)
    return (ret + np.where(n < max_exact, n, large)).astype(np.int32)


def _toeplitz(vec, rows, cols):
    length = rows + cols - 1
    assert vec.shape[-1] == length
    lead = vec.shape[:-1]
    ext = jnp.concatenate([vec, jnp.zeros(lead + (1,), vec.dtype)], axis=-1)
    flat = jnp.broadcast_to(ext[..., None, :], lead + (rows, length + 1)).reshape(lead + (rows * (length + 1),))
    return flat[..., :rows * length].reshape(lead + (rows, length))[..., rows - 1:rows - 1 + cols]


def _a_bias_tiles(t5_table, tq, tk):
    r = np.arange(tq)[None, :]
    c = np.arange(tk)[:, None]
    far_bucket = _t5_bucket_np((c - 2 * tk) - r)
    assert (far_bucket == far_bucket[0, 0]).all() and far_bucket[0, 0] == _t5_bucket_np(np.array(-10 * tq))
    far = t5_table[int(far_bucket[0, 0])].astype(F32)
    tiles = [jnp.zeros((A_HEADS, tk, tq), F32)]
    bmax = jnp.zeros((A_HEADS,), F32)
    u = np.arange(tk + tq - 1)
    for koff in range(-tk, tq, tk):
        vec = t5_table[_t5_bucket_np(koff + tk - 1 - u)].astype(F32).T
        vec = (vec - far[:, None]) * LOG2E
        bmax = jnp.maximum(bmax, jnp.max(vec, axis=1))
        bias = _toeplitz(vec, tk, tq)
        allowed = ((c + koff) // CHUNK) <= (r // CHUNK)
        tiles.append(jnp.where(jnp.asarray(allowed)[None], bias, NEG_INF))
    tiles.append(jnp.full((A_HEADS, tk, tq), NEG_INF, F32))
    return jnp.stack(tiles, axis=1), jnp.broadcast_to(bmax[:, None, None], (A_HEADS, 1, LANES))


def _attn_a_kernel(lq1_ref, lk1_ref, lq2_ref, lk2_ref, subg_ref, bmax_ref, bias_ref, q_ref, k_ref, v_ref,
                   o_ref, vt_ref, acc_ref, *s_refs, tq, tk, lam_init):
    seq = q_ref.shape[0]
    nq = seq // tq
    nkb = seq // tk
    per_q = tq // tk
    unroll = len(s_refs)
    tp = TP_A
    lam = (jnp.exp(jnp.sum(lq1_ref[...] * lk1_ref[...], axis=-1, keepdims=True))
           - jnp.exp(jnp.sum(lq2_ref[...] * lk2_ref[...], axis=-1, keepdims=True)) + lam_init)
    lane = lax.broadcasted_iota(jnp.int32, (1, LANES), 1)
    mask1 = (lane < HEAD_DIM).astype(BF16)
    mask2 = (lane >= HEAD_DIM).astype(BF16)
    subg = subg_ref[...] * (1.0 - lam_init)
    sel = (lax.broadcasted_iota(jnp.int32, (8, LANES), 0)
           == lax.broadcasted_iota(jnp.int32, (8, LANES), 1) // HEAD_DIM).astype(BF16)

    def prepare(b, kn2):
        c0 = pl.multiple_of(b * tk, tk)
        vt_ref[:, pl.ds(c0, tk)] = jnp.transpose(v_ref[pl.ds(c0, tk), :])
        kb = k_ref[pl.ds(c0, tk), :]
        blk = lax.dot_general(sel, kb * kb, (((1,), (1,)), ((), ())), preferred_element_type=F32)
        return jnp.maximum(kn2, blk)

    kn2 = lax.fori_loop(0, nkb, prepare, jnp.zeros((8, tk), F32))
    kmax2 = jnp.max(kn2, axis=1, keepdims=True) * NORM_SLACK_A
    kmax = jnp.sqrt(jnp.concatenate([jnp.broadcast_to(kmax2[0:1], (1, tq)),
                                     jnp.broadcast_to(kmax2[1:2], (1, tq))], axis=1))

    def finish(i, acc, l):
        o = acc * (1.0 / l)
        d = o[:, :tq] - lam * o[:, tq:]
        y = d * lax.rsqrt(jnp.mean(d * d, axis=0, keepdims=True) + EPS) * subg
        o_ref[pl.ds(pl.multiple_of(i * tq, tq), tq), :] = jnp.transpose(y).astype(o_ref.dtype)

    def load_q(i):
        qb = q_ref[pl.ds(pl.multiple_of(i * tq, tq), tq), :]
        return jnp.concatenate([qb * mask1, qb * mask2], axis=0)

    def steps(i, step, st):
        n_plain = jnp.maximum(per_q * i - 1, 0) // unroll
        n_steps = (per_q * (i + 1) + unroll - 1) // unroll
        st = lax.fori_loop(0, n_plain, lambda it, s_: step(it, s_, False), st)
        return lax.fori_loop(n_plain, n_steps, lambda it, s_: step(it, s_, True), st)

    def bounded_pass(i):
        qq = load_q(i)
        qn2 = lax.dot_general(jnp.ones((8, LANES), BF16), qq * qq, (((1,), (1,)), ((), ())),
                              preferred_element_type=F32)[0:1]
        shift = jnp.sqrt(qn2 * NORM_SLACK_A) * kmax + (bmax_ref[:, 0:1] + 1.0)
        acc_ref[...] = jnp.zeros(acc_ref.shape, F32)

        def block(js, l, bias, late_queries_only=False):
            c0 = pl.multiple_of(js * tk, tk)
            kb = k_ref[pl.ds(c0, tk), :]
            vt = vt_ref[:, pl.ds(c0, tk)]
            sums = []
            for m0 in (0, tq):
                cols = slice(m0 + tq // 2, m0 + tq) if late_queries_only else slice(m0, m0 + tq)
                s = lax.dot_general(kb, qq[cols], (((1,), (1,)), ((), ())), preferred_element_type=F32)
                if bias is not None:
                    s = s + jnp.concatenate([bias, bias], axis=1)[:, cols]
                p = jnp.exp2(s - shift[:, cols])
                ps = jnp.sum(p, axis=0, keepdims=True)
                if late_queries_only:
                    ps = jnp.concatenate([jnp.zeros((1, tq // 2), F32), ps], axis=1)
                sums.append(ps)
                acc_ref[:, cols] += jnp.dot(vt, p.astype(BF16), preferred_element_type=F32)
            return l + jnp.concatenate(sums, axis=1)

        def plain_blocks(first, count, l):
            for u in range(count):
                l = block(first + u, l, None)
            return l

        n_pairs = jnp.maximum(i - 1, 0)
        l = lax.fori_loop(0, n_pairs // 2, lambda it, l: plain_blocks(4 * it, 4, l),
                          jnp.zeros((1, 2 * tq), F32))
        l = lax.cond(n_pairs % 2 == 1, lambda l: plain_blocks(2 * n_pairs - 2, 2, l), lambda l: l, l)

        def own_blocks(l):
            return block(2 * i + 1, block(2 * i, l, bias_ref[2]), bias_ref[3], late_queries_only=True)

        def previous_and_own_blocks(l):
            return own_blocks(block(2 * i - 1, block(2 * i - 2, l, None), bias_ref[1]))

        l = lax.cond(i > 0, previous_and_own_blocks, own_blocks, l)
        finish(i, acc_ref[...], l)
        return jnp.min(l)

    def running_max_pass(i):
        qq = load_q(i)
        acc_ref[...] = jnp.zeros(acc_ref.shape, F32)

        def block(js, st, with_bias, s_ref):
            m, l = st
            c0 = pl.multiple_of(jnp.minimum(js, nkb - 1) * tk, tk)
            kb = k_ref[pl.ds(c0, tk), :]
            s = lax.dot_general(kb, qq, (((1,), (1,)), ((), ())), preferred_element_type=F32)
            if with_bias:
                bias = bias_ref[jnp.clip(js - per_q * i + 2, 0, per_q + 2)]
                s = s + jnp.concatenate([bias, bias], axis=1)
            s_ref[...] = s
            m_new = jnp.maximum(m, jnp.max(s, axis=0, keepdims=True))
            alpha = jnp.exp2(m - m_new)
            sums = []
            for ct in range(2 * tq // tp):
                cols = slice(ct * tp, (ct + 1) * tp)
                p = jnp.exp2(s_ref[:, cols] - m_new[:, cols])
                sums.append(jnp.sum(p, axis=0, keepdims=True))
                vt = vt_ref[:, pl.ds(c0, tk)]
                acc_ref[:, cols] = (alpha[:, cols] * acc_ref[:, cols]
                                    + jnp.dot(vt, p.astype(BF16), preferred_element_type=F32))
            return m_new, alpha * l + jnp.concatenate(sums, axis=1)

        def step(it, st, with_bias):
            for u in range(unroll):
                st = block(it * unroll + u, st, with_bias, s_refs[u])
            return st

        _, l = steps(i, step, (jnp.full((1, 2 * tq), NEG_INF, F32), jnp.zeros((1, 2 * tq), F32)))
        finish(i, acc_ref[...], l)

    def qtile(i, carry):
        l_min = bounded_pass(i)

        @pl.when(jnp.logical_not(l_min > L_MIN_A))
        def _():
            running_max_pass(i)

        return carry

    lax.fori_loop(0, nq, qtile, 0)


def _attn_a(proj, bias, bmax, lq1, lk1, lq2, lk2, subg, layer, bsz, seq, lam_init):
    tq = min(TQ_A, seq // 2)
    tk = min(TK_A, tq)
    assert tk == TP_A and tq == 2 * tk and UNROLL_A == 2
    vec = lambda n: pl.BlockSpec((None, 1, n), lambda b, h: (layer, 0, 0))
    col = lambda base: pl.BlockSpec((None, seq, LANES), lambda b, h: (base + h, b, 0))
    return pl.pallas_call(
        functools.partial(_attn_a_kernel, tq=tq, tk=tk, lam_init=lam_init),
        grid=(bsz, A_HEADS),
        in_specs=[vec(HEAD_DIM), vec(HEAD_DIM), vec(HEAD_DIM), vec(HEAD_DIM),
                  pl.BlockSpec((None, 2 * HEAD_DIM, 1), lambda b, h: (layer, 0, 0)),
                  pl.BlockSpec((None, 1, LANES), lambda b, h: (h, 0, 0)),
                  pl.BlockSpec((None, tq // tk + 3, tk, tq), lambda b, h: (h, 0, 0, 0)),
                  col(AQ0), col(AK0), col(AV0)],
        out_specs=pl.BlockSpec((None, seq, LANES), lambda b, h: (h, b, 0)),
        out_shape=jax.ShapeDtypeStruct((A_HEADS, bsz * seq, LANES), BF16),
        scratch_shapes=[pltpu.VMEM((LANES, seq), BF16), pltpu.VMEM((LANES, 2 * tq), F32)]
        + [pltpu.VMEM((tk, 2 * tq), F32) for _ in range(UNROLL_A)],
        compiler_params=_cparams("parallel", "parallel"),
        name="attn_a",
    )(lq1, lk1, lq2, lk2, subg, bmax, bias, proj, proj, proj)


def _b_bias_tiles(rel_bias, tq):
    depth = rel_bias.shape[0]
    assert B_PAD % tq == 0
    r = np.arange(tq)[None, :]
    c = np.arange(tq)[:, None]
    lo = CHUNK * (r // CHUNK)
    u = np.arange(2 * tq - 1)
    tiles = []
    for koff in range(-B_PAD, tq, tq):
        in_band = (c + koff >= lo - B_PAD) & (c + koff < lo + CHUNK)
        idx = np.clip(u - (tq - 1) - koff, -B_MAX_REL, B_MAX_REL) + B_MAX_REL
        bias = _toeplitz(rel_bias[..., idx].astype(F32) * LOG2E, tq, tq)
        bias = jnp.where(jnp.asarray(in_band), bias, NEG_INF)
        tiles.append(bias.reshape(depth, B_HEADS // 2, 2, tq, 2, tq // 2).transpose(0, 1, 3, 4, 2, 5)
                     .reshape(depth, B_HEADS // 2, tq, 2 * tq))
    bmax = jnp.max((rel_bias.astype(F32) * LOG2E).reshape(depth, B_HEADS // 2, -1), axis=2)
    return (jnp.stack(tiles, axis=2),
            jnp.broadcast_to(bmax[:, :, None, None], (depth, B_HEADS // 2, 1, LANES)))


def _attn_b_kernel(bmax_ref, bias_ref, q_ref, k_ref, v_ref, o_ref, vt_ref, *s_refs, tq):
    seq = q_ref.shape[0]
    nq = seq // tq
    max_prev = B_PAD // tq
    lane = lax.broadcasted_iota(jnp.int32, (1, LANES), 1)
    mask1 = (lane < HEAD_DIM).astype(BF16)
    mask2 = (lane >= HEAD_DIM).astype(BF16)
    hq = tq // 2
    feat = lax.broadcasted_iota(jnp.int32, (LANES, hq), 0)
    sel = (lax.broadcasted_iota(jnp.int32, (8, LANES), 0)
           == lax.broadcasted_iota(jnp.int32, (8, LANES), 1) // HEAD_DIM).astype(BF16)

    def prepare(b, kn2):
        c0 = pl.multiple_of(b * tq, tq)
        vt_ref[:, pl.ds(c0, tq)] = jnp.transpose(v_ref[pl.ds(c0, tq), :])
        kb = k_ref[pl.ds(c0, tq), :]
        blk = lax.dot_general(sel, kb * kb, (((1,), (1,)), ((), ())), preferred_element_type=F32)
        return jnp.maximum(kn2, blk)

    kn2 = lax.fori_loop(0, nq, prepare, jnp.zeros((8, tq), F32))
    kmax2 = jnp.max(kn2, axis=1, keepdims=True) * NORM_SLACK_A
    kmax = jnp.sqrt(jnp.concatenate([jnp.broadcast_to(kmax2[h:h + 1], (1, hq)) for h in (0, 1, 0, 1)],
                                    axis=1))

    def tile_blocks(i, n_prev):
        static = isinstance(i, int)
        r0 = i * tq if static else pl.multiple_of(i * tq, tq)
        qb = q_ref[pl.ds(r0, tq), :]
        qq = jnp.concatenate([qb[:hq] * mask1, qb[:hq] * mask2, qb[hq:] * mask1, qb[hq:] * mask2], axis=0)
        starts = [r0 - d * tq for d in range(n_prev, 0, -1)] + [r0]
        blocks = tuple((c0 if static else pl.multiple_of(c0, tq), max_prev - n_prev + j)
                       for j, c0 in enumerate(starts))
        return r0, qq, blocks

    def finish(r0, acc, den):
        o = acc * (1.0 / den)
        o = jnp.concatenate([jnp.where(feat < HEAD_DIM, o[:, h * tq:h * tq + hq], o[:, h * tq + hq:(h + 1) * tq])
                             for h in (0, 1)], axis=1)
        o_ref[pl.ds(r0, tq), :] = jnp.transpose(o).astype(o_ref.dtype)

    def bounded_tile(i, n_prev):
        r0, qq, blocks = tile_blocks(i, n_prev)
        qn2 = lax.dot_general(jnp.ones((8, LANES), BF16), qq * qq, (((1,), (1,)), ((), ())),
                              preferred_element_type=F32)[0:1]
        shift = jnp.sqrt(qn2 * NORM_SLACK_A) * kmax + (bmax_ref[:, 0:1] + 1.0)
        if n_prev == 1 and not isinstance(i, int):
            prev, own = bias_ref[0], bias_ref[1]
            chains = ((r0 - tq, slice(0, tq), jnp.concatenate([prev[:, :tq], own[:hq, :tq]], axis=0)),
                      (r0 - hq, slice(tq, 2 * tq), jnp.concatenate([prev[hq:, tq:], own[:, tq:]], axis=0)))
            accs, dens = [], []
            for start, cols, bias in chains:
                start = pl.multiple_of(start, hq)
                s = lax.dot_general(k_ref[pl.ds(start, tq + hq), :], qq[cols], (((1,), (1,)), ((), ())),
                                    preferred_element_type=F32) + bias
                p = jnp.exp2(s - shift[:, cols])
                dens.append(jnp.sum(p, axis=0, keepdims=True))
                accs.append(jnp.dot(vt_ref[:, pl.ds(start, tq + hq)], p.astype(BF16),
                                    preferred_element_type=F32))
            acc, den = jnp.concatenate(accs, axis=1), jnp.concatenate(dens, axis=1)
        else:
            acc = None
            den = None
            for c0, t in blocks:
                s = lax.dot_general(k_ref[pl.ds(c0, tq), :], qq, (((1,), (1,)), ((), ())),
                                    preferred_element_type=F32) + bias_ref[t]
                p = jnp.exp2(s - shift)
                ps = jnp.sum(p, axis=0, keepdims=True)
                pv = jnp.dot(vt_ref[:, pl.ds(c0, tq)], p.astype(BF16), preferred_element_type=F32)
                den = ps if den is None else den + ps
                acc = pv if acc is None else acc + pv
        finish(r0, acc, den)
        return jnp.min(den)

    def exact_tile(i, refs, n_prev):
        r0, qq, blocks = tile_blocks(i, n_prev)
        m = None
        for (c0, t), s_ref in zip(blocks, refs):
            s = lax.dot_general(k_ref[pl.ds(c0, tq), :], qq, (((1,), (1,)), ((), ())),
                                preferred_element_type=F32) + bias_ref[t]
            s_ref[...] = s
            bm = jnp.max(s, axis=0, keepdims=True)
            m = bm if m is None else jnp.maximum(m, bm)
        acc = None
        den = None
        for (c0, t), s_ref in zip(blocks, refs):
            p = jnp.exp2(s_ref[...] - m)
            ps = jnp.sum(p, axis=0, keepdims=True)
            pv = jnp.dot(vt_ref[:, pl.ds(c0, tq)], p.astype(BF16), preferred_element_type=F32)
            den = ps if den is None else den + ps
            acc = pv if acc is None else acc + pv
        finish(r0, acc, den)

    def tiles(idx, n_prev):
        mins = [bounded_tile(i, n_prev) for i in idx]
        for i, l_min in zip(idx, mins):
            @pl.when(jnp.logical_not(l_min > L_MIN_A))
            def _():
                exact_tile(i, s_refs, n_prev)

    for i in range(min(max_prev, nq)):
        tiles([i], i)

    def step(it, carry):
        tiles([max_prev + UNROLL_B * it + u for u in range(UNROLL_B)], max_prev)
        return carry

    n_rest = max(nq - max_prev, 0)
    lax.fori_loop(0, n_rest // UNROLL_B, step, 0)
    for i in range(nq - n_rest % UNROLL_B, nq):
        tiles([i], max_prev)


def _attn_b(proj, bias, bmax, layer, bsz, seq):
    tq = TQ_B
    col = lambda base: pl.BlockSpec((None, seq, LANES), lambda b, h: (base + h, b, 0))
    return pl.pallas_call(
        functools.partial(_attn_b_kernel, tq=tq),
        grid=(bsz, B_HEADS // 2),
        in_specs=[pl.BlockSpec((None, None, 1, LANES), lambda b, h: (layer, h, 0, 0)),
                  pl.BlockSpec((None, None, B_PAD // tq + 1, tq, 2 * tq), lambda b, h: (layer, h, 0, 0, 0)),
                  col(BQ0), col(BK0), col(BV0)],
        out_specs=pl.BlockSpec((None, seq, LANES), lambda b, h: (h, b, 0)),
        out_shape=jax.ShapeDtypeStruct((B_HEADS // 2, bsz * seq, LANES), BF16),
        scratch_shapes=[pltpu.VMEM((LANES, seq), BF16)]
        + [pltpu.VMEM((tq, 2 * tq), F32) for _ in range(B_PAD // tq + 1)],
        compiler_params=_cparams("parallel", "parallel"),
        name="attn_b",
    )(bmax, bias, proj, proj, proj)


def _attn_c_kernel(q_ref, k_ref, v_ref, o_ref, *, tq, group):
    seq = q_ref.shape[0]
    nq = seq // tq
    row = lax.broadcasted_iota(jnp.int32, (tq, tq), 0)
    colm = lax.broadcasted_iota(jnp.int32, (tq, tq), 1)
    tri_strict = (row > colm).astype(BF16)
    before = colm < row

    def block(i, j, carry, acc, diag):
        qb = q_ref[pl.ds(pl.multiple_of(i * tq, tq), tq), :]
        c0 = pl.multiple_of(j * tq, tq)
        kb = k_ref[pl.ds(c0, tq), :]
        vb = v_ref[pl.ds(c0, tq), :]
        z = lax.dot_general(qb, kb, (((1,), (1,)), ((), ())), preferred_element_type=F32)
        lm = -(jnp.maximum(z, 0.0) + jnp.log2(1.0 + jnp.exp2(-jnp.abs(z))))
        if diag:
            lm = jnp.where(before, lm, 0.0)
        lm_b = lm.astype(BF16)
        excl = jnp.dot(lm_b, tri_strict, preferred_element_type=F32)
        logw = (z + lm) + excl + carry
        wgt = jnp.exp2(logw)
        if diag:
            wgt = jnp.where(before, wgt, 0.0)
        acc = acc + jnp.dot(wgt.astype(BF16), vb, preferred_element_type=F32)
        carry = carry + excl[:, 0:1] + lm_b[:, 0:1].astype(F32)
        return carry, acc

    def qgroup(gi, c):
        i0 = gi * group
        sts = [block(i0 + g, i0 + g, jnp.zeros((tq, 1), F32), jnp.zeros((tq, LANES), F32), True)
               for g in range(group)]
        carries = tuple(st[0] for st in sts)
        accs = tuple(st[1] for st in sts)

        def live(carries):
            return functools.reduce(jnp.maximum, [jnp.max(cr) for cr in carries])

        def cond(st):
            t, cmax, _, _ = st
            return jnp.logical_and(t <= i0 + group - 1, cmax > C_SKIP_LOG2)

        def body(st):
            t, _, carries, accs = st
            new = []
            for g in range(group):
                j = i0 + g - t
                cin = jnp.where(j >= 0, carries[g], NEG_INF)
                new.append(block(i0 + g, jnp.maximum(j, 0), cin, accs[g], False))
            carries = tuple(st_[0] for st_ in new)
            return t + 1, live(carries), carries, tuple(st_[1] for st_ in new)

        _, _, _, accs = lax.while_loop(cond, body, (jnp.int32(1), live(carries), carries, accs))
        for g in range(group):
            o_ref[pl.ds(pl.multiple_of((i0 + g) * tq, tq), tq), :] = accs[g].astype(o_ref.dtype)
        return c

    lax.fori_loop(0, nq // group, qgroup, 0)


def _attn_c(proj, bsz, seq):
    tq = min(TQ_C, seq)
    group = min(GROUP_C, seq // tq)
    assert (seq // tq) % group == 0
    col = lambda base: pl.BlockSpec((None, seq, LANES), lambda b, h: (base + h, b, 0))
    return pl.pallas_call(
        functools.partial(_attn_c_kernel, tq=tq, group=group),
        grid=(bsz, C_HEADS),
        in_specs=[col(CQ0), col(CK0), col(CV0)],
        out_specs=pl.BlockSpec((None, seq, LANES), lambda b, h: (h, b, 0)),
        out_shape=jax.ShapeDtypeStruct((C_HEADS, bsz * seq, LANES), BF16),
        compiler_params=_cparams("parallel", "parallel"),
        name="attn_c",
    )(proj, proj, proj)


def _cat_lanes(ref, start, count):
    return jnp.concatenate([ref[start + c] for c in range(count)], axis=1)


def _merge_ffn_kernel(x_ref, mod_ref, gate_ref, bg_ref, ya_ref, yb_ref, yc_ref, wb_ref, wo_ref,
                      g_ref, fg_ref, w13_ref, w2_ref, o_ref, *, final):
    blks = D_MODEL // LANES
    dff = w2_ref.shape[0]
    merged = None
    for r, y_ref in enumerate((ya_ref, yb_ref, yc_ref)):
        y = _cat_lanes(y_ref, 0, BRANCH_WIDTH // LANES)
        br = jnp.dot(y, wb_ref[r], preferred_element_type=F32)
        pre = _cat_lanes(gate_ref, r * blks, blks).astype(F32) + bg_ref[:, r * D_MODEL:(r + 1) * D_MODEL]
        term = jax.nn.sigmoid(pre) * br
        merged = term if merged is None else merged + term
    out = jnp.dot(merged.astype(BF16), wo_ref[...], preferred_element_type=F32)
    x = x_ref[...] + mod_ref[2:3, :] * out

    h = _norm_mod(x, g_ref[...], mod_ref[3:4, :], mod_ref[4:5, :]).astype(BF16)
    acc = None
    for f0, f1 in FFN_CHUNKS:
        u_gate = jnp.dot(h, w13_ref[:, f0:f1], preferred_element_type=F32)
        u_up = jnp.dot(h, w13_ref[:, dff + f0:dff + f1], preferred_element_type=F32)
        act = (u_gate * jax.nn.sigmoid(u_gate) * u_up).astype(BF16)
        part = jnp.dot(act, w2_ref[f0:f1, :], preferred_element_type=F32)
        acc = part if acc is None else acc + part
    y = x + mod_ref[5:6, :] * acc
    if final:
        y = y * lax.rsqrt(jnp.mean(y * y, axis=-1, keepdims=True) + EPS) * fg_ref[...]
    o_ref[...] = y


def _merge_ffn(x2d, mod, proj, b_gate, ya, yb, yc, w_branch, w_out, g, final_g, w13, w2, layer, seq, final):
    t, d = x2d.shape
    dff = w2.shape[1]
    assert FFN_CHUNKS[0][0] == 0 and FFN_CHUNKS[-1][1] == dff
    tm = min(TM_FFN, seq)
    ybs = lambda: pl.BlockSpec((BRANCH_WIDTH // LANES, tm, LANES), lambda i: (0, i, 0))
    resident = lambda *shape: pl.BlockSpec((None,) + shape, lambda i: (layer,) + (0,) * len(shape),
                                           pipeline_mode=pl.Buffered(1))
    return pl.pallas_call(
        functools.partial(_merge_ffn_kernel, final=final),
        grid=(t // tm,),
        in_specs=[
            pl.BlockSpec((tm, d), lambda i: (i, 0)),
            pl.BlockSpec((None, None, 6, d), lambda i: (layer, (i * tm) // seq, 0, 0)),
            pl.BlockSpec((GATE_BLKS, tm, LANES), lambda i: (0, i, 0)),
            pl.BlockSpec((None, 1, GATE_COLS), lambda i: (layer, 0, 0)),
            ybs(), ybs(), ybs(),
            resident(N_BRANCH, BRANCH_WIDTH, d),
            resident(d, d),
            pl.BlockSpec((None, 1, d), lambda i: (layer, 0, 0)),
            pl.BlockSpec((1, d), lambda i: (0, 0)),
            resident(d, 2 * dff),
            resident(dff, d),
        ],
        out_specs=pl.BlockSpec((tm, d), lambda i: (i, 0)),
        out_shape=jax.ShapeDtypeStruct((t, d), F32),
        compiler_params=_cparams("parallel"),
        name="merge_ffn",
    )(x2d, mod, proj, b_gate, ya, yb, yc, w_branch, w_out, g, final_g, w13, w2)


def _prep_w_in(w_in):
    scale = np.ones((IN_COLS,), np.float32)
    scale[0 * BRANCH_WIDTH:1 * BRANCH_WIDTH] = HEAD_DIM ** -0.5 * LOG2E
    scale[3 * BRANCH_WIDTH:4 * BRANCH_WIDTH] = HEAD_DIM ** -0.5 * LOG2E
    scale[6 * BRANCH_WIDTH:7 * BRANCH_WIDTH] = C_HEAD_DIM ** -0.5 * LOG2E
    w = w_in * jnp.asarray(scale)
    return jnp.concatenate([w[..., QKV_COLS:], w[..., :QKV_COLS]], axis=-1).astype(BF16)


def kernel(x, c, w_ada, b_ada, norm1_g, w_in, b_gate, lam_q1, lam_k1, lam_q2, lam_k2, subln_g,
           t5_table, rel_bias_b, w_branch, w_out, norm2_g, w13, w2, final_g):
    bsz, seq, d = x.shape
    depth = w_in.shape[0]
    t = bsz * seq
    assert d == D_MODEL and w_in.shape[2] == IN_COLS and w2.shape[1] == D_FF
    assert seq % (2 * B_PAD) == 0, "sequence tiles assume a multiple of 1024 frames"

    mod = _ada_mod(c, w_ada, b_ada).reshape(depth, bsz, 6, d)
    w_in_b = _prep_w_in(w_in)
    w_branch_b = w_branch.astype(BF16)
    w_out_b = w_out.astype(BF16)
    w13_b = w13.astype(BF16)
    w2_b = w2.astype(BF16)
    tq_a = min(TQ_A, seq // 2)
    a_bias, a_bmax = _a_bias_tiles(t5_table, tq_a, min(TK_A, tq_a))
    b_bias, b_bmax = _b_bias_tiles(rel_bias_b, TQ_B)
    fg = final_g.reshape(1, d)
    row = lambda p: p.reshape(depth, 1, -1)
    g1, g2, bg = row(norm1_g), row(norm2_g), row(b_gate)
    lams = [row(p) for p in (lam_q1, lam_k1, lam_q2, lam_k2)]
    subg = subln_g.reshape(depth, -1, 1)

    x2d = x.reshape(t, d)
    for l in range(depth):
        lam_init = 0.8 - 0.6 * math.exp(-0.3 * l)
        proj = _inproj(x2d, mod, g1, w_in_b, l, seq)
        ya = _attn_a(proj, a_bias, a_bmax, *lams, subg, l, bsz, seq, lam_init)
        yb = _attn_b(proj, b_bias, b_bmax, l, bsz, seq)
        yc = _attn_c(proj, bsz, seq)
        x2d = _merge_ffn(x2d, mod, proj, bg, ya, yb, yc, w_branch_b, w_out_b, g2, fg, w13_b, w2_b, l, seq,
                         final=(l == depth - 1))
    return x2d.reshape(bsz, seq, d)
```

```python
import functools
import math

import numpy as np
import jax
import jax.numpy as jnp
from jax import lax
from jax.experimental import pallas as pl
from jax.experimental.pallas import tpu as pltpu

F32 = jnp.float32
BF16 = jnp.bfloat16

D_MODEL = 1024
CHUNK = 64
HEAD_DIM = 64
BRANCH_WIDTH = 512
N_BRANCH = 3
A_HEADS = 4
B_HEADS = 8
C_HEADS = 4
C_HEAD_DIM = 128
B_LEFT_CHUNKS = 8
B_PAD = B_LEFT_CHUNKS * CHUNK
B_MAX_REL = 128
T5_BUCKETS = 32
T5_MAX_DIST = 128
D_FF = 2816
QKV_COLS = 3 * N_BRANCH * BRANCH_WIDTH
GATE_COLS = N_BRANCH * D_MODEL
IN_COLS = QKV_COLS + GATE_COLS
NEG_INF = -1e30
EPS = 1e-6
LOG2E = math.log2(math.e)

LANES = 128
GATE_BLKS = GATE_COLS // LANES
AQ0, AK0, AV0 = GATE_BLKS, GATE_BLKS + 4, GATE_BLKS + 8
BQ0, BK0, BV0 = GATE_BLKS + 12, GATE_BLKS + 16, GATE_BLKS + 20
CQ0, CK0, CV0 = GATE_BLKS + 24, GATE_BLKS + 28, GATE_BLKS + 32

TM_PROJ = 512
TN_PROJ = 1536
TM_FFN = 512
FFN_CHUNKS = ((0, 1536), (1536, 2816))
TQ_A = 1024
TK_A = 512
TP_A = 512
UNROLL_A = 2
NORM_SLACK_A = 1.02
L_MIN_A = 2.0 ** -100
TQ_B = 512
UNROLL_B = 3
TQ_C = 256
GROUP_C = 16
C_SKIP_LOG2 = -150.0
VMEM_LIMIT = 56 * 1024 * 1024


def _cparams(*sem):
    return pltpu.CompilerParams(dimension_semantics=sem, vmem_limit_bytes=VMEM_LIMIT)


def _ada_kernel(c_ref, w_ref, b_ref, o_ref):
    c = c_ref[...]
    cs = c * jax.nn.sigmoid(c)
    o_ref[...] = jnp.dot(cs, w_ref[...], preferred_element_type=F32,
                         precision=lax.Precision.HIGHEST) + b_ref[...]


def _ada_mod(c, w_ada, b_ada):
    depth, d, e = w_ada.shape
    bsz = c.shape[0]
    nblk = e // d
    return pl.pallas_call(
        _ada_kernel,
        grid=(depth, nblk),
        in_specs=[
            pl.BlockSpec((bsz, d), lambda l, j: (0, 0)),
            pl.BlockSpec((None, d, d), lambda l, j: (l, 0, j)),
            pl.BlockSpec((None, 1, d), lambda l, j: (l, 0, j)),
        ],
        out_specs=pl.BlockSpec((None, bsz, d), lambda l, j: (l, 0, j)),
        out_shape=jax.ShapeDtypeStruct((depth, bsz, e), F32),
        compiler_params=_cparams("arbitrary", "arbitrary"),
        name="ada_mod",
    )(c, w_ada, b_ada.reshape(depth, 1, e))


def _norm_mod(x, g, shift, scale):
    ms = jnp.mean(x * x, axis=-1, keepdims=True)
    y = x * lax.rsqrt(ms + EPS) * g
    return y * (1.0 + scale) + shift


def _inproj_kernel(x_ref, mod_ref, g_ref, w_ref, o_ref):
    h = _norm_mod(x_ref[...], g_ref[...], mod_ref[0:1, :], mod_ref[1:2, :]).astype(BF16)
    n = w_ref.shape[1]
    for c0 in range(0, n, TN_PROJ):
        res = jnp.dot(h, w_ref[:, c0:c0 + TN_PROJ], preferred_element_type=F32)
        for cb in range(TN_PROJ // LANES):
            o_ref[c0 // LANES + cb] = res[:, cb * LANES:(cb + 1) * LANES].astype(o_ref.dtype)


def _inproj(x2d, mod, g, w, layer, seq):
    t, d = x2d.shape
    n = w.shape[2]
    assert n % TN_PROJ == 0
    tm = min(TM_PROJ, seq)
    return pl.pallas_call(
        _inproj_kernel,
        grid=(t // tm,),
        in_specs=[
            pl.BlockSpec((tm, d), lambda i: (i, 0)),
            pl.BlockSpec((None, None, 6, d), lambda i: (layer, (i * tm) // seq, 0, 0)),
            pl.BlockSpec((None, 1, d), lambda i: (layer, 0, 0)),
            pl.BlockSpec((None, d, n), lambda i: (layer, 0, 0), pipeline_mode=pl.Buffered(1)),
        ],
        out_specs=pl.BlockSpec((n // LANES, tm, LANES), lambda i: (0, i, 0)),
        out_shape=jax.ShapeDtypeStruct((n // LANES, t, LANES), BF16),
        compiler_params=_cparams("parallel"),
        name="in_proj",
    )(x2d, mod, g, w)


def _t5_bucket_np(rel):
    nb = T5_BUCKETS // 2
    max_exact = nb // 2
    ret = np.where(rel > 0, nb, 0)
    n = np.abs(rel)
    nf = np.maximum(n, 1).astype(np.float32)
    scaled = (np.log(nf / np.float32(max_exact)) / np.float32(math.log(T5_MAX_DIST / max_exact))
              * np.float32(nb - max_exact))
    large = max_exact + scaled.astype(np.int32)
    large = np.minimum(large, nb - 1)
    return (ret + np.where(n < max_exact, n, large)).astype(np.int32)


def _toeplitz(vec, rows, cols):
    length = rows + cols - 1
    assert vec.shape[-1] == length
    lead = vec.shape[:-1]
    ext = jnp.concatenate([vec, jnp.zeros(lead + (1,), vec.dtype)], axis=-1)
    flat = jnp.broadcast_to(ext[..., None, :], lead + (rows, length + 1)).reshape(lead + (rows * (length + 1),))
    return flat[..., :rows * length].reshape(lead + (rows, length))[..., rows - 1:rows - 1 + cols]


def _a_bias_tiles(t5_table, tq, tk):
    r = np.arange(tq)[None, :]
    c = np.arange(tk)[:, None]
    far_bucket = _t5_bucket_np((c - 2 * tk) - r)
    assert (far_bucket == far_bucket[0, 0]).all() and far_bucket[0, 0] == _t5_bucket_np(np.array(-10 * tq))
    far = t5_table[int(far_bucket[0, 0])].astype(F32)
    tiles = [jnp.zeros((A_HEADS, tk, tq), F32)]
    bmax = jnp.zeros((A_HEADS,), F32)
    u = np.arange(tk + tq - 1)
    for koff in range(-tk, tq, tk):
        vec = t5_table[_t5_bucket_np(koff + tk - 1 - u)].astype(F32).T
        vec = (vec - far[:, None]) * LOG2E
        bmax = jnp.maximum(bmax, jnp.max(vec, axis=1))
        bias = _toeplitz(vec, tk, tq)
        allowed = ((c + koff) // CHUNK) <= (r // CHUNK)
        tiles.append(jnp.where(jnp.asarray(allowed)[None], bias, NEG_INF))
    tiles.append(jnp.full((A_HEADS, tk, tq), NEG_INF, F32))
    return jnp.stack(tiles, axis=1), jnp.broadcast_to(bmax[:, None, None], (A_HEADS, 1, LANES))


def _attn_a_kernel(lq1_ref, lk1_ref, lq2_ref, lk2_ref, subg_ref, bmax_ref, bias_ref, q_ref, k_ref, v_ref,
                   o_ref, vt_ref, acc_ref, *s_refs, tq, tk, lam_init):
    seq = q_ref.shape[0]
    nq = seq // tq
    nkb = seq // tk
    per_q = tq // tk
    unroll = len(s_refs)
    tp = TP_A
    lam = (jnp.exp(jnp.sum(lq1_ref[...] * lk1_ref[...], axis=-1, keepdims=True))
           - jnp.exp(jnp.sum(lq2_ref[...] * lk2_ref[...], axis=-1, keepdims=True)) + lam_init)
    lane = lax.broadcasted_iota(jnp.int32, (1, LANES), 1)
    mask1 = (lane < HEAD_DIM).astype(BF16)
    mask2 = (lane >= HEAD_DIM).astype(BF16)
    subg = subg_ref[...] * (1.0 - lam_init)
    sel = (lax.broadcasted_iota(jnp.int32, (8, LANES), 0)
           == lax.broadcasted_iota(jnp.int32, (8, LANES), 1) // HEAD_DIM).astype(BF16)

    def prepare(b, kn2):
        c0 = pl.multiple_of(b * tk, tk)
        vt_ref[:, pl.ds(c0, tk)] = jnp.transpose(v_ref[pl.ds(c0, tk), :])
        kb = k_ref[pl.ds(c0, tk), :]
        blk = lax.dot_general(sel, kb * kb, (((1,), (1,)), ((), ())), preferred_element_type=F32)
        return jnp.maximum(kn2, blk)

    kn2 = lax.fori_loop(0, nkb, prepare, jnp.zeros((8, tk), F32))
    kmax2 = jnp.max(kn2, axis=1, keepdims=True) * NORM_SLACK_A
    kmax = jnp.sqrt(jnp.concatenate([jnp.broadcast_to(kmax2[0:1], (1, tq)),
                                     jnp.broadcast_to(kmax2[1:2], (1, tq))], axis=1))

    def finish(i, acc, l):
        o = acc * (1.0 / l)
        d = o[:, :tq] - lam * o[:, tq:]
        y = d * lax.rsqrt(jnp.mean(d * d, axis=0, keepdims=True) + EPS) * subg
        o_ref[pl.ds(pl.multiple_of(i * tq, tq), tq), :] = jnp.transpose(y).astype(o_ref.dtype)

    def load_q(i):
        qb = q_ref[pl.ds(pl.multiple_of(i * tq, tq), tq), :]
        return jnp.concatenate([qb * mask1, qb * mask2], axis=0)

    def steps(i, step, st):
        n_plain = jnp.maximum(per_q * i - 1, 0) // unroll
        n_steps = (per_q * (i + 1) + unroll - 1) // unroll
        st = lax.fori_loop(0, n_plain, lambda it, s_: step(it, s_, False), st)
        return lax.fori_loop(n_plain, n_steps, lambda it, s_: step(it, s_, True), st)

    def bounded_pass(i):
        qq = load_q(i)
        qn2 = lax.dot_general(jnp.ones((8, LANES), BF16), qq * qq, (((1,), (1,)), ((), ())),
                              preferred_element_type=F32)[0:1]
        shift = jnp.sqrt(qn2 * NORM_SLACK_A) * kmax + (bmax_ref[:, 0:1] + 1.0)
        acc_ref[...] = jnp.zeros(acc_ref.shape, F32)

        def block(js, l, bias, late_queries_only=False):
            c0 = pl.multiple_of(js * tk, tk)
            kb = k_ref[pl.ds(c0, tk), :]
            vt = vt_ref[:, pl.ds(c0, tk)]
            sums = []
            for m0 in (0, tq):
                cols = slice(m0 + tq // 2, m0 + tq) if late_queries_only else slice(m0, m0 + tq)
                s = lax.dot_general(kb, qq[cols], (((1,), (1,)), ((), ())), preferred_element_type=F32)
                if bias is not None:
                    s = s + jnp.concatenate([bias, bias], axis=1)[:, cols]
                p = jnp.exp2(s - shift[:, cols])
                ps = jnp.sum(p, axis=0, keepdims=True)
                if late_queries_only:
                    ps = jnp.concatenate([jnp.zeros((1, tq // 2), F32), ps], axis=1)
                sums.append(ps)
                acc_ref[:, cols] += jnp.dot(vt, p.astype(BF16), preferred_element_type=F32)
            return l + jnp.concatenate(sums, axis=1)

        def plain_blocks(first, count, l):
            for u in range(count):
                l = block(first + u, l, None)
            return l

        n_pairs = jnp.maximum(i - 1, 0)
        l = lax.fori_loop(0, n_pairs // 2, lambda it, l: plain_blocks(4 * it, 4, l),
                          jnp.zeros((1, 2 * tq), F32))
        l = lax.cond(n_pairs % 2 == 1, lambda l: plain_blocks(2 * n_pairs - 2, 2, l), lambda l: l, l)

        def own_blocks(l):
            return block(2 * i + 1, block(2 * i, l, bias_ref[2]), bias_ref[3], late_queries_only=True)

        def previous_and_own_blocks(l):
            return own_blocks(block(2 * i - 1, block(2 * i - 2, l, None), bias_ref[1]))

        l = lax.cond(i > 0, previous_and_own_blocks, own_blocks, l)
        finish(i, acc_ref[...], l)
        return jnp.min(l)

    def running_max_pass(i):
        qq = load_q(i)
        acc_ref[...] = jnp.zeros(acc_ref.shape, F32)

        def block(js, st, with_bias, s_ref):
            m, l = st
            c0 = pl.multiple_of(jnp.minimum(js, nkb - 1) * tk, tk)
            kb = k_ref[pl.ds(c0, tk), :]
            s = lax.dot_general(kb, qq, (((1,), (1,)), ((), ())), preferred_element_type=F32)
            if with_bias:
                bias = bias_ref[jnp.clip(js - per_q * i + 2, 0, per_q + 2)]
                s = s + jnp.concatenate([bias, bias], axis=1)
            s_ref[...] = s
            m_new = jnp.maximum(m, jnp.max(s, axis=0, keepdims=True))
            alpha = jnp.exp2(m - m_new)
            sums = []
            for ct in range(2 * tq // tp):
                cols = slice(ct * tp, (ct + 1) * tp)
                p = jnp.exp2(s_ref[:, cols] - m_new[:, cols])
                sums.append(jnp.sum(p, axis=0, keepdims=True))
                vt = vt_ref[:, pl.ds(c0, tk)]
                acc_ref[:, cols] = (alpha[:, cols] * acc_ref[:, cols]
                                    + jnp.dot(vt, p.astype(BF16), preferred_element_type=F32))
            return m_new, alpha * l + jnp.concatenate(sums, axis=1)

        def step(it, st, with_bias):
            for u in range(unroll):
                st = block(it * unroll + u, st, with_bias, s_refs[u])
            return st

        _, l = steps(i, step, (jnp.full((1, 2 * tq), NEG_INF, F32), jnp.zeros((1, 2 * tq), F32)))
        finish(i, acc_ref[...], l)

    def qtile(i, carry):
        l_min = bounded_pass(i)

        @pl.when(jnp.logical_not(l_min > L_MIN_A))
        def _():
            running_max_pass(i)

        return carry

    lax.fori_loop(0, nq, qtile, 0)


def _attn_a(proj, bias, bmax, lq1, lk1, lq2, lk2, subg, layer, bsz, seq, lam_init):
    tq = min(TQ_A, seq // 2)
    tk = min(TK_A, tq)
    assert tk == TP_A and tq == 2 * tk and UNROLL_A == 2
    vec = lambda n: pl.BlockSpec((None, 1, n), lambda b, h: (layer, 0, 0))
    col = lambda base: pl.BlockSpec((None, seq, LANES), lambda b, h: (base + h, b, 0))
    return pl.pallas_call(
        functools.partial(_attn_a_kernel, tq=tq, tk=tk, lam_init=lam_init),
        grid=(bsz, A_HEADS),
        in_specs=[vec(HEAD_DIM), vec(HEAD_DIM), vec(HEAD_DIM), vec(HEAD_DIM),
                  pl.BlockSpec((None, 2 * HEAD_DIM, 1), lambda b, h: (layer, 0, 0)),
                  pl.BlockSpec((None, 1, LANES), lambda b, h: (h, 0, 0)),
                  pl.BlockSpec((None, tq // tk + 3, tk, tq), lambda b, h: (h, 0, 0, 0)),
                  col(AQ0), col(AK0), col(AV0)],
        out_specs=pl.BlockSpec((None, seq, LANES), lambda b, h: (h, b, 0)),
        out_shape=jax.ShapeDtypeStruct((A_HEADS, bsz * seq, LANES), BF16),
        scratch_shapes=[pltpu.VMEM((LANES, seq), BF16), pltpu.VMEM((LANES, 2 * tq), F32)]
        + [pltpu.VMEM((tk, 2 * tq), F32) for _ in range(UNROLL_A)],
        compiler_params=_cparams("parallel", "parallel"),
        name="attn_a",
    )(lq1, lk1, lq2, lk2, subg, bmax, bias, proj, proj, proj)


def _b_bias_tiles(rel_bias, tq):
    depth = rel_bias.shape[0]
    assert B_PAD % tq == 0
    r = np.arange(tq)[None, :]
    c = np.arange(tq)[:, None]
    lo = CHUNK * (r // CHUNK)
    u = np.arange(2 * tq - 1)
    tiles = []
    for koff in range(-B_PAD, tq, tq):
        in_band = (c + koff >= lo - B_PAD) & (c + koff < lo + CHUNK)
        idx = np.clip(u - (tq - 1) - koff, -B_MAX_REL, B_MAX_REL) + B_MAX_REL
        bias = _toeplitz(rel_bias[..., idx].astype(F32) * LOG2E, tq, tq)
        bias = jnp.where(jnp.asarray(in_band), bias, NEG_INF)
        tiles.append(bias.reshape(depth, B_HEADS // 2, 2, tq, 2, tq // 2).transpose(0, 1, 3, 4, 2, 5)
                     .reshape(depth, B_HEADS // 2, tq, 2 * tq))
    bmax = jnp.max((rel_bias.astype(F32) * LOG2E).reshape(depth, B_HEADS // 2, -1), axis=2)
    return (jnp.stack(tiles, axis=2),
            jnp.broadcast_to(bmax[:, :, None, None], (depth, B_HEADS // 2, 1, LANES)))


def _attn_b_kernel(bmax_ref, bias_ref, q_ref, k_ref, v_ref, o_ref, vt_ref, *s_refs, tq):
    seq = q_ref.shape[0]
    nq = seq // tq
    max_prev = B_PAD // tq
    lane = lax.broadcasted_iota(jnp.int32, (1, LANES), 1)
    mask1 = (lane < HEAD_DIM).astype(BF16)
    mask2 = (lane >= HEAD_DIM).astype(BF16)
    hq = tq // 2
    feat = lax.broadcasted_iota(jnp.int32, (LANES, hq), 0)
    sel = (lax.broadcasted_iota(jnp.int32, (8, LANES), 0)
           == lax.broadcasted_iota(jnp.int32, (8, LANES), 1) // HEAD_DIM).astype(BF16)

    def prepare(b, kn2):
        c0 = pl.multiple_of(b * tq, tq)
        vt_ref[:, pl.ds(c0, tq)] = jnp.transpose(v_ref[pl.ds(c0, tq), :])
        kb = k_ref[pl.ds(c0, tq), :]
        blk = lax.dot_general(sel, kb * kb, (((1,), (1,)), ((), ())), preferred_element_type=F32)
        return jnp.maximum(kn2, blk)

    kn2 = lax.fori_loop(0, nq, prepare, jnp.zeros((8, tq), F32))
    kmax2 = jnp.max(kn2, axis=1, keepdims=True) * NORM_SLACK_A
    kmax = jnp.sqrt(jnp.concatenate([jnp.broadcast_to(kmax2[h:h + 1], (1, hq)) for h in (0, 1, 0, 1)],
                                    axis=1))

    def tile_blocks(i, n_prev):
        static = isinstance(i, int)
        r0 = i * tq if static else pl.multiple_of(i * tq, tq)
        qb = q_ref[pl.ds(r0, tq), :]
        qq = jnp.concatenate([qb[:hq] * mask1, qb[:hq] * mask2, qb[hq:] * mask1, qb[hq:] * mask2], axis=0)
        starts = [r0 - d * tq for d in range(n_prev, 0, -1)] + [r0]
        blocks = tuple((c0 if static else pl.multiple_of(c0, tq), max_prev - n_prev + j)
                       for j, c0 in enumerate(starts))
        return r0, qq, blocks

    def finish(r0, acc, den):
        o = acc * (1.0 / den)
        o = jnp.concatenate([jnp.where(feat < HEAD_DIM, o[:, h * tq:h * tq + hq], o[:, h * tq + hq:(h + 1) * tq])
                             for h in (0, 1)], axis=1)
        o_ref[pl.ds(r0, tq), :] = jnp.transpose(o).astype(o_ref.dtype)

    def bounded_tile(i, n_prev):
        r0, qq, blocks = tile_blocks(i, n_prev)
        qn2 = lax.dot_general(jnp.ones((8, LANES), BF16), qq * qq, (((1,), (1,)), ((), ())),
                              preferred_element_type=F32)[0:1]
        shift = jnp.sqrt(qn2 * NORM_SLACK_A) * kmax + (bmax_ref[:, 0:1] + 1.0)
        if n_prev == 1 and not isinstance(i, int):
            prev, own = bias_ref[0], bias_ref[1]
            chains = ((r0 - tq, slice(0, tq), jnp.concatenate([prev[:, :tq], own[:hq, :tq]], axis=0)),
                      (r0 - hq, slice(tq, 2 * tq), jnp.concatenate([prev[hq:, tq:], own[:, tq:]], axis=0)))
            accs, dens = [], []
            for start, cols, bias in chains:
                start = pl.multiple_of(start, hq)
                s = lax.dot_general(k_ref[pl.ds(start, tq + hq), :], qq[cols], (((1,), (1,)), ((), ())),
                                    preferred_element_type=F32) + bias
                p = jnp.exp2(s - shift[:, cols])
                dens.append(jnp.sum(p, axis=0, keepdims=True))
                accs.append(jnp.dot(vt_ref[:, pl.ds(start, tq + hq)], p.astype(BF16),
                                    preferred_element_type=F32))
            acc, den = jnp.concatenate(accs, axis=1), jnp.concatenate(dens, axis=1)
        else:
            acc = None
            den = None
            for c0, t in blocks:
                s = lax.dot_general(k_ref[pl.ds(c0, tq), :], qq, (((1,), (1,)), ((), ())),
                                    preferred_element_type=F32) + bias_ref[t]
                p = jnp.exp2(s - shift)
                ps = jnp.sum(p, axis=0, keepdims=True)
                pv = jnp.dot(vt_ref[:, pl.ds(c0, tq)], p.astype(BF16), preferred_element_type=F32)
                den = ps if den is None else den + ps
                acc = pv if acc is None else acc + pv
        finish(r0, acc, den)
        return jnp.min(den)

    def exact_tile(i, refs, n_prev):
        r0, qq, blocks = tile_blocks(i, n_prev)
        m = None
        for (c0, t), s_ref in zip(blocks, refs):
            s = lax.dot_general(k_ref[pl.ds(c0, tq), :], qq, (((1,), (1,)), ((), ())),
                                preferred_element_type=F32) + bias_ref[t]
            s_ref[...] = s
            bm = jnp.max(s, axis=0, keepdims=True)
            m = bm if m is None else jnp.maximum(m, bm)
        acc = None
        den = None
        for (c0, t), s_ref in zip(blocks, refs):
            p = jnp.exp2(s_ref[...] - m)
            ps = jnp.sum(p, axis=0, keepdims=True)
            pv = jnp.dot(vt_ref[:, pl.ds(c0, tq)], p.astype(BF16), preferred_element_type=F32)
            den = ps if den is None else den + ps
            acc = pv if acc is None else acc + pv
        finish(r0, acc, den)

    def tiles(idx, n_prev):
        mins = [bounded_tile(i, n_prev) for i in idx]
        for i, l_min in zip(idx, mins):
            @pl.when(jnp.logical_not(l_min > L_MIN_A))
            def _():
                exact_tile(i, s_refs, n_prev)

    for i in range(min(max_prev, nq)):
        tiles([i], i)

    def step(it, carry):
        tiles([max_prev + UNROLL_B * it + u for u in range(UNROLL_B)], max_prev)
        return carry

    n_rest = max(nq - max_prev, 0)
    lax.fori_loop(0, n_rest // UNROLL_B, step, 0)
    for i in range(nq - n_rest % UNROLL_B, nq):
        tiles([i], max_prev)


def _attn_b(proj, bias, bmax, layer, bsz, seq):
    tq = TQ_B
    col = lambda base: pl.BlockSpec((None, seq, LANES), lambda b, h: (base + h, b, 0))
    return pl.pallas_call(
        functools.partial(_attn_b_kernel, tq=tq),
        grid=(bsz, B_HEADS // 2),
        in_specs=[pl.BlockSpec((None, None, 1, LANES), lambda b, h: (layer, h, 0, 0)),
                  pl.BlockSpec((None, None, B_PAD // tq + 1, tq, 2 * tq), lambda b, h: (layer, h, 0, 0, 0)),
                  col(BQ0), col(BK0), col(BV0)],
        out_specs=pl.BlockSpec((None, seq, LANES), lambda b, h: (h, b, 0)),
        out_shape=jax.ShapeDtypeStruct((B_HEADS // 2, bsz * seq, LANES), BF16),
        scratch_shapes=[pltpu.VMEM((LANES, seq), BF16)]
        + [pltpu.VMEM((tq, 2 * tq), F32) for _ in range(B_PAD // tq + 1)],
        compiler_params=_cparams("parallel", "parallel"),
        name="attn_b",
    )(bmax, bias, proj, proj, proj)


def _attn_c_kernel(q_ref, k_ref, v_ref, o_ref, *, tq, group):
    seq = q_ref.shape[0]
    nq = seq // tq
    row = lax.broadcasted_iota(jnp.int32, (tq, tq), 0)
    colm = lax.broadcasted_iota(jnp.int32, (tq, tq), 1)
    tri_strict = (row > colm).astype(BF16)
    before = colm < row

    def block(i, j, carry, acc, diag):
        qb = q_ref[pl.ds(pl.multiple_of(i * tq, tq), tq), :]
        c0 = pl.multiple_of(j * tq, tq)
        kb = k_ref[pl.ds(c0, tq), :]
        vb = v_ref[pl.ds(c0, tq), :]
        z = lax.dot_general(qb, kb, (((1,), (1,)), ((), ())), preferred_element_type=F32)
        lm = -(jnp.maximum(z, 0.0) + jnp.log2(1.0 + jnp.exp2(-jnp.abs(z))))
        if diag:
            lm = jnp.where(before, lm, 0.0)
        lm_b = lm.astype(BF16)
        excl = jnp.dot(lm_b, tri_strict, preferred_element_type=F32)
        logw = (z + lm) + excl + carry
        wgt = jnp.exp2(logw)
        if diag:
            wgt = jnp.where(before, wgt, 0.0)
        acc = acc + jnp.dot(wgt.astype(BF16), vb, preferred_element_type=F32)
        carry = carry + excl[:, 0:1] + lm_b[:, 0:1].astype(F32)
        return carry, acc

    def qgroup(gi, c):
        i0 = gi * group
        sts = [block(i0 + g, i0 + g, jnp.zeros((tq, 1), F32), jnp.zeros((tq, LANES), F32), True)
               for g in range(group)]
        carries = tuple(st[0] for st in sts)
        accs = tuple(st[1] for st in sts)

        def live(carries):
            return functools.reduce(jnp.maximum, [jnp.max(cr) for cr in carries])

        def cond(st):
            t, cmax, _, _ = st
            return jnp.logical_and(t <= i0 + group - 1, cmax > C_SKIP_LOG2)

        def body(st):
            t, _, carries, accs = st
            new = []
            for g in range(group):
                j = i0 + g - t
                cin = jnp.where(j >= 0, carries[g], NEG_INF)
                new.append(block(i0 + g, jnp.maximum(j, 0), cin, accs[g], False))
            carries = tuple(st_[0] for st_ in new)
            return t + 1, live(carries), carries, tuple(st_[1] for st_ in new)

        first = body((jnp.int32(1), jnp.float32(0.0), carries, accs))
        _, _, _, accs = lax.while_loop(cond, body, first)
        for g in range(group):
            o_ref[pl.ds(pl.multiple_of((i0 + g) * tq, tq), tq), :] = accs[g].astype(o_ref.dtype)
        return c

    lax.fori_loop(0, nq // group, qgroup, 0)


def _attn_c(proj, bsz, seq):
    tq = min(TQ_C, seq)
    group = min(GROUP_C, seq // tq)
    assert (seq // tq) % group == 0
    col = lambda base: pl.BlockSpec((None, seq, LANES), lambda b, h: (base + h, b, 0))
    return pl.pallas_call(
        functools.partial(_attn_c_kernel, tq=tq, group=group),
        grid=(bsz, C_HEADS),
        in_specs=[col(CQ0), col(CK0), col(CV0)],
        out_specs=pl.BlockSpec((None, seq, LANES), lambda b, h: (h, b, 0)),
        out_shape=jax.ShapeDtypeStruct((C_HEADS, bsz * seq, LANES), BF16),
        compiler_params=_cparams("parallel", "parallel"),
        name="attn_c",
    )(proj, proj, proj)


def _cat_lanes(ref, start, count):
    return jnp.concatenate([ref[start + c] for c in range(count)], axis=1)


def _merge_ffn_kernel(x_ref, mod_ref, gate_ref, bg_ref, ya_ref, yb_ref, yc_ref, wb_ref, wo_ref,
                      g_ref, fg_ref, w13_ref, w2_ref, o_ref, *, final):
    blks = D_MODEL // LANES
    dff = w2_ref.shape[0]
    merged = None
    for r, y_ref in enumerate((ya_ref, yb_ref, yc_ref)):
        y = _cat_lanes(y_ref, 0, BRANCH_WIDTH // LANES)
        br = jnp.dot(y, wb_ref[r], preferred_element_type=F32)
        pre = _cat_lanes(gate_ref, r * blks, blks).astype(F32) + bg_ref[:, r * D_MODEL:(r + 1) * D_MODEL]
        term = jax.nn.sigmoid(pre) * br
        merged = term if merged is None else merged + term
    out = jnp.dot(merged.astype(BF16), wo_ref[...], preferred_element_type=F32)
    x = x_ref[...] + mod_ref[2:3, :] * out

    h = _norm_mod(x, g_ref[...], mod_ref[3:4, :], mod_ref[4:5, :]).astype(BF16)
    acc = None
    for f0, f1 in FFN_CHUNKS:
        u_gate = jnp.dot(h, w13_ref[:, f0:f1], preferred_element_type=F32)
        u_up = jnp.dot(h, w13_ref[:, dff + f0:dff + f1], preferred_element_type=F32)
        act = (u_gate * jax.nn.sigmoid(u_gate) * u_up).astype(BF16)
        part = jnp.dot(act, w2_ref[f0:f1, :], preferred_element_type=F32)
        acc = part if acc is None else acc + part
    y = x + mod_ref[5:6, :] * acc
    if final:
        y = y * lax.rsqrt(jnp.mean(y * y, axis=-1, keepdims=True) + EPS) * fg_ref[...]
    o_ref[...] = y


def _merge_ffn(x2d, mod, proj, b_gate, ya, yb, yc, w_branch, w_out, g, final_g, w13, w2, layer, seq, final):
    t, d = x2d.shape
    dff = w2.shape[1]
    assert FFN_CHUNKS[0][0] == 0 and FFN_CHUNKS[-1][1] == dff
    tm = min(TM_FFN, seq)
    ybs = lambda: pl.BlockSpec((BRANCH_WIDTH // LANES, tm, LANES), lambda i: (0, i, 0))
    resident = lambda *shape: pl.BlockSpec((None,) + shape, lambda i: (layer,) + (0,) * len(shape),
                                           pipeline_mode=pl.Buffered(1))
    return pl.pallas_call(
        functools.partial(_merge_ffn_kernel, final=final),
        grid=(t // tm,),
        in_specs=[
            pl.BlockSpec((tm, d), lambda i: (i, 0)),
            pl.BlockSpec((None, None, 6, d), lambda i: (layer, (i * tm) // seq, 0, 0)),
            pl.BlockSpec((GATE_BLKS, tm, LANES), lambda i: (0, i, 0)),
            pl.BlockSpec((None, 1, GATE_COLS), lambda i: (layer, 0, 0)),
            ybs(), ybs(), ybs(),
            resident(N_BRANCH, BRANCH_WIDTH, d),
            resident(d, d),
            pl.BlockSpec((None, 1, d), lambda i: (layer, 0, 0)),
            pl.BlockSpec((1, d), lambda i: (0, 0)),
            resident(d, 2 * dff),
            resident(dff, d),
        ],
        out_specs=pl.BlockSpec((tm, d), lambda i: (i, 0)),
        out_shape=jax.ShapeDtypeStruct((t, d), F32),
        compiler_params=_cparams("parallel"),
        name="merge_ffn",
    )(x2d, mod, proj, b_gate, ya, yb, yc, w_branch, w_out, g, final_g, w13, w2)


def _prep_w_in(w_in):
    scale = np.ones((IN_COLS,), np.float32)
    scale[0 * BRANCH_WIDTH:1 * BRANCH_WIDTH] = HEAD_DIM ** -0.5 * LOG2E
    scale[3 * BRANCH_WIDTH:4 * BRANCH_WIDTH] = HEAD_DIM ** -0.5 * LOG2E
    scale[6 * BRANCH_WIDTH:7 * BRANCH_WIDTH] = C_HEAD_DIM ** -0.5 * LOG2E
    w = w_in * jnp.asarray(scale)
    return jnp.concatenate([w[..., QKV_COLS:], w[..., :QKV_COLS]], axis=-1).astype(BF16)


def kernel(x, c, w_ada, b_ada, norm1_g, w_in, b_gate, lam_q1, lam_k1, lam_q2, lam_k2, subln_g,
           t5_table, rel_bias_b, w_branch, w_out, norm2_g, w13, w2, final_g):
    bsz, seq, d = x.shape
    depth = w_in.shape[0]
    t = bsz * seq
    assert d == D_MODEL and w_in.shape[2] == IN_COLS and w2.shape[1] == D_FF
    assert seq % (2 * B_PAD) == 0, "sequence tiles assume a multiple of 1024 frames"

    mod = _ada_mod(c, w_ada, b_ada).reshape(depth, bsz, 6, d)
    w_in_b = _prep_w_in(w_in)
    w_branch_b = w_branch.astype(BF16)
    w_out_b = w_out.astype(BF16)
    w13_b = w13.astype(BF16)
    w2_b = w2.astype(BF16)
    tq_a = min(TQ_A, seq // 2)
    a_bias, a_bmax = _a_bias_tiles(t5_table, tq_a, min(TK_A, tq_a))
    b_bias, b_bmax = _b_bias_tiles(rel_bias_b, TQ_B)
    fg = final_g.reshape(1, d)
    row = lambda p: p.reshape(depth, 1, -1)
    g1, g2, bg = row(norm1_g), row(norm2_g), row(b_gate)
    lams = [row(p) for p in (lam_q1, lam_k1, lam_q2, lam_k2)]
    subg = subln_g.reshape(depth, -1, 1)

    x2d = x.reshape(t, d)
    for l in range(depth):
        lam_init = 0.8 - 0.6 * math.exp(-0.3 * l)
        proj = _inproj(x2d, mod, g1, w_in_b, l, seq)
        ya = _attn_a(proj, a_bias, a_bmax, *lams, subg, l, bsz, seq, lam_init)
        yb = _attn_b(proj, b_bias, b_bmax, l, bsz, seq)
        yc = _attn_c(proj, bsz, seq)
        x2d = _merge_ffn(x2d, mod, proj, bg, ya, yb, yc, w_branch_b, w_out_b, g2, fg, w13_b, w2_b, l, seq,
                         final=(l == depth - 1))
    return x2d.reshape(bsz, seq, d)
```
